```python
import math
import jax, jax.numpy as jnp
from jax import lax
import numpy as np

D_MODEL = 2048
BATCH = 1
SEQ = 8192
DEPTH = 4

GRID_W = 64
CTX_LEN = 256
HEAD_DIM = 128
RMS_EPS = 1e-6
N_BRANCH = 4
BRANCH_W = 512
NA_HEADS = 4
NA_WIN_R = 8
NA_WIN_C = 16
NA_QBLOCK = 16
NA_KBLOCK = 32
NA_W = NA_HEADS * HEAD_DIM
SSD_HEADS = 8
SSD_HEAD_DIM = 64
SSD_GROUPS = 2
SSD_STATE = 128
SSD_CONV = 5
SSD_CHUNK = 128
SSD_W = SSD_HEADS * SSD_HEAD_DIM
SSD_XBC_W = SSD_W + 2 * SSD_GROUPS * SSD_STATE
SSD_DT_W = 2 * SSD_HEADS
GQA_HEADS = 4
GQA_KV_HEADS = 2
GQA_W = GQA_HEADS * HEAD_DIM
GQA_KV_W = GQA_KV_HEADS * HEAD_DIM
GQA_QBLOCK = 128
ROPE_THETA = 10000.0
FNET_GROUPS = 4
FNET_GROUP_DIM = 128
FNET_W = FNET_GROUPS * FNET_GROUP_DIM
PROJ_SPLITS = (NA_W, NA_W, NA_W, SSD_W, SSD_XBC_W, SSD_DT_W, GQA_W, GQA_KV_W, GQA_KV_W, FNET_W, N_BRANCH * D_MODEL)
PROJ_W = sum(PROJ_SPLITS)
N_EXPERTS = 16
EXPERT_FF = D_MODEL // 2
CAPACITY_FACTOR = 2

kernel_name = 'hybrid_gated_parallel_mixers_ec_moe_dit'


def rms_norm(x, w):
    xf = x.astype(jnp.float32)
    y = xf * lax.rsqrt(jnp.mean(xf * xf, axis=-1, keepdims=True) + RMS_EPS)
    return (y * w.astype(jnp.float32)).astype(x.dtype)


def ada_mod(cvec, w, b):
    m = jax.nn.silu(cvec) @ w + b
    return [t[..., None, :] for t in jnp.split(m, 6, axis=-1)]


def split_proj(p):
    return jnp.split(p, np.cumsum(PROJ_SPLITS)[:-1].tolist(), axis=-1)


def heads(t, nh):
    return t.reshape(t.shape[0], t.shape[1], nh, -1)


def axial_rope_tables(n_tok):
    t = jnp.arange(n_tok)
    pos = jnp.stack([t // GRID_W, t % GRID_W], axis=-1).astype(jnp.float32)
    n_freq = HEAD_DIM // 4
    inv = ROPE_THETA ** (-jnp.arange(n_freq, dtype=jnp.float32) / n_freq)
    ang = pos[:, :, None] * inv
    return jnp.cos(ang), jnp.sin(ang)


def apply_rope(x, cos, sin):
    b, n, h, d = x.shape
    xf = x.astype(jnp.float32).reshape(b, n, h, 2, 2, d // 4)
    re, im = xf[..., 0, :], xf[..., 1, :]
    c, s = cos[None, :, None], sin[None, :, None]
    out = jnp.stack([re * c - im * s, re * s + im * c], axis=-2)
    return out.reshape(b, n, h, d).astype(x.dtype)


def head_rms_rope(raw, nh, gain, rope):
    t = rms_norm(heads(raw, nh), gain)
    return t if rope is None else apply_rope(t, *rope)


def mha(q, k, v):
    b, nq, hq, d = q.shape
    hkv = k.shape[2]
    qg = q.reshape(b, nq, hkv, hq // hkv, d)
    s = jnp.einsum('bqgrd,bkgd->bgrqk', qg, k).astype(jnp.float32) * d ** -0.5
    p = jax.nn.softmax(s, axis=-1).astype(v.dtype)
    return jnp.einsum('bgrqk,bkgd->bqgrd', p, v).reshape(b, nq, hq * d)


def gqa_latent(q, k, v, kc, vc):
    b, n = q.shape[:2]
    k_all = jnp.concatenate([kc, k], axis=1)
    v_all = jnp.concatenate([vc, v], axis=1)
    nb = n // GQA_QBLOCK
    qb = q.reshape(b, nb, GQA_QBLOCK, GQA_HEADS, HEAD_DIM).transpose(1, 0, 2, 3, 4)
    ob = lax.map(lambda qq: mha(qq, k_all, v_all), qb)
    return ob.transpose(1, 0, 2, 3).reshape(b, n, GQA_W)


def _na_indices(rows):
    kr = min(NA_WIN_R, rows)
    ncb = GRID_W // NA_QBLOCK
    r = np.arange(rows)
    row_start = np.clip(r - kr // 2, 0, rows - kr)
    krow = row_start[:, None] + np.arange(kr)
    qcol = np.arange(ncb)[:, None] * NA_QBLOCK + np.arange(NA_QBLOCK)
    kcol_start = np.clip(np.arange(ncb) * NA_QBLOCK - (NA_KBLOCK - NA_QBLOCK) // 2, 0, GRID_W - NA_KBLOCK)
    kcol = kcol_start[:, None] + np.arange(NA_KBLOCK)
    key_idx = (krow[:, None, :, None] * GRID_W + kcol[None, :, None, :]).reshape(-1)
    col_start = np.clip(qcol - NA_WIN_C // 2, 0, GRID_W - NA_WIN_C)
    kc3 = kcol[:, None, :]
    col_ok = (kc3 >= col_start[..., None]) & (kc3 < col_start[..., None] + NA_WIN_C)
    bias_r = krow - r[:, None] + NA_WIN_R - 1
    bias_c = np.clip(kc3 - qcol[:, :, None] + NA_WIN_C - 1, 0, 2 * NA_WIN_C - 2)
    return kr, key_idx, bias_r, bias_c, col_ok


def na_latent(q, k, v, kc, vc, rpb):
    b, n, nh, hd = q.shape
    rows = n // GRID_W
    ncb = GRID_W // NA_QBLOCK
    kr, key_idx, bias_r, bias_c, col_ok = _na_indices(rows)
    nk = kr * NA_KBLOCK
    bias = rpb.astype(jnp.float32)[:, bias_r[:, None, None, :, None], bias_c[None, :, :, None, :]]
    bias = jnp.where(col_ok[None, None, :, :, None, :], bias, -jnp.inf).reshape(nh, rows, ncb, NA_QBLOCK, nk)
    kg = jnp.take(k, key_idx, axis=1).reshape(b, rows, ncb, nk, nh, hd)
    vg = jnp.take(v, key_idx, axis=1).reshape(b, rows, ncb, nk, nh, hd)
    qb = q.reshape(b, rows, ncb, NA_QBLOCK, nh, hd)
    scale = hd ** -0.5
    s_win = jnp.einsum('brjqhd,brjkhd->bhrjqk', qb, kg).astype(jnp.float32) * scale + bias[None]
    s_ctx = jnp.einsum('brjqhd,bchd->bhrjqc', qb, kc).astype(jnp.float32) * scale
    p = jax.nn.softmax(jnp.concatenate([s_win, s_ctx], axis=-1), axis=-1).astype(v.dtype)
    o = jnp.einsum('bhrjqk,brjkhd->brjqhd', p[..., :nk], vg) + jnp.einsum('bhrjqc,bchd->brjqhd', p[..., nk:], vc)
    return o.reshape(b, n, nh * hd)


def dwconv(u, w, bias):
    ch = u.shape[-1]
    out = lax.conv_general_dilated(u, w.astype(u.dtype)[:, None, :], window_strides=(1,),
                                   padding=((SSD_CONV // 2, SSD_CONV // 2),),
                                   dimension_numbers=('NWC', 'WIO', 'NWC'), feature_group_count=ch)
    return out + bias.astype(u.dtype)


def ssd_prepare(xbc, dt_raw, conv_w, conv_b, dt_bias):
    b, n, _ = xbc.shape
    u = jax.nn.silu(dwconv(xbc, conv_w, conv_b))
    xs, bm, cm = jnp.split(u, [SSD_W, SSD_W + SSD_GROUPS * SSD_STATE], axis=-1)
    xs = xs.reshape(b, n, SSD_HEADS, SSD_HEAD_DIM)
    bm = bm.reshape(b, n, SSD_GROUPS, SSD_STATE)
    cm = cm.reshape(b, n, SSD_GROUPS, SSD_STATE)
    dt = jax.nn.softplus(dt_raw.astype(jnp.float32) + dt_bias.reshape(-1).astype(jnp.float32)).reshape(b, n, 2, SSD_HEADS)
    return xs, bm, cm, dt[:, :, 0], dt[:, :, 1]


def ssd_chunked(x, dt, a_log, b_mat, c_mat, h0, with_output):
    bsz, n, nh, hp = x.shape
    g, ns = b_mat.shape[2], b_mat.shape[3]
    e = nh // g
    nc = n // SSD_CHUNK
    f32 = jnp.float32
    a_neg = -jnp.exp(a_log.astype(f32)).reshape(g, e)
    xr = x.astype(f32).reshape(bsz, nc, SSD_CHUNK, g, e, hp)
    dtr = dt.astype(f32).reshape(bsz, nc, SSD_CHUNK, g, e)
    bc = b_mat.astype(f32).reshape(bsz, nc, SSD_CHUNK, g, ns)
    cc = c_mat.astype(f32).reshape(bsz, nc, SSD_CHUNK, g, ns)
    a_cum = jnp.cumsum(dtr * a_neg, axis=2)
    a_tot = a_cum[:, :, -1]
    x_dt = xr * dtr[..., None]
    states = jnp.einsum('bclgn,bclge,bclgep->bcgepn', bc, jnp.exp(a_tot[:, :, None] - a_cum), x_dt)

    def carry_step(h, inp):
        st, at = inp
        return h * jnp.exp(at)[..., None, None] + st, h

    h_fin, h_start = lax.scan(carry_step, h0, (jnp.moveaxis(states, 1, 0), jnp.moveaxis(a_tot, 1, 0)))
    if not with_output:
        return None, h_fin
    h_start = jnp.moveaxis(h_start, 0, 1)
    causal = np.tril(np.ones((SSD_CHUNK, SSD_CHUNK), dtype=bool))
    seg = a_cum[:, :, :, None] - a_cum[:, :, None, :]
    decay = jnp.exp(jnp.where(causal[:, :, None, None], seg, -jnp.inf))
    cb = jnp.einsum('bclgn,bcsgn->bclsg', cc, bc)
    y = (jnp.einsum('bclsg,bclsge,bcsgep->bclgep', cb, decay, x_dt)
         + jnp.einsum('bclgn,bclge,bcgepn->bclgep', cc, jnp.exp(a_cum), h_start))
    return y.reshape(bsz, n, nh, hp), h_fin


def ssd_bidir(prep, a_log, h0_f, h0_b, with_output):
    xs, bm, cm, dt_f, dt_b = prep
    flip = lambda t: jnp.flip(t, axis=1)
    y_f, h_f = ssd_chunked(xs, dt_f, a_log[0], bm, cm, h0_f, with_output)
    y_b, h_b = ssd_chunked(flip(xs), flip(dt_b), a_log[1], flip(bm), flip(cm), h0_b, with_output)
    y = y_f + flip(y_b) if with_output else None
    return y, h_f, h_b


def ssd_output(y, xs, z, d_skip, norm_w):
    b, n = xs.shape[:2]
    y = y + d_skip.astype(jnp.float32)[:, None] * xs.astype(jnp.float32)
    gated = (y * jax.nn.silu(z.astype(jnp.float32)).reshape(b, n, SSD_HEADS, SSD_HEAD_DIM))
    gated = gated.reshape(b, n, SSD_GROUPS, SSD_W // SSD_GROUPS)
    normed = gated * lax.rsqrt(jnp.mean(gated * gated, axis=-1, keepdims=True) + RMS_EPS)
    return (normed.reshape(b, n, SSD_W) * norm_w.astype(jnp.float32)).astype(xs.dtype)


def fourier_mix(u):
    b, n, _ = u.shape
    uf = u.astype(jnp.float32).reshape(b, n, FNET_GROUPS, FNET_GROUP_DIM)
    out = jnp.fft.fft2(uf, axes=(1, 3), norm='ortho').real
    return out.reshape(b, n, FNET_W).astype(u.dtype)


def merge(branches, gates_raw, w_branch, w_out):
    g = jax.nn.sigmoid(gates_raw.astype(jnp.float32)).reshape(gates_raw.shape[0], gates_raw.shape[1], N_BRANCH, D_MODEL)
    y = g[..., 0, :] * (branches[0] @ w_branch[0]).astype(jnp.float32)
    for i in range(1, N_BRANCH):
        y = y + g[..., i, :] * (branches[i] @ w_branch[i]).astype(jnp.float32)
    return y.astype(gates_raw.dtype) @ w_out


def expert_choice(h, w_router, w_gate, w_up, w_down):
    b, n, d = h.shape
    cap = CAPACITY_FACTOR * n // N_EXPERTS
    aff = jax.nn.softmax((h @ w_router).astype(jnp.float32), axis=-1)
    gate, idx = lax.top_k(jnp.swapaxes(aff, 1, 2), cap)
    xs = jax.vmap(lambda hb, ib: hb[ib])(h, idx)
    hid = jax.nn.silu(jnp.einsum('becd,edf->becf', xs, w_gate)) * jnp.einsum('becd,edf->becf', xs, w_up)
    ye = jnp.einsum('becf,efd->becd', hid, w_down) * gate[..., None].astype(h.dtype)
    return jax.vmap(lambda yb, ib: jnp.zeros((n, d), h.dtype).at[ib.reshape(-1)].add(yb.reshape(-1, d).astype(h.dtype)))(ye, idx)


def layer_forward(x, xc, c, c_ctx, w_ada, b_ada, norm_mix, norm_ffn, w_in, na_rpb, conv_w, conv_b, a_log,
                  dt_bias, d_skip, ssd_gn, q_gain, k_gain, w_branch, w_out, w_router, w_gate, w_up, w_down,
                  rope, update_ctx):
    bsz = x.shape[0]
    sh1, sc1, g1, sh2, sc2, g2 = ada_mod(c, w_ada, b_ada)
    csh1, csc1, cg1, csh2, csc2, cg2 = ada_mod(c_ctx, w_ada, b_ada)
    h = rms_norm(x, norm_mix) * (1 + sc1) + sh1
    hc = rms_norm(xc, norm_mix) * (1 + csc1) + csh1
    na_q, na_k, na_v, s_z, s_xbc, s_dt, g_q, g_k, g_v, f_in, gates = split_proj(h @ w_in)
    na_qc, na_kc, na_vc, s_zc, s_xbcc, s_dtc, g_qc, g_kc, g_vc, f_inc, gates_c = split_proj(hc @ w_in)

    kc_na, vc_na = heads(na_kc, NA_HEADS), heads(na_vc, NA_HEADS)
    o_na = na_latent(heads(na_q, NA_HEADS), heads(na_k, NA_HEADS), heads(na_v, NA_HEADS), kc_na, vc_na, na_rpb)

    h0 = jnp.zeros((bsz, SSD_GROUPS, SSD_HEADS // SSD_GROUPS, SSD_HEAD_DIM, SSD_STATE), jnp.float32)
    prep_c = ssd_prepare(s_xbcc, s_dtc, conv_w, conv_b, dt_bias)
    y_c, hf_c, hb_c = ssd_bidir(prep_c, a_log, h0, h0, update_ctx)
    prep = ssd_prepare(s_xbc, s_dt, conv_w, conv_b, dt_bias)
    y, _, _ = ssd_bidir(prep, a_log, hf_c, hb_c, True)
    o_ssd = ssd_output(y, prep[0], s_z, d_skip, ssd_gn)

    kc_g, vc_g = head_rms_rope(g_kc, GQA_KV_HEADS, k_gain, None), heads(g_vc, GQA_KV_HEADS)
    o_gqa = gqa_latent(head_rms_rope(g_q, GQA_HEADS, q_gain, rope), head_rms_rope(g_k, GQA_KV_HEADS, k_gain, rope),
                       heads(g_v, GQA_KV_HEADS), kc_g, vc_g)

    o_fn = fourier_mix(f_in)

    x = x + g1 * merge((o_na, o_ssd, o_gqa, o_fn), gates, w_branch, w_out)
    h2 = rms_norm(x, norm_ffn) * (1 + sc2) + sh2
    x = x + g2 * expert_choice(h2, w_router, w_gate, w_up, w_down)

    if update_ctx:
        o_na_c = mha(heads(na_qc, NA_HEADS), kc_na, vc_na)
        o_ssd_c = ssd_output(y_c, prep_c[0], s_zc, d_skip, ssd_gn)
        o_gqa_c = mha(head_rms_rope(g_qc, GQA_HEADS, q_gain, None), kc_g, vc_g)
        o_fn_c = fourier_mix(f_inc)
        xc = xc + cg1 * merge((o_na_c, o_ssd_c, o_gqa_c, o_fn_c), gates_c, w_branch, w_out)
        hc2 = rms_norm(xc, norm_ffn) * (1 + csc2) + csh2
        xc = xc + cg2 * expert_choice(hc2, w_router, w_gate, w_up, w_down)
    return x, xc


def setup_inputs(seed: int = 0) -> dict:
    key = jax.random.key(seed)
    ks = jax.random.split(key, 26)
    f32 = jnp.float32
    L = DEPTH
    nrm = lambda k, shape, scale: jax.random.normal(k, shape, f32) * scale
    dt0 = jnp.exp(jax.random.uniform(ks[10], (L, 2, SSD_HEADS), f32, math.log(1e-3), math.log(1e-1)))
    return {
        'x': nrm(ks[0], (BATCH, SEQ, D_MODEL), 1.0),
        'c': nrm(ks[1], (BATCH, D_MODEL), 1.0),
        'ctx': nrm(ks[2], (BATCH, CTX_LEN, D_MODEL), 1.0),
        'c_ctx': nrm(ks[3], (D_MODEL,), 1.0),
        'w_ada': nrm(ks[4], (L, D_MODEL, 6 * D_MODEL), 0.5 * D_MODEL ** -0.5),
        'b_ada': nrm(ks[5], (L, 6 * D_MODEL), 0.01),
        'norm_mix': 1.0 + nrm(ks[6], (L, D_MODEL), 0.02),
        'norm_ffn': 1.0 + nrm(ks[7], (L, D_MODEL), 0.02),
        'w_in': nrm(ks[8], (L, D_MODEL, PROJ_W), D_MODEL ** -0.5),
        'na_rpb': nrm(ks[9], (L, NA_HEADS, 2 * NA_WIN_R - 1, 2 * NA_WIN_C - 1), 0.1),
        'ssd_conv_w': nrm(ks[11], (L, SSD_CONV, SSD_XBC_W), SSD_CONV ** -0.5),
        'ssd_conv_b': nrm(ks[12], (L, SSD_XBC_W), 0.01),
        'ssd_a_log': jnp.log(jax.random.uniform(ks[13], (L, 2, SSD_HEADS), f32, 1.0, 16.0)),
        'ssd_dt_bias': dt0 + jnp.log(-jnp.expm1(-dt0)),
        'ssd_d': 1.0 + nrm(ks[14], (L, SSD_HEADS), 0.1),
        'ssd_norm': 1.0 + nrm(ks[15], (L, SSD_W), 0.02),
        'gqa_q_norm': 1.0 + nrm(ks[16], (L, HEAD_DIM), 0.02),
        'gqa_k_norm': 1.0 + nrm(ks[17], (L, HEAD_DIM), 0.02),
        'w_branch': nrm(ks[18], (L, N_BRANCH, BRANCH_W, D_MODEL), BRANCH_W ** -0.5),
        'w_out': nrm(ks[19], (L, D_MODEL, D_MODEL), D_MODEL ** -0.5),
        'w_router': nrm(ks[20], (L, D_MODEL, N_EXPERTS), D_MODEL ** -0.5),
        'moe_w_gate': nrm(ks[21], (L, N_EXPERTS, D_MODEL, EXPERT_FF), D_MODEL ** -0.5),
        'moe_w_up': nrm(ks[22], (L, N_EXPERTS, D_MODEL, EXPERT_FF), D_MODEL ** -0.5),
        'moe_w_down': nrm(ks[23], (L, N_EXPERTS, EXPERT_FF, D_MODEL), EXPERT_FF ** -0.5),
        'final_norm': 1.0 + nrm(ks[24], (D_MODEL,), 0.02),
    }


def reference(x, c, ctx, c_ctx, w_ada, b_ada, norm_mix, norm_ffn, w_in, na_rpb, ssd_conv_w, ssd_conv_b,
              ssd_a_log, ssd_dt_bias, ssd_d, ssd_norm, gqa_q_norm, gqa_k_norm, w_branch, w_out, w_router,
              moe_w_gate, moe_w_up, moe_w_down, final_norm):
    rope = axial_rope_tables(x.shape[1])
    xc = ctx
    for l in range(DEPTH):
        x, xc = layer_forward(x, xc, c, c_ctx, w_ada[l], b_ada[l], norm_mix[l], norm_ffn[l], w_in[l], na_rpb[l],
                              ssd_conv_w[l], ssd_conv_b[l], ssd_a_log[l], ssd_dt_bias[l], ssd_d[l], ssd_norm[l],
                              gqa_q_norm[l], gqa_k_norm[l], w_branch[l], w_out[l], w_router[l], moe_w_gate[l],
                              moe_w_up[l], moe_w_down[l], rope, l < DEPTH - 1)
    return rms_norm(x, final_norm)
```

```python
import functools
import math

import numpy as np
import jax
import jax.numpy as jnp
from jax import lax
from jax.experimental import pallas as pl
from jax.experimental.pallas import tpu as pltpu

F32 = jnp.float32
BF16 = jnp.bfloat16

D_MODEL = 2048
GRID_W = 64
HEAD_DIM = 128
RMS_EPS = 1e-6
N_BRANCH = 4
BRANCH_W = 512
NA_HEADS = 4
NA_WIN_R = 8
NA_WIN_C = 16
SSD_HEADS = 8
SSD_HEAD_DIM = 64
SSD_GROUPS = 2
SSD_STATE = 128
SSD_CONV = 5
SSD_CHUNK = 128
SSD_W = SSD_HEADS * SSD_HEAD_DIM
SSD_XBC_W = SSD_W + 2 * SSD_GROUPS * SSD_STATE
GQA_HEADS = 4
GQA_KV_HEADS = 2
ROPE_THETA = 10000.0
FNET_GROUPS = 4
FNET_GROUP_DIM = 128
N_EXPERTS = 16
EXPERT_FF = D_MODEL // 2
CAPACITY_FACTOR = 2
LOG2E = 1.4426950408889634

VMEM_LIMIT_BYTES = 56 * 1024 * 1024
LANES = 128

FB_XBC, FB_Z, FB_Q, FB_K, FB_DT, FB_FN, FB_W = 0, 1024, 1536, 2048, 2304, 2560, 3072
AB_W = 1792


def _cparams(sem):
    return pltpu.CompilerParams(dimension_semantics=sem, vmem_limit_bytes=VMEM_LIMIT_BYTES)


def _silu(x):
    return x * jax.nn.sigmoid(x)


def _ada_kernel(c_ref, w_ref, b_ref, o_ref):
    a = _silu(c_ref[...]).astype(BF16)
    o_ref[0] = jnp.dot(a, w_ref[0].astype(BF16), preferred_element_type=F32) + b_ref[0]


def ada_all(cc, w_ada, b_ada):
    depth, d, n = w_ada.shape
    tn = 1024
    return pl.pallas_call(
        _ada_kernel,
        grid=(depth, n // tn),
        in_specs=[pl.BlockSpec((8, d), lambda l, j: (0, 0)),
                  pl.BlockSpec((1, d, tn), lambda l, j: (l, 0, j)),
                  pl.BlockSpec((1, 1, tn), lambda l, j: (l, 0, j))],
        out_specs=pl.BlockSpec((1, 8, tn), lambda l, j: (l, 0, j)),
        out_shape=jax.ShapeDtypeStruct((depth, 8, n), F32),
        compiler_params=_cparams(("arbitrary", "arbitrary")),
        name="ada_mod",
    )(cc, w_ada, b_ada.reshape(depth, 1, n))


def _row_select(mod_ref, chunk, row_is_ctx):
    lo = chunk * D_MODEL
    return jnp.where(row_is_ctx, mod_ref[1:2, lo:lo + D_MODEL], mod_ref[0:1, lo:lo + D_MODEL])


def _norm_body(x_ref, nw_ref):
    x = x_ref[...]
    ms = jnp.mean(x * x, axis=-1, keepdims=True)
    return x * lax.rsqrt(ms + RMS_EPS) * nw_ref[...]


def _norm_mod_kernel(x_ref, nw_ref, mod_ref, o_ref, *, n_ctx, tm, sh_chunk):
    y = _norm_body(x_ref, nw_ref)
    row = pl.program_id(0) * tm + lax.broadcasted_iota(jnp.int32, (tm, 1), 0)
    is_ctx = row < n_ctx
    sh = _row_select(mod_ref, sh_chunk, is_ctx)
    sc = _row_select(mod_ref, sh_chunk + 1, is_ctx)
    o_ref[...] = (y * (1.0 + sc) + sh).astype(o_ref.dtype)


def norm_mod(x, nw, mod, n_ctx, sh_chunk, tm=768):
    m, d = x.shape
    return pl.pallas_call(
        functools.partial(_norm_mod_kernel, n_ctx=n_ctx, tm=tm, sh_chunk=sh_chunk),
        grid=(m // tm,),
        in_specs=[pl.BlockSpec((tm, d), lambda i: (i, 0)),
                  pl.BlockSpec((1, d), lambda i: (0, 0)),
                  pl.BlockSpec(mod.shape, lambda i: (0, 0))],
        out_specs=pl.BlockSpec((tm, d), lambda i: (i, 0)),
        out_shape=jax.ShapeDtypeStruct((m, d), BF16),
        compiler_params=_cparams(("arbitrary",)),
        name="norm_mod",
    )(x, nw.reshape(1, d), mod)


def _norm_router_kernel(x_ref, nw_ref, mod_ref, wr_ref, h_ref, aff_ref, *, n_ctx, tm, sh_chunk):
    y = _norm_body(x_ref, nw_ref)
    row = pl.program_id(0) * tm + lax.broadcasted_iota(jnp.int32, (tm, 1), 0)
    is_ctx = row < n_ctx
    sh = _row_select(mod_ref, sh_chunk, is_ctx)
    sc = _row_select(mod_ref, sh_chunk + 1, is_ctx)
    h = (y * (1.0 + sc) + sh).astype(BF16)
    h_ref[...] = h
    logits = jnp.dot(h, wr_ref[...].astype(BF16), preferred_element_type=F32)
    lane = lax.broadcasted_iota(jnp.int32, logits.shape, 1)
    logits = jnp.where(lane < N_EXPERTS, logits, -jnp.inf)
    mx = jnp.max(logits, axis=-1, keepdims=True)
    e = jnp.exp(logits - mx)
    aff_ref[...] = e / jnp.sum(e, axis=-1, keepdims=True)


def norm_router(x, nw, mod, w_router_pad, n_ctx, sh_chunk, tm=768):
    m, d = x.shape
    return pl.pallas_call(
        functools.partial(_norm_router_kernel, n_ctx=n_ctx, tm=tm, sh_chunk=sh_chunk),
        grid=(m // tm,),
        in_specs=[pl.BlockSpec((tm, d), lambda i: (i, 0)),
                  pl.BlockSpec((1, d), lambda i: (0, 0)),
                  pl.BlockSpec(mod.shape, lambda i: (0, 0)),
                  pl.BlockSpec((d, LANES), lambda i: (0, 0))],
        out_specs=[pl.BlockSpec((tm, d), lambda i: (i, 0)),
                   pl.BlockSpec((tm, LANES), lambda i: (i, 0))],
        out_shape=[jax.ShapeDtypeStruct((m, d), BF16),
                   jax.ShapeDtypeStruct((m, LANES), F32)],
        compiler_params=_cparams(("arbitrary",)),
        name="norm_router",
    )(x, nw.reshape(1, d), mod, w_router_pad)


def _final_norm_kernel(x_ref, nw_ref, o_ref):
    o_ref[...] = _norm_body(x_ref, nw_ref)


def final_norm(x, nw, row0, n_rows, tm=256):
    d = x.shape[1]
    off = row0 // tm
    return pl.pallas_call(
        _final_norm_kernel,
        grid=(n_rows // tm,),
        in_specs=[pl.BlockSpec((tm, d), lambda i: (i + off, 0)),
                  pl.BlockSpec((1, d), lambda i: (0, 0))],
        out_specs=pl.BlockSpec((tm, d), lambda i: (i, 0)),
        out_shape=jax.ShapeDtypeStruct((n_rows, d), F32),
        compiler_params=_cparams(("arbitrary",)),
        name="final_norm",
    )(x, nw.reshape(1, d))


def _mm_kernel(a_ref, w_ref, o_ref):
    o_ref[...] = jnp.dot(a_ref[...].astype(BF16), w_ref[...].astype(BF16),
                         preferred_element_type=F32).astype(o_ref.dtype)


def matmul(a, w, out_dtype, tm, tn, name):
    m, k = a.shape
    n = w.shape[1]
    return pl.pallas_call(
        _mm_kernel,
        grid=(n // tn, m // tm),
        in_specs=[pl.BlockSpec((tm, k), lambda j, i: (i, 0)),
                  pl.BlockSpec((k, tn), lambda j, i: (0, j))],
        out_specs=pl.BlockSpec((tm, tn), lambda j, i: (i, j)),
        out_shape=jax.ShapeDtypeStruct((m, n), out_dtype),
        compiler_params=_cparams(("arbitrary", "arbitrary")),
        name=name,
    )(a, w)


def _mm_residual_kernel(a_ref, w_ref, x_ref, g_ref, o_ref, *, n_ctx, tm):
    acc = jnp.dot(a_ref[...].astype(BF16), w_ref[...].astype(BF16), preferred_element_type=F32)
    row = pl.program_id(1) * tm + lax.broadcasted_iota(jnp.int32, (tm, 1), 0)
    g = jnp.where(row < n_ctx, g_ref[1:2, :], g_ref[0:1, :])
    o_ref[...] = x_ref[...] + g * acc


def matmul_residual(a, w, x, mod, gate_chunk, n_ctx, tm, tn, name):
    m, k = a.shape
    n = w.shape[1]
    goff = gate_chunk * D_MODEL // tn
    return pl.pallas_call(
        functools.partial(_mm_residual_kernel, n_ctx=n_ctx, tm=tm),
        grid=(n // tn, m // tm),
        in_specs=[pl.BlockSpec((tm, k), lambda j, i: (i, 0)),
                  pl.BlockSpec((k, tn), lambda j, i: (0, j)),
                  pl.BlockSpec((tm, tn), lambda j, i: (i, j)),
                  pl.BlockSpec((8, tn), lambda j, i: (0, goff + j))],
        out_specs=pl.BlockSpec((tm, tn), lambda j, i: (i, j)),
        out_shape=jax.ShapeDtypeStruct((m, n), F32),
        compiler_params=_cparams(("arbitrary", "arbitrary")),
        name=name,
    )(a, w, x, mod)


def _softmax_pv(parts):
    m = None
    for s, _ in parts:
        mi = jnp.max(s, axis=-1, keepdims=True)
        m = mi if m is None else jnp.maximum(m, mi)
    ps, l = [], None
    for s, _ in parts:
        p = jnp.exp(s - m)
        ps.append(p)
        li = jnp.sum(p, axis=-1, keepdims=True)
        l = li if l is None else l + li
    inv = 1.0 / l
    o = None
    for p, (_, v) in zip(ps, parts):
        oi = jnp.dot((p * inv).astype(BF16), v, preferred_element_type=F32)
        o = oi if o is None else o + oi
    return o


def _qk(q, k):
    return lax.dot_general(q, k, (((1,), (1,)), ((), ())), preferred_element_type=F32)


def _na_kernel(q_ref, k_ref, v_ref, b_ref, o_ref, *, n_ctx, rows):
    step = pl.program_id(0)
    n_ctx_steps = n_ctx // GRID_W
    scale = HEAD_DIM ** -0.5
    win = NA_WIN_R * GRID_W

    @pl.when(step < n_ctx_steps)
    def _ctx():
        for h in range(NA_HEADS):
            cs = slice(h * HEAD_DIM, (h + 1) * HEAD_DIM)
            q = q_ref[:, cs]
            s_c = _qk(q, k_ref[0:n_ctx, cs]) * scale
            o_ref[:, cs] = _softmax_pv([(s_c, v_ref[0:n_ctx, cs])]).astype(o_ref.dtype)

    @pl.when(step >= n_ctx_steps)
    def _lat():
        r = step - n_ctx_steps
        row_start = jnp.clip(r - NA_WIN_R // 2, 0, rows - NA_WIN_R)
        base = pl.multiple_of(n_ctx + row_start * GRID_W, GRID_W)
        for h in range(NA_HEADS):
            cs = slice(h * HEAD_DIM, (h + 1) * HEAD_DIM)
            q = q_ref[:, cs]
            s_w = _qk(q, k_ref[pl.ds(base, win), cs]) * scale + b_ref[h, 0]
            s_c = _qk(q, k_ref[0:n_ctx, cs]) * scale
            o = _softmax_pv([(s_w, v_ref[pl.ds(base, win), cs]), (s_c, v_ref[0:n_ctx, cs])])
            o_ref[:, cs] = o.astype(o_ref.dtype)


def na_bias_table(rpb):
    kc = np.arange(GRID_W)[None, :]
    qc = np.arange(GRID_W)[:, None]
    col_start = np.clip(qc - NA_WIN_C // 2, 0, GRID_W - NA_WIN_C)
    valid = (kc >= col_start) & (kc < col_start + NA_WIN_C)
    cidx = np.clip(kc - qc + NA_WIN_C - 1, 0, 2 * NA_WIN_C - 2)
    ridx = np.arange(NA_WIN_R)[:, None] + np.arange(NA_WIN_R)[None, :]
    t = rpb.astype(F32)[:, ridx[:, :, None, None], cidx[None, None]]
    t = jnp.where(valid[None, None, None], t, -jnp.inf)
    t = jnp.transpose(t, (0, 1, 3, 2, 4))
    return t.reshape(rpb.shape[0], NA_WIN_R, GRID_W, NA_WIN_R * GRID_W)


def na_attention(ab, bias, n_ctx):
    m = ab.shape[0]
    rows = (m - n_ctx) // GRID_W
    n_ctx_steps = n_ctx // GRID_W
    w = NA_HEADS * HEAD_DIM

    def bias_map(s):
        r = jnp.maximum(s - n_ctx_steps, 0)
        rs = jnp.clip(r - NA_WIN_R // 2, 0, rows - NA_WIN_R)
        return (0, rs - r + NA_WIN_R - 1, 0, 0)

    return pl.pallas_call(
        functools.partial(_na_kernel, n_ctx=n_ctx, rows=rows),
        grid=(m // GRID_W,),
        in_specs=[pl.BlockSpec((GRID_W, w), lambda s: (s, 0)),
                  pl.BlockSpec((m, w), lambda s: (0, 1)),
                  pl.BlockSpec((m, w), lambda s: (0, 2)),
                  pl.BlockSpec((NA_HEADS, 1, GRID_W, NA_WIN_R * GRID_W), bias_map)],
        out_specs=pl.BlockSpec((GRID_W, w), lambda s: (s, 0)),
        out_shape=jax.ShapeDtypeStruct((m, w), BF16),
        compiler_params=_cparams(("arbitrary",)),
        name="na_attn",
    )(ab, ab, ab, bias)


def rope_tables(n_ctx, n_lat):
    t = jnp.arange(n_lat)
    pos = jnp.stack([t // GRID_W, t % GRID_W], axis=-1).astype(F32)
    n_freq = HEAD_DIM // 4
    inv = ROPE_THETA ** (-jnp.arange(n_freq, dtype=F32) / n_freq)
    ang = pos[:, :, None] * inv
    c, s = jnp.cos(ang), jnp.sin(ang)
    cos = jnp.concatenate([c[:, 0], c[:, 0], c[:, 1], c[:, 1]], axis=-1)
    sin = jnp.concatenate([-s[:, 0], s[:, 0], -s[:, 1], s[:, 1]], axis=-1)
    cos = jnp.concatenate([jnp.ones((n_ctx, HEAD_DIM), F32), cos], axis=0)
    sin = jnp.concatenate([jnp.zeros((n_ctx, HEAD_DIM), F32), sin], axis=0)
    return cos, sin


def _qk_prep_kernel(x_ref, cos_ref, sin_ref, qg_ref, kg_ref, qo_ref, ko_ref):
    cos = cos_ref[...]
    sin = sin_ref[...]
    lane = lax.broadcasted_iota(jnp.int32, (1, HEAD_DIM), 1)
    first_half = (lane % (HEAD_DIM // 2)) < (HEAD_DIM // 4)
    for h in range(GQA_HEADS + GQA_KV_HEADS):
        x = x_ref[:, h * HEAD_DIM:(h + 1) * HEAD_DIM]
        gain = qg_ref[...] if h < GQA_HEADS else kg_ref[...]
        y = x * lax.rsqrt(jnp.mean(x * x, axis=-1, keepdims=True) + RMS_EPS) * gain
        sw = jnp.where(first_half, pltpu.roll(y, HEAD_DIM - HEAD_DIM // 4, 1), pltpu.roll(y, HEAD_DIM // 4, 1))
        out = (y * cos + sw * sin).astype(BF16)
        if h < GQA_HEADS:
            qo_ref[:, h * HEAD_DIM:(h + 1) * HEAD_DIM] = out
        else:
            hk = h - GQA_HEADS
            ko_ref[:, hk * HEAD_DIM:(hk + 1) * HEAD_DIM] = out


def qk_prep(fb, cos, sin, q_gain, k_gain, tm=768):
    m = fb.shape[0]
    wq, wk = GQA_HEADS * HEAD_DIM, GQA_KV_HEADS * HEAD_DIM
    return pl.pallas_call(
        _qk_prep_kernel,
        grid=(m // tm,),
        in_specs=[pl.BlockSpec((tm, wq + wk), lambda i: (i, FB_Q // (wq + wk))),
                  pl.BlockSpec((tm, HEAD_DIM), lambda i: (i, 0)),
                  pl.BlockSpec((tm, HEAD_DIM), lambda i: (i, 0)),
                  pl.BlockSpec((1, HEAD_DIM), lambda i: (0, 0)),
                  pl.BlockSpec((1, HEAD_DIM), lambda i: (0, 0))],
        out_specs=[pl.BlockSpec((tm, wq), lambda i: (i, 0)),
                   pl.BlockSpec((tm, wk), lambda i: (i, 0))],
        out_shape=[jax.ShapeDtypeStruct((m, wq), BF16),
                   jax.ShapeDtypeStruct((m, wk), BF16)],
        compiler_params=_cparams(("arbitrary",)),
        name="qk_prep",
    )(fb, cos, sin, q_gain.reshape(1, HEAD_DIM), k_gain.reshape(1, HEAD_DIM))


def _gqa_kernel(q_ref, k_ref, v_ref, o_ref, *, n_ctx, n_all, tq, tk):
    qi = pl.program_id(1)
    c1 = (HEAD_DIM ** -0.5) * LOG2E
    n_ctx_tiles = n_ctx // tq
    rep = GQA_HEADS // GQA_KV_HEADS

    def chunk(q, k, v, carry):
        m, l, acc = carry
        s = _qk(q, k)
        m_new = jnp.maximum(m, jnp.max(s, axis=-1, keepdims=True))
        p = jnp.exp2((s - m_new) * c1)
        alpha = jnp.exp2((m - m_new) * c1)
        l = alpha * l + jnp.sum(p, axis=-1, keepdims=True)
        acc = alpha * acc + jnp.dot(p.astype(BF16), v, preferred_element_type=F32)
        return m_new, l, acc

    def init():
        return (jnp.full((tq, 1), -jnp.inf, F32), jnp.zeros((tq, 1), F32), jnp.zeros((tq, HEAD_DIM), F32))

    @pl.when(qi < n_ctx_tiles)
    def _ctx():
        for r in range(rep):
            cs = slice(r * HEAD_DIM, (r + 1) * HEAD_DIM)
            _, l, acc = chunk(q_ref[:, cs], k_ref[0:n_ctx, :], v_ref[0:n_ctx, :], init())
            o_ref[:, cs] = (acc / l).astype(o_ref.dtype)

    @pl.when(qi >= n_ctx_tiles)
    def _lat():
        for r in range(rep):
            cs = slice(r * HEAD_DIM, (r + 1) * HEAD_DIM)
            q = q_ref[:, cs]

            def body(c, carry):
                off = pl.multiple_of(c * tk, tk)
                return chunk(q, k_ref[pl.ds(off, tk), :], v_ref[pl.ds(off, tk), :], carry)

            _, l, acc = lax.fori_loop(0, n_all // tk, body, init())
            o_ref[:, cs] = (acc / l).astype(o_ref.dtype)


def gqa_attention(qn, kn, ab, n_ctx, tq=256, tk=768):
    m = qn.shape[0]
    rep = GQA_HEADS // GQA_KV_HEADS
    v_off = (3 * NA_HEADS * HEAD_DIM) // HEAD_DIM
    if m % tk:
        tk = 128
    return pl.pallas_call(
        functools.partial(_gqa_kernel, n_ctx=n_ctx, n_all=m, tq=tq, tk=tk),
        grid=(GQA_KV_HEADS, m // tq),
        in_specs=[pl.BlockSpec((tq, rep * HEAD_DIM), lambda g, i: (i, g)),
                  pl.BlockSpec((m, HEAD_DIM), lambda g, i: (0, g)),
                  pl.BlockSpec((m, HEAD_DIM), lambda g, i: (0, v_off + g))],
        out_specs=pl.BlockSpec((tq, rep * HEAD_DIM), lambda g, i: (i, g)),
        out_shape=jax.ShapeDtypeStruct((m, GQA_HEADS * HEAD_DIM), BF16),
        compiler_params=_cparams(("arbitrary", "arbitrary")),
        name="gqa_attn",
    )(qn, kn, ab)


def _ssd_prep_kernel(prev_ref, x_ref, next_ref, dtr_ref, cw_ref, cb_ref, dtb_ref, u_ref, dt_ref, *, n_ctx, n_all, tm):
    i = pl.program_id(0)
    lo = i * tm
    hi = lo + tm
    top_ok = jnp.logical_and(lo != 0, lo != n_ctx)
    bot_ok = jnp.logical_and(hi != n_ctx, hi != n_all)
    prev = jnp.where(top_ok, prev_ref[...], 0.0)
    nxt = jnp.where(bot_ok, next_ref[...], 0.0)
    ext = jnp.concatenate([prev, x_ref[...], nxt], axis=0)
    half = SSD_CONV // 2
    acc = None
    for j in range(SSD_CONV):
        sl = ext[8 - half + j:8 - half + j + tm, :]
        term = sl * cw_ref[j:j + 1, :]
        acc = term if acc is None else acc + term
    u_ref[...] = _silu(acc + cb_ref[...])
    dt_ref[...] = jax.nn.softplus(dtr_ref[...] + dtb_ref[...])


def ssd_prep(fb, conv_w, conv_b, dt_bias, n_ctx, tm=256):
    m = fb.shape[0]
    nb8 = tm // 8
    last8 = m // 8 - 1
    cw = jnp.concatenate([conv_w, jnp.zeros((8 - SSD_CONV, SSD_XBC_W), F32)], axis=0)
    dtb = jnp.concatenate([dt_bias.reshape(-1), jnp.zeros((LANES - 2 * SSD_HEADS,), F32)]).reshape(1, LANES)
    return pl.pallas_call(
        functools.partial(_ssd_prep_kernel, n_ctx=n_ctx, n_all=m, tm=tm),
        grid=(m // tm,),
        in_specs=[pl.BlockSpec((8, SSD_XBC_W), lambda i: (jnp.maximum(i * nb8 - 1, 0), 0)),
                  pl.BlockSpec((tm, SSD_XBC_W), lambda i: (i, 0)),
                  pl.BlockSpec((8, SSD_XBC_W), lambda i: (jnp.minimum((i + 1) * nb8, last8), 0)),
                  pl.BlockSpec((tm, LANES), lambda i: (i, FB_DT // LANES)),
                  pl.BlockSpec((8, SSD_XBC_W), lambda i: (0, 0)),
                  pl.BlockSpec((1, SSD_XBC_W), lambda i: (0, 0)),
                  pl.BlockSpec((1, LANES), lambda i: (0, 0))],
        out_specs=[pl.BlockSpec((tm, SSD_XBC_W), lambda i: (i, 0)),
                   pl.BlockSpec((tm, LANES), lambda i: (i, 0))],
        out_shape=[jax.ShapeDtypeStruct((m, SSD_XBC_W), F32),
                   jax.ShapeDtypeStruct((m, LANES), F32)],
        compiler_params=_cparams(("arbitrary",)),
        name="ssd_prep",
    )(fb, fb, fb, fb, cw, conv_b.reshape(1, SSD_XBC_W), dtb)


def _ssd_scan_kernel(u_ref, dt_ref, alog_ref, y_ref, ht_ref):
    d = pl.program_id(0)
    s = pl.program_id(1)
    ln = SSD_CHUNK
    p = SSD_HEAD_DIM
    ns = SSD_STATE
    epg = SSD_HEADS // SSD_GROUPS

    @pl.when(s == 0)
    def _init():
        ht_ref[...] = jnp.zeros_like(ht_ref)

    fwd = d == 0
    dt_all = dt_ref[...]
    dt = jnp.where(fwd, dt_all, pltpu.roll(dt_all, LANES - SSD_HEADS, 1))
    a = dt * (-jnp.exp(alog_ref[0]))
    li = lax.broadcasted_iota(jnp.int32, (ln, ln), 0)
    si = lax.broadcasted_iota(jnp.int32, (ln, ln), 1)
    mask = jnp.where(fwd, li - si, si - li) >= 0
    tri = mask.astype(F32)
    a_cum = jnp.dot(tri, a, preferred_element_type=F32, precision=lax.Precision.HIGHEST)
    a_cum_t = a_cum.T
    a_tot = jnp.where(fwd, a_cum[ln - 1:ln, :], a_cum[0:1, :])
    w_all = jnp.exp(a_tot - a_cum)
    ea_all = jnp.exp(a_cum)
    eat = jnp.exp(a_tot)
    for g in range(SSD_GROUPS):
        bg = u_ref[:, SSD_W + g * ns:SSD_W + (g + 1) * ns]
        cg = u_ref[:, SSD_W + SSD_GROUPS * ns + g * ns:SSD_W + SSD_GROUPS * ns + (g + 1) * ns]
        cgb = cg.astype(BF16)
        cb = _qk(cgb, bg.astype(BF16))
        bgt = bg.T.astype(BF16)
        for e in range(epg):
            h = g * epg + e
            ac = a_cum[:, h:h + 1]
            act = a_cum_t[h:h + 1, :]
            decay = jnp.exp(jnp.where(mask, ac - act, -jnp.inf))
            mm = (cb * decay).astype(BF16)
            xdt = u_ref[:, h * p:(h + 1) * p] * dt[:, h:h + 1]
            ht = ht_ref[h]
            y = (jnp.dot(mm, xdt.astype(BF16), preferred_element_type=F32)
                 + ea_all[:, h:h + 1] * jnp.dot(cgb, ht.astype(BF16), preferred_element_type=F32))
            st = jnp.dot(bgt, (xdt * w_all[:, h:h + 1]).astype(BF16), preferred_element_type=F32)
            ht_ref[h] = eat[:, h:h + 1] * ht + st
            y_ref[0, :, h * p:(h + 1) * p] = y


def ssd_scan(u, dt, a_log, n_ctx):
    m = u.shape[0]
    nc = m // SSD_CHUNK
    ncc = n_ctx // SSD_CHUNK

    def chunk_of(d, s):
        bwd = jnp.where(s < ncc, ncc - 1 - s, ncc + nc - 1 - s)
        return jnp.where(d == 0, s, bwd)

    al = jnp.concatenate([a_log, jnp.zeros((2, LANES - SSD_HEADS), F32)], axis=1).reshape(2, 1, LANES)
    return pl.pallas_call(
        _ssd_scan_kernel,
        grid=(2, nc),
        in_specs=[pl.BlockSpec((SSD_CHUNK, SSD_XBC_W), lambda d, s: (chunk_of(d, s), 0)),
                  pl.BlockSpec((SSD_CHUNK, LANES), lambda d, s: (chunk_of(d, s), 0)),
                  pl.BlockSpec((1, 1, LANES), lambda d, s: (d, 0, 0))],
        out_specs=pl.BlockSpec((1, SSD_CHUNK, SSD_W), lambda d, s: (d, chunk_of(d, s), 0)),
        out_shape=jax.ShapeDtypeStruct((2, m, SSD_W), F32),
        scratch_shapes=[pltpu.VMEM((SSD_HEADS, SSD_STATE, SSD_HEAD_DIM), F32)],
        compiler_params=_cparams(("arbitrary", "arbitrary")),
        name="ssd_scan",
    )(u, dt, al)


def _ssd_out_kernel(yf_ref, yb_ref, xs_ref, z_ref, dsk_ref, nw_ref, o_ref):
    y = yf_ref[0] + yb_ref[0] + dsk_ref[...] * xs_ref[...]
    gated = y * _silu(z_ref[...])
    gw = SSD_W // SSD_GROUPS
    for g in range(SSD_GROUPS):
        blk = gated[:, g * gw:(g + 1) * gw]
        nrm = blk * lax.rsqrt(jnp.mean(blk * blk, axis=-1, keepdims=True) + RMS_EPS)
        o_ref[:, g * gw:(g + 1) * gw] = (nrm * nw_ref[:, g * gw:(g + 1) * gw]).astype(o_ref.dtype)


def ssd_out(y2, u, fb, d_skip, norm_w, tm=768):
    m = u.shape[0]
    dsk = jnp.repeat(d_skip.astype(F32), SSD_HEAD_DIM).reshape(1, SSD_W)
    return pl.pallas_call(
        _ssd_out_kernel,
        grid=(m // tm,),
        in_specs=[pl.BlockSpec((1, tm, SSD_W), lambda i: (0, i, 0)),
                  pl.BlockSpec((1, tm, SSD_W), lambda i: (1, i, 0)),
                  pl.BlockSpec((tm, SSD_W), lambda i: (i, 0)),
                  pl.BlockSpec((tm, SSD_W), lambda i: (i, FB_Z // SSD_W)),
                  pl.BlockSpec((1, SSD_W), lambda i: (0, 0)),
                  pl.BlockSpec((1, SSD_W), lambda i: (0, 0))],
        out_specs=pl.BlockSpec((tm, SSD_W), lambda i: (i, 0)),
        out_shape=jax.ShapeDtypeStruct((m, SSD_W), BF16),
        compiler_params=_cparams(("arbitrary",)),
        name="ssd_out",
    )(y2, y2, u, fb, dsk, norm_w.reshape(1, SSD_W))


def _dft_tables(n):
    ang = 2.0 * np.pi * np.outer(np.arange(n), np.arange(n)) / n
    return jnp.asarray(np.cos(ang), F32), jnp.asarray(np.sin(ang), F32)


def _hdot(a, b):
    return jnp.dot(a, b, preferred_element_type=F32, precision=lax.Precision.HIGHEST)


def _fnet_a_kernel(c_ref, s_ref, x_ref, yr_ref, yi_ref):
    x = x_ref[...]
    yr_ref[...] = _hdot(c_ref[...], x)
    yi_ref[...] = -_hdot(s_ref[...], x)


def _fnet_b_kernel(yr_ref, yi_ref, twc_ref, tws_ref, c1_ref, s1_ref, cc_ref, sc_ref, o_ref, *, scale, kb):
    c1, s1 = c1_ref[...], s1_ref[...]
    cc, sc = cc_ref[...], sc_ref[...]
    for j in range(kb):
        yr, yi = yr_ref[j], yi_ref[j]
        tc, ts = twc_ref[j], tws_ref[j]
        pr = yr * tc + yi * ts
        pi = yi * tc - yr * ts
        zr = _hdot(c1, pr) + _hdot(s1, pi)
        zi = _hdot(c1, pi) - _hdot(s1, pr)
        for g in range(FNET_GROUPS):
            cs = slice(g * FNET_GROUP_DIM, (g + 1) * FNET_GROUP_DIM)
            o_ref[:, j, cs] = (_hdot(zr[:, cs], cc) + _hdot(zi[:, cs], sc)) * scale


def _fnet_ctx_kernel(x_ref, cn_ref, sn_ref, cc_ref, sc_ref, o_ref, *, scale):
    x = x_ref[...]
    wr = _hdot(cn_ref[...], x)
    ws = _hdot(sn_ref[...], x)
    cc, sc = cc_ref[...], sc_ref[...]
    for g in range(FNET_GROUPS):
        cs = slice(g * FNET_GROUP_DIM, (g + 1) * FNET_GROUP_DIM)
        o_ref[:, cs] = (_hdot(wr[:, cs], cc) - _hdot(ws[:, cs], sc)) * scale


def fourier_mix(f_lat, f_ctx):
    n_lat, w = f_lat.shape
    n_ctx = f_ctx.shape[0]
    n2 = 128
    n1 = n_lat // n2
    c2, s2 = _dft_tables(n2)
    c1, s1 = _dft_tables(n1)
    cc, sc = _dft_tables(FNET_GROUP_DIM)
    tw = 2.0 * np.pi * np.outer(np.arange(n2), np.arange(n1)) / n_lat
    twc = jnp.asarray(np.cos(tw), F32).reshape(n2, n1, 1)
    tws = jnp.asarray(np.sin(tw), F32).reshape(n2, n1, 1)
    xr = f_lat.reshape(n2, n1 * w)
    tn = min(4096, n1 * w)
    yr, yi = pl.pallas_call(
        _fnet_a_kernel,
        grid=(n1 * w // tn,),
        in_specs=[pl.BlockSpec((n2, n2), lambda j: (0, 0)),
                  pl.BlockSpec((n2, n2), lambda j: (0, 0)),
                  pl.BlockSpec((n2, tn), lambda j: (0, j))],
        out_specs=[pl.BlockSpec((n2, tn), lambda j: (0, j)),
                   pl.BlockSpec((n2, tn), lambda j: (0, j))],
        out_shape=[jax.ShapeDtypeStruct((n2, n1 * w), F32)] * 2,
        compiler_params=_cparams(("arbitrary",)),
        name="fnet_stage_a",
    )(c2, s2, xr)
    kb = 8
    lat = pl.pallas_call(
        functools.partial(_fnet_b_kernel, scale=float(1.0 / math.sqrt(n_lat * FNET_GROUP_DIM)), kb=kb),
        grid=(n2 // kb,),
        in_specs=[pl.BlockSpec((kb, n1, w), lambda j: (j, 0, 0)),
                  pl.BlockSpec((kb, n1, w), lambda j: (j, 0, 0)),
                  pl.BlockSpec((kb, n1, 1), lambda j: (j, 0, 0)),
                  pl.BlockSpec((kb, n1, 1), lambda j: (j, 0, 0)),
                  pl.BlockSpec((n1, n1), lambda j: (0, 0)),
                  pl.BlockSpec((n1, n1), lambda j: (0, 0)),
                  pl.BlockSpec((FNET_GROUP_DIM, FNET_GROUP_DIM), lambda j: (0, 0)),
                  pl.BlockSpec((FNET_GROUP_DIM, FNET_GROUP_DIM), lambda j: (0, 0))],
        out_specs=pl.BlockSpec((n1, kb, w), lambda j: (0, j, 0)),
        out_shape=jax.ShapeDtypeStruct((n1, n2, w), F32),
        compiler_params=_cparams(("arbitrary",)),
        name="fnet_stage_b",
    )(yr.reshape(n2, n1, w), yi.reshape(n2, n1, w), twc, tws, c1, s1, cc, sc)
    cn, sn = _dft_tables(n_ctx)
    ctx = pl.pallas_call(
        functools.partial(_fnet_ctx_kernel, scale=float(1.0 / math.sqrt(n_ctx * FNET_GROUP_DIM))),
        out_shape=jax.ShapeDtypeStruct((n_ctx, w), F32),
        compiler_params=pltpu.CompilerParams(vmem_limit_bytes=VMEM_LIMIT_BYTES),
        name="fnet_ctx",
    )(f_ctx, cn, sn, cc, sc)
    return jnp.concatenate([ctx, lat.reshape(n_lat, w)], axis=0)


def _merge_kernel(o0_ref, o1_ref, o2_ref, o3_ref, g0_ref, g1_ref, g2_ref, g3_ref, wb_ref, y_ref):
    y = None
    for b, (o_ref, g_ref) in enumerate(((o0_ref, g0_ref), (o1_ref, g1_ref), (o2_ref, g2_ref), (o3_ref, g3_ref))):
        pr = jnp.dot(o_ref[...].astype(BF16), wb_ref[b].astype(BF16), preferred_element_type=F32)
        t = jax.nn.sigmoid(g_ref[...]) * pr
        y = t if y is None else y + t
    y_ref[...] = y.astype(y_ref.dtype)


def merge(branches, gates, w_branch, tm=768, tn=512):
    m = gates.shape[0]
    nb = D_MODEL // tn
    o_specs = [pl.BlockSpec((tm, BRANCH_W), lambda j, i: (i, 0)) for _ in range(N_BRANCH)]
    g_specs = [pl.BlockSpec((tm, tn), functools.partial(lambda j, i, b: (i, b * nb + j), b=b)) for b in range(N_BRANCH)]
    return pl.pallas_call(
        _merge_kernel,
        grid=(nb, m // tm),
        in_specs=o_specs + g_specs + [pl.BlockSpec((N_BRANCH, BRANCH_W, tn), lambda j, i: (0, 0, j))],
        out_specs=pl.BlockSpec((tm, tn), lambda j, i: (i, j)),
        out_shape=jax.ShapeDtypeStruct((m, D_MODEL), BF16),
        compiler_params=_cparams(("arbitrary", "arbitrary")),
        name="merge",
    )(*branches, gates, gates, gates, gates, w_branch)


def _expert_kernel(xs_ref, wg_ref, wu_ref, wd_ref, gate_ref, o_ref, acc_ref):
    f = pl.program_id(1)
    xs = xs_ref[0]
    hg = jnp.dot(xs, wg_ref[0].astype(BF16), preferred_element_type=F32)
    hu = jnp.dot(xs, wu_ref[0].astype(BF16), preferred_element_type=F32)
    hid = (_silu(hg) * hu).astype(BF16)
    part = jnp.dot(hid, wd_ref[0].astype(BF16), preferred_element_type=F32)

    @pl.when(f == 0)
    def _first():
        acc_ref[...] = part

    @pl.when(f != 0)
    def _rest():
        acc_ref[...] += part

    @pl.when(f == pl.num_programs(1) - 1)
    def _done():
        o_ref[0] = acc_ref[...] * gate_ref[0]


def expert_ffn(xs, gate, w_gate, w_up, w_down, tf=256):
    ne, cap, d = xs.shape
    ff = w_gate.shape[2]
    return pl.pallas_call(
        _expert_kernel,
        grid=(ne, ff // tf),
        in_specs=[pl.BlockSpec((1, cap, d), lambda e, f: (e, 0, 0)),
                  pl.BlockSpec((1, d, tf), lambda e, f: (e, 0, f)),
                  pl.BlockSpec((1, d, tf), lambda e, f: (e, 0, f)),
                  pl.BlockSpec((1, tf, d), lambda e, f: (e, f, 0)),
                  pl.BlockSpec((1, cap, 1), lambda e, f: (e, 0, 0))],
        out_specs=pl.BlockSpec((1, cap, d), lambda e, f: (e, 0, 0)),
        out_shape=jax.ShapeDtypeStruct((ne, cap, d), F32),
        scratch_shapes=[pltpu.VMEM((cap, d), F32)],
        compiler_params=_cparams(("arbitrary", "arbitrary")),
        name="expert_ffn",
    )(xs, w_gate, w_up, w_down, gate.reshape(ne, cap, 1))


def moe(h2, aff, w_gate, w_up, w_down, n_ctx):
    m = h2.shape[0]
    n_lat = m - n_ctx
    a = aff[:, :N_EXPERTS]
    g_c, i_c = lax.top_k(a[:n_ctx].T, CAPACITY_FACTOR * n_ctx // N_EXPERTS)
    g_l, i_l = lax.top_k(a[n_ctx:].T, CAPACITY_FACTOR * n_lat // N_EXPERTS)
    idx = jnp.concatenate([i_c, i_l + n_ctx], axis=1)
    gate = jnp.concatenate([g_c, g_l], axis=1)
    xs = h2[idx]
    ye = expert_ffn(xs, gate, w_gate, w_up, w_down)
    return jnp.zeros((m, D_MODEL), F32).at[idx.reshape(-1)].add(ye.reshape(-1, D_MODEL))


def _split_w_in(w_in):
    na = 3 * NA_HEADS * HEAD_DIM
    o_z = na
    o_xbc = o_z + SSD_W
    o_dt = o_xbc + SSD_XBC_W
    o_gq = o_dt + 2 * SSD_HEADS
    o_gk = o_gq + GQA_HEADS * HEAD_DIM
    o_gv = o_gk + GQA_KV_HEADS * HEAD_DIM
    o_fn = o_gv + GQA_KV_HEADS * HEAD_DIM
    o_gt = o_fn + FNET_GROUPS * FNET_GROUP_DIM
    d = w_in.shape[0]
    w_a = jnp.concatenate([w_in[:, :na], w_in[:, o_gv:o_fn]], axis=1)
    w_b = jnp.concatenate([w_in[:, o_xbc:o_dt], w_in[:, o_z:o_xbc], w_in[:, o_gq:o_gv],
                           w_in[:, o_dt:o_gq], jnp.zeros((d, FB_FN - FB_DT - 2 * SSD_HEADS), w_in.dtype),
                           w_in[:, o_fn:o_gt]], axis=1)
    return w_a, w_b, w_in[:, o_gt:]


def _layer(x, mod, n_ctx, norm_mix, norm_ffn, w_in, na_rpb, conv_w, conv_b, a_log, dt_bias, d_skip, ssd_gn,
           q_gain, k_gain, w_branch, w_out, w_router, w_gate, w_up, w_down, rope):
    m = x.shape[0]
    tm = 768 if m % 768 == 0 else 256
    h = norm_mod(x, norm_mix, mod, n_ctx, 0, tm=tm)
    w_a, w_b, w_g = _split_w_in(w_in)
    ab = matmul(h, w_a, BF16, tm, AB_W // 2, "proj_bf16")
    fb = matmul(h, w_b, F32, tm, 1024, "proj_f32")
    gates = matmul(h, w_g, F32, tm, 1024, "proj_gates")

    o_na = na_attention(ab, na_bias_table(na_rpb), n_ctx)

    u, dt = ssd_prep(fb, conv_w, conv_b, dt_bias, n_ctx)
    y2 = ssd_scan(u, dt, a_log, n_ctx)
    o_ssd = ssd_out(y2, u, fb, d_skip, ssd_gn, tm=tm)

    qn, kn = qk_prep(fb, rope[0], rope[1], q_gain, k_gain, tm=tm)
    o_gqa = gqa_attention(qn, kn, ab, n_ctx)

    f_in = fb[:, FB_FN:FB_FN + FNET_GROUPS * FNET_GROUP_DIM]
    o_fn = fourier_mix(f_in[n_ctx:], f_in[:n_ctx])

    y = merge((o_na, o_ssd, o_gqa, o_fn), gates, w_branch, tm=tm)
    x = matmul_residual(y, w_out, x, mod, 2, n_ctx, tm, 1024, "out_proj")

    wr = jnp.concatenate([w_router, jnp.zeros((D_MODEL, LANES - N_EXPERTS), w_router.dtype)], axis=1)
    h2, aff = norm_router(x, norm_ffn, mod, wr, n_ctx, 3, tm=tm)
    mo = moe(h2, aff, w_gate, w_up, w_down, n_ctx)
    row = jnp.arange(m)[:, None]
    g2 = jnp.where(row < n_ctx, mod[1, 5 * D_MODEL:][None], mod[0, 5 * D_MODEL:][None])
    return x + g2 * mo


def kernel(x, c, ctx, c_ctx, w_ada, b_ada, norm_mix, norm_ffn, w_in, na_rpb, ssd_conv_w, ssd_conv_b, ssd_a_log,
           ssd_dt_bias, ssd_d, ssd_norm, gqa_q_norm, gqa_k_norm, w_branch, w_out, w_router, moe_w_gate, moe_w_up,
           moe_w_down, final_norm_w):
    n_lat = x.shape[1]
    n_ctx = ctx.shape[1]
    depth = w_ada.shape[0]
    cc = jnp.concatenate([c[0:1], c_ctx[None], jnp.zeros((6, D_MODEL), F32)], axis=0)
    mods = ada_all(cc, w_ada, b_ada)
    rope = rope_tables(n_ctx, n_lat)
    xs = jnp.concatenate([ctx[0], x[0]], axis=0)
    for l in range(depth):
        xs = _layer(xs, mods[l], n_ctx, norm_mix[l], norm_ffn[l], w_in[l], na_rpb[l], ssd_conv_w[l],
                    ssd_conv_b[l], ssd_a_log[l], ssd_dt_bias[l], ssd_d[l], ssd_norm[l], gqa_q_norm[l],
                    gqa_k_norm[l], w_branch[l], w_out[l], w_router[l], moe_w_gate[l], moe_w_up[l],
                    moe_w_down[l], rope)
    out = final_norm(xs, final_norm_w, n_ctx, n_lat)
    return out[None]
```

```python
import functools
import math

import numpy as np
import jax
import jax.numpy as jnp
from jax import lax
from jax.experimental import pallas as pl
from jax.experimental.pallas import tpu as pltpu

F32 = jnp.float32
BF16 = jnp.bfloat16

D_MODEL = 2048
GRID_W = 64
HEAD_DIM = 128
RMS_EPS = 1e-6
N_BRANCH = 4
BRANCH_W = 512
NA_HEADS = 4
NA_WIN_R = 8
NA_WIN_C = 16
SSD_HEADS = 8
SSD_HEAD_DIM = 64
SSD_GROUPS = 2
SSD_STATE = 128
SSD_CONV = 5
SSD_CHUNK = 128
SSD_W = SSD_HEADS * SSD_HEAD_DIM
SSD_XBC_W = SSD_W + 2 * SSD_GROUPS * SSD_STATE
GQA_HEADS = 4
GQA_KV_HEADS = 2
ROPE_THETA = 10000.0
FNET_GROUPS = 4
FNET_GROUP_DIM = 128
N_EXPERTS = 16
EXPERT_FF = D_MODEL // 2
CAPACITY_FACTOR = 2
LOG2E = 1.4426950408889634

VMEM_LIMIT_BYTES = 56 * 1024 * 1024
LANES = 128

ZX_XBC, ZX_Z, ZX_W = 0, 1024, 1536
FR_Q, FR_K, FR_V, FR_FN, FR_DT, FR_W = 0, 512, 768, 1024, 1536, 1792


def _cparams(sem):
    return pltpu.CompilerParams(dimension_semantics=sem, vmem_limit_bytes=VMEM_LIMIT_BYTES)


def _silu(x):
    return x * jax.nn.sigmoid(x)


def _ada_kernel(ct_ref, w_ref, b_ref, o_ref):
    a = _silu(ct_ref[...])
    w = w_ref[0]
    rows = [jnp.sum(w * a[:, r:r + 1], axis=0, keepdims=True) + b_ref[0] for r in range(2)]
    o_ref[0] = jnp.concatenate(rows + [jnp.zeros((6, w.shape[1]), F32)], axis=0)


def ada_all(cc, w_ada, b_ada):
    depth, d, n = w_ada.shape
    tn = 1024
    return pl.pallas_call(
        _ada_kernel,
        grid=(depth, n // tn),
        in_specs=[pl.BlockSpec((d, 8), lambda l, j: (0, 0)),
                  pl.BlockSpec((1, d, tn), lambda l, j: (l, 0, j)),
                  pl.BlockSpec((1, 1, tn), lambda l, j: (l, 0, j))],
        out_specs=pl.BlockSpec((1, 8, tn), lambda l, j: (l, 0, j)),
        out_shape=jax.ShapeDtypeStruct((depth, 8, n), F32),
        compiler_params=_cparams(("arbitrary", "arbitrary")),
        name="ada_mod",
    )(cc.T, w_ada, b_ada.reshape(depth, 1, n))


def _row_select(mod_ref, chunk, row_is_ctx):
    lo = chunk * D_MODEL
    return jnp.where(row_is_ctx, mod_ref[1:2, lo:lo + D_MODEL], mod_ref[0:1, lo:lo + D_MODEL])


def _norm_body(x_ref, nw_ref):
    x = x_ref[...]
    ms = jnp.mean(x * x, axis=-1, keepdims=True)
    return x * lax.rsqrt(ms + RMS_EPS) * nw_ref[...]


def _norm_mod_kernel(x_ref, nw_ref, mod_ref, o_ref, *, n_ctx, tm, sh_chunk):
    y = _norm_body(x_ref, nw_ref)
    row = pl.program_id(0) * tm + lax.broadcasted_iota(jnp.int32, (tm, 1), 0)
    is_ctx = row < n_ctx
    sh = _row_select(mod_ref, sh_chunk, is_ctx)
    sc = _row_select(mod_ref, sh_chunk + 1, is_ctx)
    o_ref[...] = (y * (1.0 + sc) + sh).astype(o_ref.dtype)


def norm_mod(x, nw, mod, n_ctx, sh_chunk, tm=768):
    m, d = x.shape
    return pl.pallas_call(
        functools.partial(_norm_mod_kernel, n_ctx=n_ctx, tm=tm, sh_chunk=sh_chunk),
        grid=(m // tm,),
        in_specs=[pl.BlockSpec((tm, d), lambda i: (i, 0)),
                  pl.BlockSpec((1, d), lambda i: (0, 0)),
                  pl.BlockSpec(mod.shape, lambda i: (0, 0))],
        out_specs=pl.BlockSpec((tm, d), lambda i: (i, 0)),
        out_shape=jax.ShapeDtypeStruct((m, d), BF16),
        compiler_params=_cparams(("arbitrary",)),
        name="norm_mod",
    )(x, nw.reshape(1, d), mod)


def _norm_router_kernel(x_ref, nw_ref, mod_ref, wr_ref, h_ref, aff_ref, *, n_ctx, tm, sh_chunk):
    y = _norm_body(x_ref, nw_ref)
    row = pl.program_id(0) * tm + lax.broadcasted_iota(jnp.int32, (tm, 1), 0)
    is_ctx = row < n_ctx
    sh = _row_select(mod_ref, sh_chunk, is_ctx)
    sc = _row_select(mod_ref, sh_chunk + 1, is_ctx)
    h = y * (1.0 + sc) + sh
    h_ref[...] = h
    logits = jnp.dot(h.astype(BF16), wr_ref[...].astype(BF16), preferred_element_type=F32)
    lane = lax.broadcasted_iota(jnp.int32, logits.shape, 1)
    logits = jnp.where(lane < N_EXPERTS, logits, -jnp.inf)
    mx = jnp.max(logits, axis=-1, keepdims=True)
    e = jnp.exp(logits - mx)
    aff_ref[...] = e / jnp.sum(e, axis=-1, keepdims=True)


def norm_router(x, nw, mod, w_router_pad, n_ctx, sh_chunk, tm=768):
    m, d = x.shape
    return pl.pallas_call(
        functools.partial(_norm_router_kernel, n_ctx=n_ctx, tm=tm, sh_chunk=sh_chunk),
        grid=(m // tm,),
        in_specs=[pl.BlockSpec((tm, d), lambda i: (i, 0)),
                  pl.BlockSpec((1, d), lambda i: (0, 0)),
                  pl.BlockSpec(mod.shape, lambda i: (0, 0)),
                  pl.BlockSpec((d, LANES), lambda i: (0, 0))],
        out_specs=[pl.BlockSpec((tm, d), lambda i: (i, 0)),
                   pl.BlockSpec((tm, LANES), lambda i: (i, 0))],
        out_shape=[jax.ShapeDtypeStruct((m, d), F32),
                   jax.ShapeDtypeStruct((m, LANES), F32)],
        compiler_params=_cparams(("arbitrary",)),
        name="norm_router",
    )(x, nw.reshape(1, d), mod, w_router_pad)


def _final_norm_kernel(x_ref, nw_ref, o_ref):
    o_ref[...] = _norm_body(x_ref, nw_ref)


def final_norm(x, nw, row0, n_rows, tm=256):
    d = x.shape[1]
    off = row0 // tm
    return pl.pallas_call(
        _final_norm_kernel,
        grid=(n_rows // tm,),
        in_specs=[pl.BlockSpec((tm, d), lambda i: (i + off, 0)),
                  pl.BlockSpec((1, d), lambda i: (0, 0))],
        out_specs=pl.BlockSpec((tm, d), lambda i: (i, 0)),
        out_shape=jax.ShapeDtypeStruct((n_rows, d), F32),
        compiler_params=_cparams(("arbitrary",)),
        name="final_norm",
    )(x, nw.reshape(1, d))


def _mm_kernel(a_ref, w_ref, o_ref):
    o_ref[...] = jnp.dot(a_ref[...].astype(BF16), w_ref[...].astype(BF16),
                         preferred_element_type=F32).astype(o_ref.dtype)


def matmul(a, w, out_dtype, tm, tn, name):
    m, k = a.shape
    n = w.shape[1]
    return pl.pallas_call(
        _mm_kernel,
        grid=(n // tn, m // tm),
        in_specs=[pl.BlockSpec((tm, k), lambda j, i: (i, 0)),
                  pl.BlockSpec((k, tn), lambda j, i: (0, j))],
        out_specs=pl.BlockSpec((tm, tn), lambda j, i: (i, j)),
        out_shape=jax.ShapeDtypeStruct((m, n), out_dtype),
        compiler_params=_cparams(("arbitrary", "arbitrary")),
        name=name,
    )(a, w)


def _mm_layer_kernel(a_ref, w_ref, o_ref):
    o_ref[...] = jnp.dot(a_ref[...], w_ref[0].astype(BF16), preferred_element_type=F32).astype(o_ref.dtype)


def matmul_cols(a, w_all, layer, col0, ncols, out_dtype, tm, tn, name, out_perm=None):
    m, k = a.shape
    nb = ncols // tn
    c0 = col0 // tn
    perm = tuple(range(nb)) if out_perm is None else tuple(out_perm)

    def out_map(j, i):
        pj = j
        for src, dst in enumerate(perm):
            pj = jnp.where(j == src, dst, pj)
        return (i, pj)

    return pl.pallas_call(
        _mm_layer_kernel,
        grid=(nb, m // tm),
        in_specs=[pl.BlockSpec((tm, k), lambda j, i: (i, 0)),
                  pl.BlockSpec((1, k, tn), lambda j, i: (layer, 0, c0 + j))],
        out_specs=pl.BlockSpec((tm, tn), out_map),
        out_shape=jax.ShapeDtypeStruct((m, ncols), out_dtype),
        compiler_params=_cparams(("arbitrary", "arbitrary")),
        name=name,
    )(a, w_all)


def _mm_shift_kernel(a_ref, wa_ref, wb_ref, o_ref, ws_ref, *, shift, tn):
    @pl.when(pl.program_id(1) == 0)
    def _realign():
        wcat = jnp.concatenate([wa_ref[0], wb_ref[0]], axis=1)
        ws_ref[...] = wcat[:, shift:shift + tn].astype(BF16)

    o_ref[...] = jnp.dot(a_ref[...], ws_ref[...], preferred_element_type=F32).astype(o_ref.dtype)


def matmul_cols_unaligned(a, w_all, layer, col0, ncols, out_dtype, tm, tn, name):
    m, k = a.shape
    base = (col0 // LANES) * LANES
    shift = col0 - base
    assert base % tn == 0 and ncols % tn == 0 and 0 < shift < LANES
    c0 = base // tn
    r = tn // LANES
    return pl.pallas_call(
        functools.partial(_mm_shift_kernel, shift=shift, tn=tn),
        grid=(ncols // tn, m // tm),
        in_specs=[pl.BlockSpec((tm, k), lambda j, i: (i, 0)),
                  pl.BlockSpec((1, k, tn), lambda j, i: (layer, 0, c0 + j)),
                  pl.BlockSpec((1, k, LANES), lambda j, i: (layer, 0, (c0 + j + 1) * r))],
        out_specs=pl.BlockSpec((tm, tn), lambda j, i: (i, j)),
        out_shape=jax.ShapeDtypeStruct((m, ncols), out_dtype),
        scratch_shapes=[pltpu.VMEM((k, tn), BF16)],
        compiler_params=_cparams(("arbitrary", "arbitrary")),
        name=name,
    )(a, w_all, w_all)


def _mm_residual_kernel(a_ref, w_ref, x_ref, g_ref, o_ref, *, n_ctx, tm):
    acc = jnp.dot(a_ref[...].astype(BF16), w_ref[0].astype(BF16), preferred_element_type=F32)
    row = pl.program_id(1) * tm + lax.broadcasted_iota(jnp.int32, (tm, 1), 0)
    g = jnp.where(row < n_ctx, g_ref[1:2, :], g_ref[0:1, :])
    o_ref[...] = x_ref[...] + g * acc


def matmul_residual(a, w_all, layer, x, mod, gate_chunk, n_ctx, tm, tn, name):
    m, k = a.shape
    n = w_all.shape[2]
    goff = gate_chunk * D_MODEL // tn
    return pl.pallas_call(
        functools.partial(_mm_residual_kernel, n_ctx=n_ctx, tm=tm),
        grid=(n // tn, m // tm),
        in_specs=[pl.BlockSpec((tm, k), lambda j, i: (i, 0)),
                  pl.BlockSpec((1, k, tn), lambda j, i: (layer, 0, j)),
                  pl.BlockSpec((tm, tn), lambda j, i: (i, j)),
                  pl.BlockSpec((8, tn), lambda j, i: (0, goff + j))],
        out_specs=pl.BlockSpec((tm, tn), lambda j, i: (i, j)),
        out_shape=jax.ShapeDtypeStruct((m, n), F32),
        compiler_params=_cparams(("arbitrary", "arbitrary")),
        name=name,
    )(a, w_all, x, mod)


def _softmax_pv(parts):
    m = None
    for s, _ in parts:
        mi = jnp.max(s, axis=-1, keepdims=True)
        m = mi if m is None else jnp.maximum(m, mi)
    ps, l = [], None
    for s, _ in parts:
        p = jnp.exp(s - m)
        ps.append(p)
        li = jnp.sum(p, axis=-1, keepdims=True)
        l = li if l is None else l + li
    inv = 1.0 / l
    o = None
    for p, (_, v) in zip(ps, parts):
        oi = jnp.dot((p * inv).astype(BF16), v, preferred_element_type=F32)
        o = oi if o is None else o + oi
    return o


def _qk(q, k):
    return lax.dot_general(q, k, (((1,), (1,)), ((), ())), preferred_element_type=F32)


NA_QROWS = 4
NA_KROWS = NA_WIN_R + NA_QROWS - 1


def _na_kernel(tid_ref, q_ref, k_ref, v_ref, b_ref, o_ref, *, n_ctx, rows):
    del tid_ref
    step = pl.program_id(0)
    tq = NA_QROWS * GRID_W
    n_ctx_steps = n_ctx // tq
    scale = HEAD_DIM ** -0.5
    win = NA_KROWS * GRID_W

    @pl.when(step < n_ctx_steps)
    def _ctx():
        for h in range(NA_HEADS):
            cs = slice(h * HEAD_DIM, (h + 1) * HEAD_DIM)
            q = q_ref[:, cs]
            s_c = _qk(q, k_ref[0:n_ctx, cs]) * scale
            o_ref[:, cs] = _softmax_pv([(s_c, v_ref[0:n_ctx, cs])]).astype(o_ref.dtype)

    @pl.when(step >= n_ctx_steps)
    def _lat():
        p = step - n_ctx_steps
        row0 = jnp.clip(NA_QROWS * p - NA_WIN_R // 2, 0, rows - NA_KROWS)
        base = pl.multiple_of(n_ctx + row0 * GRID_W, GRID_W)
        for h in range(NA_HEADS):
            cs = slice(h * HEAD_DIM, (h + 1) * HEAD_DIM)
            q = q_ref[:, cs]
            s_w = _qk(q, k_ref[pl.ds(base, win), cs]) * scale + b_ref[0, h]
            s_c = _qk(q, k_ref[0:n_ctx, cs]) * scale
            o = _softmax_pv([(s_w, v_ref[pl.ds(base, win), cs]), (s_c, v_ref[0:n_ctx, cs])])
            o_ref[:, cs] = o.astype(o_ref.dtype)


def _na_patterns(rows):
    a = np.arange(NA_QROWS)[:, None]
    i = np.arange(NA_KROWS)[None, :]
    pats, keys, tid = [], [], []
    for p in range(rows // NA_QROWS):
        row0 = int(np.clip(NA_QROWS * p - NA_WIN_R // 2, 0, rows - NA_KROWS))
        r = NA_QROWS * p + a
        rs = np.clip(r - NA_WIN_R // 2, 0, rows - NA_WIN_R)
        krow = row0 + i
        valid = (krow >= rs) & (krow < rs + NA_WIN_R)
        ridx = np.clip(krow - r + NA_WIN_R - 1, 0, 2 * NA_WIN_R - 2)
        key = (valid.tobytes(), ridx.tobytes())
        if key not in keys:
            keys.append(key)
            pats.append((valid, ridx))
        tid.append(keys.index(key))
    return np.stack([v for v, _ in pats]), np.stack([x for _, x in pats]), np.asarray(tid, np.int32)


def na_bias_table(rpb, rows):
    valid_r, ridx, tid = _na_patterns(rows)
    kc = np.arange(GRID_W)[None, :]
    qc = np.arange(GRID_W)[:, None]
    col_start = np.clip(qc - NA_WIN_C // 2, 0, GRID_W - NA_WIN_C)
    valid_c = (kc >= col_start) & (kc < col_start + NA_WIN_C)
    cidx = np.clip(kc - qc + NA_WIN_C - 1, 0, 2 * NA_WIN_C - 2)
    t = rpb.astype(F32)[:, ridx[:, :, :, None, None], cidx[None, None, None]]
    ok = valid_r[:, :, :, None, None] & valid_c[None, None, None]
    t = jnp.where(ok[None], t, -jnp.inf)
    t = jnp.transpose(t, (1, 0, 2, 4, 3, 5))
    npat = valid_r.shape[0]
    return t.reshape(npat, rpb.shape[0], NA_QROWS * GRID_W, NA_KROWS * GRID_W), jnp.asarray(tid)


def na_attention(ab, bias, tid, n_ctx):
    m = ab.shape[0]
    rows = (m - n_ctx) // GRID_W
    tq = NA_QROWS * GRID_W
    n_ctx_steps = n_ctx // tq
    w = NA_HEADS * HEAD_DIM
    grid_spec = pltpu.PrefetchScalarGridSpec(
        num_scalar_prefetch=1,
        grid=(m // tq,),
        in_specs=[pl.BlockSpec((tq, w), lambda s, t: (s, 0)),
                  pl.BlockSpec((m, w), lambda s, t: (0, 1)),
                  pl.BlockSpec((m, w), lambda s, t: (0, 2)),
                  pl.BlockSpec((1, NA_HEADS, tq, NA_KROWS * GRID_W),
                               lambda s, t: (t[jnp.maximum(s - n_ctx_steps, 0)], 0, 0, 0))],
        out_specs=pl.BlockSpec((tq, w), lambda s, t: (s, 0)),
    )
    return pl.pallas_call(
        functools.partial(_na_kernel, n_ctx=n_ctx, rows=rows),
        grid_spec=grid_spec,
        out_shape=jax.ShapeDtypeStruct((m, w), BF16),
        compiler_params=_cparams(("arbitrary",)),
        name="na_attn",
    )(tid, ab, ab, ab, bias)


def rope_tables(n_ctx, n_lat):
    t = jnp.arange(n_lat)
    pos = jnp.stack([t // GRID_W, t % GRID_W], axis=-1).astype(F32)
    n_freq = HEAD_DIM // 4
    inv = ROPE_THETA ** (-jnp.arange(n_freq, dtype=F32) / n_freq)
    ang = pos[:, :, None] * inv
    c, s = jnp.cos(ang), jnp.sin(ang)
    cos = jnp.concatenate([c[:, 0], c[:, 0], c[:, 1], c[:, 1]], axis=-1)
    sin = jnp.concatenate([-s[:, 0], s[:, 0], -s[:, 1], s[:, 1]], axis=-1)
    cos = jnp.concatenate([jnp.ones((n_ctx, HEAD_DIM), F32), cos], axis=0)
    sin = jnp.concatenate([jnp.zeros((n_ctx, HEAD_DIM), F32), sin], axis=0)
    return cos, sin


def _qk_prep_kernel(x_ref, cos_ref, sin_ref, qg_ref, kg_ref, qt_ref, ko_ref, vt_ref):
    cos = cos_ref[...]
    sin = sin_ref[...]
    lane = lax.broadcasted_iota(jnp.int32, (1, HEAD_DIM), 1)
    first_half = (lane % (HEAD_DIM // 2)) < (HEAD_DIM // 4)
    for h in range(GQA_HEADS + GQA_KV_HEADS):
        x = x_ref[:, h * HEAD_DIM:(h + 1) * HEAD_DIM]
        gain = qg_ref[...] if h < GQA_HEADS else kg_ref[...]
        y = x * lax.rsqrt(jnp.mean(x * x, axis=-1, keepdims=True) + RMS_EPS) * gain
        sw = jnp.where(first_half, pltpu.roll(y, HEAD_DIM - HEAD_DIM // 4, 1), pltpu.roll(y, HEAD_DIM // 4, 1))
        out = y * cos + sw * sin
        if h < GQA_HEADS:
            qt_ref[h * HEAD_DIM:(h + 1) * HEAD_DIM, :] = out.T.astype(BF16)
        else:
            hk = h - GQA_HEADS
            ko_ref[:, hk * HEAD_DIM:(hk + 1) * HEAD_DIM] = out.astype(BF16)
    v0 = (GQA_HEADS + GQA_KV_HEADS) * HEAD_DIM
    for hv in range(GQA_KV_HEADS):
        v = x_ref[:, v0 + hv * HEAD_DIM:v0 + (hv + 1) * HEAD_DIM]
        vt_ref[hv * HEAD_DIM:(hv + 1) * HEAD_DIM, :] = v.T.astype(BF16)


def qk_prep(fr, cos, sin, q_gain, k_gain, tm=768):
    m = fr.shape[0]
    wq, wk = GQA_HEADS * HEAD_DIM, GQA_KV_HEADS * HEAD_DIM
    return pl.pallas_call(
        _qk_prep_kernel,
        grid=(m // tm,),
        in_specs=[pl.BlockSpec((tm, wq + 2 * wk), lambda i: (i, 0)),
                  pl.BlockSpec((tm, HEAD_DIM), lambda i: (i, 0)),
                  pl.BlockSpec((tm, HEAD_DIM), lambda i: (i, 0)),
                  pl.BlockSpec((1, HEAD_DIM), lambda i: (0, 0)),
                  pl.BlockSpec((1, HEAD_DIM), lambda i: (0, 0))],
        out_specs=[pl.BlockSpec((wq, tm), lambda i: (0, i)),
                   pl.BlockSpec((tm, wk), lambda i: (i, 0)),
                   pl.BlockSpec((wk, tm), lambda i: (0, i))],
        out_shape=[jax.ShapeDtypeStruct((wq, m), BF16),
                   jax.ShapeDtypeStruct((m, wk), BF16),
                   jax.ShapeDtypeStruct((wk, m), BF16)],
        compiler_params=_cparams(("arbitrary",)),
        name="qk_prep",
    )(fr, cos, sin, q_gain.reshape(1, HEAD_DIM), k_gain.reshape(1, HEAD_DIM))


def _gqa_kernel(qt_ref, k_ref, vt_ref, o_ref, *, n_ctx, n_all, tq, tk):
    qi = pl.program_id(1)
    c1 = (HEAD_DIM ** -0.5) * LOG2E
    n_ctx_tiles = n_ctx // tq
    rep = GQA_HEADS // GQA_KV_HEADS

    def attend(n_keys, tkk):
        state = [None] * rep
        for c in range(n_keys // tkk):
            kc = k_ref[c * tkk:(c + 1) * tkk, :]
            vtc = vt_ref[:, c * tkk:(c + 1) * tkk]
            for r in range(rep):
                qt = qt_ref[r * HEAD_DIM:(r + 1) * HEAD_DIM, :]
                st = jnp.dot(kc, qt, preferred_element_type=F32)
                mc = jnp.max(st, axis=0, keepdims=True)
                if state[r] is None:
                    m_new = mc
                    p = jnp.exp2(st * c1 - m_new * c1)
                    l = jnp.sum(p, axis=0, keepdims=True)
                    acc = jnp.dot(vtc, p.astype(BF16), preferred_element_type=F32)
                else:
                    m, l, acc = state[r]
                    m_new = jnp.maximum(m, mc)
                    p = jnp.exp2(st * c1 - m_new * c1)
                    alpha = jnp.exp2((m - m_new) * c1)
                    l = alpha * l + jnp.sum(p, axis=0, keepdims=True)
                    acc = alpha * acc + jnp.dot(vtc, p.astype(BF16), preferred_element_type=F32)
                state[r] = (m_new, l, acc)
        for r in range(rep):
            _, l, acc = state[r]
            o_ref[:, r * HEAD_DIM:(r + 1) * HEAD_DIM] = (acc / l).T.astype(o_ref.dtype)

    @pl.when(qi < n_ctx_tiles)
    def _ctx():
        attend(n_ctx, n_ctx)

    @pl.when(qi >= n_ctx_tiles)
    def _lat():
        attend(n_all, tk)


def gqa_attention(qt, kn, vt, n_ctx, tq=256, tk=768):
    m = kn.shape[0]
    rep = GQA_HEADS // GQA_KV_HEADS
    if m % tk:
        tk = 256
    return pl.pallas_call(
        functools.partial(_gqa_kernel, n_ctx=n_ctx, n_all=m, tq=tq, tk=tk),
        grid=(GQA_KV_HEADS, m // tq),
        in_specs=[pl.BlockSpec((rep * HEAD_DIM, tq), lambda g, i: (g, i)),
                  pl.BlockSpec((m, HEAD_DIM), lambda g, i: (0, g)),
                  pl.BlockSpec((HEAD_DIM, m), lambda g, i: (g, 0))],
        out_specs=pl.BlockSpec((tq, rep * HEAD_DIM), lambda g, i: (i, g)),
        out_shape=jax.ShapeDtypeStruct((m, GQA_HEADS * HEAD_DIM), BF16),
        compiler_params=_cparams(("arbitrary", "arbitrary")),
        name="gqa_attn",
    )(qt, kn, vt)


def _ssd_prep_kernel(prev_ref, x_ref, next_ref, dtr_ref, cw_ref, cb_ref, dtb_ref, u_ref, dt_ref, *, n_ctx, n_all, tm):
    i = pl.program_id(0)
    lo = i * tm
    hi = lo + tm
    top_ok = jnp.logical_and(lo != 0, lo != n_ctx)
    bot_ok = jnp.logical_and(hi != n_ctx, hi != n_all)
    prev = jnp.where(top_ok, prev_ref[...], 0.0)
    nxt = jnp.where(bot_ok, next_ref[...], 0.0)
    ext = jnp.concatenate([prev, x_ref[...], nxt], axis=0)
    half = SSD_CONV // 2
    acc = None
    for j in range(SSD_CONV):
        sl = ext[8 - half + j:8 - half + j + tm, :]
        term = sl * cw_ref[j:j + 1, :]
        acc = term if acc is None else acc + term
    u_ref[...] = _silu(acc + cb_ref[...])
    dt_ref[...] = jax.nn.softplus(dtr_ref[...] + dtb_ref[...])


def ssd_prep(fzx, fr, conv_w, conv_b, dt_bias, n_ctx, tm=256):
    m = fzx.shape[0]
    nb8 = tm // 8
    last8 = m // 8 - 1
    cw = jnp.concatenate([conv_w, jnp.zeros((8 - SSD_CONV, SSD_XBC_W), F32)], axis=0)
    dtb = jnp.concatenate([dt_bias.reshape(-1), jnp.zeros((LANES - 2 * SSD_HEADS,), F32)]).reshape(1, LANES)
    return pl.pallas_call(
        functools.partial(_ssd_prep_kernel, n_ctx=n_ctx, n_all=m, tm=tm),
        grid=(m // tm,),
        in_specs=[pl.BlockSpec((8, SSD_XBC_W), lambda i: (jnp.maximum(i * nb8 - 1, 0), 0)),
                  pl.BlockSpec((tm, SSD_XBC_W), lambda i: (i, 0)),
                  pl.BlockSpec((8, SSD_XBC_W), lambda i: (jnp.minimum((i + 1) * nb8, last8), 0)),
                  pl.BlockSpec((tm, LANES), lambda i: (i, FR_DT // LANES)),
                  pl.BlockSpec((8, SSD_XBC_W), lambda i: (0, 0)),
                  pl.BlockSpec((1, SSD_XBC_W), lambda i: (0, 0)),
                  pl.BlockSpec((1, LANES), lambda i: (0, 0))],
        out_specs=[pl.BlockSpec((tm, SSD_XBC_W), lambda i: (i, 0)),
                   pl.BlockSpec((tm, LANES), lambda i: (i, 0))],
        out_shape=[jax.ShapeDtypeStruct((m, SSD_XBC_W), F32),
                   jax.ShapeDtypeStruct((m, LANES), F32)],
        compiler_params=_cparams(("arbitrary",)),
        name="ssd_prep",
    )(fzx, fzx, fzx, fr, cw, conv_b.reshape(1, SSD_XBC_W), dtb)


def _ssd_scan_kernel(u_ref, dt_ref, alog_ref, y_ref, ht_ref):
    d = pl.program_id(0)
    s = pl.program_id(1)
    ln = SSD_CHUNK
    p = SSD_HEAD_DIM
    ns = SSD_STATE
    epg = SSD_HEADS // SSD_GROUPS

    @pl.when(s == 0)
    def _init():
        ht_ref[...] = jnp.zeros_like(ht_ref)

    fwd = d == 0
    dt_all = dt_ref[...]
    dt = jnp.where(fwd, dt_all, pltpu.roll(dt_all, LANES - SSD_HEADS, 1))
    a = dt * (-jnp.exp(alog_ref[0]))
    li = lax.broadcasted_iota(jnp.int32, (ln, ln), 0)
    si = lax.broadcasted_iota(jnp.int32, (ln, ln), 1)
    mask = jnp.where(fwd, li - si, si - li) >= 0
    tri = mask.astype(F32)
    a_cum = jnp.dot(tri, a, preferred_element_type=F32, precision=lax.Precision.HIGHEST)
    a_cum_t = a_cum.T
    a_tot = jnp.where(fwd, a_cum[ln - 1:ln, :], a_cum[0:1, :])
    w_all = jnp.exp(a_tot - a_cum)
    ea_all = jnp.exp(a_cum)
    eat = jnp.exp(a_tot)
    for g in range(SSD_GROUPS):
        bg = u_ref[:, SSD_W + g * ns:SSD_W + (g + 1) * ns]
        cg = u_ref[:, SSD_W + SSD_GROUPS * ns + g * ns:SSD_W + SSD_GROUPS * ns + (g + 1) * ns]
        cgb = cg.astype(BF16)
        cb = _qk(cgb, bg.astype(BF16))
        bgt = bg.T.astype(BF16)
        for e in range(epg):
            h = g * epg + e
            ac = a_cum[:, h:h + 1]
            act = a_cum_t[h:h + 1, :]
            decay = jnp.exp(jnp.where(mask, ac - act, -jnp.inf))
            mm = (cb * decay).astype(BF16)
            xdt = u_ref[:, h * p:(h + 1) * p] * dt[:, h:h + 1]
            ht = ht_ref[h]
            y = (jnp.dot(mm, xdt.astype(BF16), preferred_element_type=F32)
                 + ea_all[:, h:h + 1] * jnp.dot(cgb, ht.astype(BF16), preferred_element_type=F32))
            st = jnp.dot(bgt, (xdt * w_all[:, h:h + 1]).astype(BF16), preferred_element_type=F32)
            ht_ref[h] = eat[:, h:h + 1] * ht + st
            y_ref[0, :, h * p:(h + 1) * p] = y


def ssd_scan(u, dt, a_log, n_ctx):
    m = u.shape[0]
    nc = m // SSD_CHUNK
    ncc = n_ctx // SSD_CHUNK

    def chunk_of(d, s):
        bwd = jnp.where(s < ncc, ncc - 1 - s, ncc + nc - 1 - s)
        return jnp.where(d == 0, s, bwd)

    al = jnp.concatenate([a_log, jnp.zeros((2, LANES - SSD_HEADS), F32)], axis=1).reshape(2, 1, LANES)
    return pl.pallas_call(
        _ssd_scan_kernel,
        grid=(2, nc),
        in_specs=[pl.BlockSpec((SSD_CHUNK, SSD_XBC_W), lambda d, s: (chunk_of(d, s), 0)),
                  pl.BlockSpec((SSD_CHUNK, LANES), lambda d, s: (chunk_of(d, s), 0)),
                  pl.BlockSpec((1, 1, LANES), lambda d, s: (d, 0, 0))],
        out_specs=pl.BlockSpec((1, SSD_CHUNK, SSD_W), lambda d, s: (d, chunk_of(d, s), 0)),
        out_shape=jax.ShapeDtypeStruct((2, m, SSD_W), F32),
        scratch_shapes=[pltpu.VMEM((SSD_HEADS, SSD_STATE, SSD_HEAD_DIM), F32)],
        compiler_params=_cparams(("arbitrary", "arbitrary")),
        name="ssd_scan",
    )(u, dt, al)


def _ssd_out_kernel(yf_ref, yb_ref, xs_ref, z_ref, dsk_ref, nw_ref, o_ref):
    y = yf_ref[0] + yb_ref[0] + dsk_ref[...] * xs_ref[...]
    gated = y * _silu(z_ref[...])
    gw = SSD_W // SSD_GROUPS
    for g in range(SSD_GROUPS):
        blk = gated[:, g * gw:(g + 1) * gw]
        nrm = blk * lax.rsqrt(jnp.mean(blk * blk, axis=-1, keepdims=True) + RMS_EPS)
        o_ref[:, g * gw:(g + 1) * gw] = (nrm * nw_ref[:, g * gw:(g + 1) * gw]).astype(o_ref.dtype)


def ssd_out(y2, u, fb, d_skip, norm_w, tm=768):
    m = u.shape[0]
    dsk = jnp.repeat(d_skip.astype(F32), SSD_HEAD_DIM).reshape(1, SSD_W)
    return pl.pallas_call(
        _ssd_out_kernel,
        grid=(m // tm,),
        in_specs=[pl.BlockSpec((1, tm, SSD_W), lambda i: (0, i, 0)),
                  pl.BlockSpec((1, tm, SSD_W), lambda i: (1, i, 0)),
                  pl.BlockSpec((tm, SSD_W), lambda i: (i, 0)),
                  pl.BlockSpec((tm, SSD_W), lambda i: (i, ZX_Z // SSD_W)),
                  pl.BlockSpec((1, SSD_W), lambda i: (0, 0)),
                  pl.BlockSpec((1, SSD_W), lambda i: (0, 0))],
        out_specs=pl.BlockSpec((tm, SSD_W), lambda i: (i, 0)),
        out_shape=jax.ShapeDtypeStruct((m, SSD_W), BF16),
        compiler_params=_cparams(("arbitrary",)),
        name="ssd_out",
    )(y2, y2, u, fb, dsk, norm_w.reshape(1, SSD_W))


def _dft_tables(n):
    ang = 2.0 * np.pi * np.outer(np.arange(n), np.arange(n)) / n
    return jnp.asarray(np.cos(ang), F32), jnp.asarray(np.sin(ang), F32)


def _hdot(a, b):
    return jnp.dot(a, b, preferred_element_type=F32, precision=lax.Precision.HIGHEST)


def _fnet_a_kernel(c_ref, s_ref, x_ref, yr_ref, yi_ref):
    x = x_ref[...]
    yr_ref[...] = _hdot(c_ref[...], x)
    yi_ref[...] = -_hdot(s_ref[...], x)


def _fnet_b_kernel(yr_ref, yi_ref, twc_ref, tws_ref, c1_ref, s1_ref, cc_ref, sc_ref, o_ref, *, scale, kb):
    c1, s1 = c1_ref[...], s1_ref[...]
    cc, sc = cc_ref[...], sc_ref[...]
    for j in range(kb):
        yr, yi = yr_ref[j], yi_ref[j]
        tc, ts = twc_ref[j], tws_ref[j]
        pr = yr * tc + yi * ts
        pi = yi * tc - yr * ts
        zr = _hdot(c1, pr) + _hdot(s1, pi)
        zi = _hdot(c1, pi) - _hdot(s1, pr)
        for g in range(FNET_GROUPS):
            cs = slice(g * FNET_GROUP_DIM, (g + 1) * FNET_GROUP_DIM)
            o_ref[:, j, cs] = (_hdot(zr[:, cs], cc) + _hdot(zi[:, cs], sc)) * scale


def _fnet_ctx_kernel(x_ref, cn_ref, sn_ref, cc_ref, sc_ref, o_ref, *, scale):
    x = x_ref[...]
    wr = _hdot(cn_ref[...], x)
    ws = _hdot(sn_ref[...], x)
    cc, sc = cc_ref[...], sc_ref[...]
    for g in range(FNET_GROUPS):
        cs = slice(g * FNET_GROUP_DIM, (g + 1) * FNET_GROUP_DIM)
        o_ref[:, cs] = (_hdot(wr[:, cs], cc) - _hdot(ws[:, cs], sc)) * scale


def fourier_mix(f_lat, f_ctx):
    n_lat, w = f_lat.shape
    n_ctx = f_ctx.shape[0]
    n2 = 128
    n1 = n_lat // n2
    c2, s2 = _dft_tables(n2)
    c1, s1 = _dft_tables(n1)
    cc, sc = _dft_tables(FNET_GROUP_DIM)
    tw = 2.0 * np.pi * np.outer(np.arange(n2), np.arange(n1)) / n_lat
    twc = jnp.asarray(np.cos(tw), F32).reshape(n2, n1, 1)
    tws = jnp.asarray(np.sin(tw), F32).reshape(n2, n1, 1)
    xr = f_lat.reshape(n2, n1 * w)
    tn = min(4096, n1 * w)
    yr, yi = pl.pallas_call(
        _fnet_a_kernel,
        grid=(n1 * w // tn,),
        in_specs=[pl.BlockSpec((n2, n2), lambda j: (0, 0)),
                  pl.BlockSpec((n2, n2), lambda j: (0, 0)),
                  pl.BlockSpec((n2, tn), lambda j: (0, j))],
        out_specs=[pl.BlockSpec((n2, tn), lambda j: (0, j)),
                   pl.BlockSpec((n2, tn), lambda j: (0, j))],
        out_shape=[jax.ShapeDtypeStruct((n2, n1 * w), F32)] * 2,
        compiler_params=_cparams(("arbitrary",)),
        name="fnet_stage_a",
    )(c2, s2, xr)
    kb = 8
    lat = pl.pallas_call(
        functools.partial(_fnet_b_kernel, scale=float(1.0 / math.sqrt(n_lat * FNET_GROUP_DIM)), kb=kb),
        grid=(n2 // kb,),
        in_specs=[pl.BlockSpec((kb, n1, w), lambda j: (j, 0, 0)),
                  pl.BlockSpec((kb, n1, w), lambda j: (j, 0, 0)),
                  pl.BlockSpec((kb, n1, 1), lambda j: (j, 0, 0)),
                  pl.BlockSpec((kb, n1, 1), lambda j: (j, 0, 0)),
                  pl.BlockSpec((n1, n1), lambda j: (0, 0)),
                  pl.BlockSpec((n1, n1), lambda j: (0, 0)),
                  pl.BlockSpec((FNET_GROUP_DIM, FNET_GROUP_DIM), lambda j: (0, 0)),
                  pl.BlockSpec((FNET_GROUP_DIM, FNET_GROUP_DIM), lambda j: (0, 0))],
        out_specs=pl.BlockSpec((n1, kb, w), lambda j: (0, j, 0)),
        out_shape=jax.ShapeDtypeStruct((n1, n2, w), F32),
        compiler_params=_cparams(("arbitrary",)),
        name="fnet_stage_b",
    )(yr.reshape(n2, n1, w), yi.reshape(n2, n1, w), twc, tws, c1, s1, cc, sc)
    cn, sn = _dft_tables(n_ctx)
    ctx = pl.pallas_call(
        functools.partial(_fnet_ctx_kernel, scale=float(1.0 / math.sqrt(n_ctx * FNET_GROUP_DIM))),
        out_shape=jax.ShapeDtypeStruct((n_ctx, w), F32),
        compiler_params=pltpu.CompilerParams(vmem_limit_bytes=VMEM_LIMIT_BYTES),
        name="fnet_ctx",
    )(f_ctx, cn, sn, cc, sc)
    return jnp.concatenate([ctx, lat.reshape(n_lat, w)], axis=0)


def _merge_kernel(o0_ref, o1_ref, o2_ref, o3_ref, g0_ref, g1_ref, g2_ref, g3_ref, wb_ref, y_ref):
    y = None
    for b, (o_ref, g_ref) in enumerate(((o0_ref, g0_ref), (o1_ref, g1_ref), (o2_ref, g2_ref), (o3_ref, g3_ref))):
        pr = jnp.dot(o_ref[...].astype(BF16), wb_ref[0, b].astype(BF16), preferred_element_type=F32)
        t = jax.nn.sigmoid(g_ref[...]) * pr
        y = t if y is None else y + t
    y_ref[...] = y.astype(y_ref.dtype)


def merge(branches, gates, w_branch_all, layer, tm=768, tn=512):
    m = gates.shape[0]
    nb = D_MODEL // tn
    o_specs = [pl.BlockSpec((tm, BRANCH_W), lambda j, i: (i, 0)) for _ in range(N_BRANCH)]
    g_specs = [pl.BlockSpec((tm, tn), functools.partial(lambda j, i, b: (i, b * nb + j), b=b)) for b in range(N_BRANCH)]
    return pl.pallas_call(
        _merge_kernel,
        grid=(nb, m // tm),
        in_specs=o_specs + g_specs + [pl.BlockSpec((1, N_BRANCH, BRANCH_W, tn), lambda j, i: (layer, 0, 0, j))],
        out_specs=pl.BlockSpec((tm, tn), lambda j, i: (i, j)),
        out_shape=jax.ShapeDtypeStruct((m, D_MODEL), BF16),
        compiler_params=_cparams(("arbitrary", "arbitrary")),
        name="merge",
    )(*branches, gates, gates, gates, gates, w_branch_all)


def _expert_kernel(idx_ref, h_hbm, wg_ref, wu_ref, wd_ref, gate_ref, o_ref, xs32_ref, xsb_ref, sem):
    e = pl.program_id(0)
    f = pl.program_id(1)
    cap = xs32_ref.shape[0]

    @pl.when(f == 0)
    def _gather():
        def issue(c, carry):
            row = idx_ref[e, c]
            pltpu.make_async_copy(h_hbm.at[pl.ds(row, 1)], xs32_ref.at[pl.ds(c, 1)], sem).start()
            return carry

        lax.fori_loop(0, cap, issue, 0, unroll=8)
        pltpu.make_async_copy(h_hbm.at[pl.ds(0, cap)], xs32_ref, sem).wait()
        xsb_ref[...] = xs32_ref[...].astype(BF16)

    xs = xsb_ref[...]
    hg = jnp.dot(xs, wg_ref[0, 0].astype(BF16), preferred_element_type=F32)
    hu = jnp.dot(xs, wu_ref[0, 0].astype(BF16), preferred_element_type=F32)
    hid = (_silu(hg) * hu).astype(BF16)
    part = jnp.dot(hid, wd_ref[0, 0].astype(BF16), preferred_element_type=F32)

    @pl.when(f == 0)
    def _first():
        o_ref[0] = part

    @pl.when(f != 0)
    def _rest():
        o_ref[0] += part

    @pl.when(f == pl.num_programs(1) - 1)
    def _done():
        o_ref[0] = o_ref[0] * gate_ref[0]


def expert_ffn(h2, idx, gate, w_gate, w_up, w_down, layer, tf=256):
    ne, cap = idx.shape
    d = h2.shape[1]
    ff = w_gate.shape[3]
    grid_spec = pltpu.PrefetchScalarGridSpec(
        num_scalar_prefetch=1,
        grid=(ne, ff // tf),
        in_specs=[pl.BlockSpec(memory_space=pl.ANY),
                  pl.BlockSpec((1, 1, d, tf), lambda e, f, ix: (layer, e, 0, f)),
                  pl.BlockSpec((1, 1, d, tf), lambda e, f, ix: (layer, e, 0, f)),
                  pl.BlockSpec((1, 1, tf, d), lambda e, f, ix: (layer, e, f, 0)),
                  pl.BlockSpec((1, cap, 1), lambda e, f, ix: (e, 0, 0))],
        out_specs=pl.BlockSpec((1, cap, d), lambda e, f, ix: (e, 0, 0)),
        scratch_shapes=[pltpu.VMEM((cap, d), F32), pltpu.VMEM((cap, d), BF16), pltpu.SemaphoreType.DMA(())],
    )
    return pl.pallas_call(
        _expert_kernel,
        grid_spec=grid_spec,
        out_shape=jax.ShapeDtypeStruct((ne, cap, d), F32),
        compiler_params=_cparams(("arbitrary", "arbitrary")),
        name="expert_ffn",
    )(idx, h2, w_gate, w_up, w_down, gate.reshape(ne, cap, 1))


def moe(h2, aff, w_gate, w_up, w_down, layer, n_ctx):
    m = h2.shape[0]
    n_lat = m - n_ctx
    a = aff[:, :N_EXPERTS]
    g_c, i_c = lax.top_k(a[:n_ctx].T, CAPACITY_FACTOR * n_ctx // N_EXPERTS)
    g_l, i_l = lax.top_k(a[n_ctx:].T, CAPACITY_FACTOR * n_lat // N_EXPERTS)
    idx = jnp.concatenate([i_c, i_l + n_ctx], axis=1).astype(jnp.int32)
    gate = jnp.concatenate([g_c, g_l], axis=1)
    ye = expert_ffn(h2, idx, gate, w_gate, w_up, w_down, layer)
    return jnp.zeros((m, D_MODEL), F32).at[idx.reshape(-1)].add(ye.reshape(-1, D_MODEL))


_P_NA = 0
_P_Z = _P_NA + 3 * NA_HEADS * HEAD_DIM
_P_XBC = _P_Z + SSD_W
_P_DT = _P_XBC + SSD_XBC_W
_P_GQ = _P_DT + 2 * SSD_HEADS
_P_FN = _P_GQ + (GQA_HEADS + 2 * GQA_KV_HEADS) * HEAD_DIM
_P_GATES = _P_FN + FNET_GROUPS * FNET_GROUP_DIM


def _rest_weight(w_in):
    d = w_in.shape[0]
    return jnp.concatenate([w_in[:, _P_GQ:_P_GATES], w_in[:, _P_DT:_P_GQ],
                            jnp.zeros((d, FR_W - FR_DT - 2 * SSD_HEADS), w_in.dtype)], axis=1)


def _layer(x, mod, n_ctx, layer, norm_mix, norm_ffn, w_in_all, na_rpb, conv_w, conv_b, a_log, dt_bias, d_skip,
           ssd_gn, q_gain, k_gain, w_branch_all, w_out_all, w_router, w_gate_all, w_up_all, w_down_all, rope):
    m = x.shape[0]
    big = m % 2112 == 0
    tm = 768 if m % 768 == 0 else 256
    tmg = 2112 if big else 256
    h = norm_mod(x, norm_mix, mod, n_ctx, 0, tm=tm)
    ab = matmul_cols(h, w_in_all, layer, _P_NA, _P_Z - _P_NA, BF16, tm, 768, "proj_na")
    fzx = matmul_cols(h, w_in_all, layer, _P_Z, _P_DT - _P_Z, F32, tm, 512, "proj_zx", out_perm=(2, 0, 1))
    fr = matmul(h, _rest_weight(w_in_all[layer]), F32, tm, FR_W // 2, "proj_rest")
    gates = matmul_cols_unaligned(h, w_in_all, layer, _P_GATES, N_BRANCH * D_MODEL, F32, tmg, 512, "proj_gates")

    rows = (m - n_ctx) // GRID_W
    bias, tid = na_bias_table(na_rpb, rows)
    o_na = na_attention(ab, bias, tid, n_ctx)

    u, dt = ssd_prep(fzx, fr, conv_w, conv_b, dt_bias, n_ctx)
    y2 = ssd_scan(u, dt, a_log, n_ctx)
    o_ssd = ssd_out(y2, u, fzx, d_skip, ssd_gn, tm=tm)

    qt, kn, vt = qk_prep(fr, rope[0], rope[1], q_gain, k_gain, tm=tm)
    o_gqa = gqa_attention(qt, kn, vt, n_ctx)

    f_in = fr[:, FR_FN:FR_FN + FNET_GROUPS * FNET_GROUP_DIM]
    o_fn = fourier_mix(f_in[n_ctx:], f_in[:n_ctx])

    y = merge((o_na, o_ssd, o_gqa, o_fn), gates, w_branch_all, layer, tm=tm)
    x = matmul_residual(y, w_out_all, layer, x, mod, 2, n_ctx, tm, 1024, "out_proj")

    wr = jnp.concatenate([w_router, jnp.zeros((D_MODEL, LANES - N_EXPERTS), w_router.dtype)], axis=1)
    h2, aff = norm_router(x, norm_ffn, mod, wr, n_ctx, 3, tm=tm)
    mo = moe(h2, aff, w_gate_all, w_up_all, w_down_all, layer, n_ctx)
    row = jnp.arange(m)[:, None]
    g2 = jnp.where(row < n_ctx, mod[1, 5 * D_MODEL:][None], mod[0, 5 * D_MODEL:][None])
    return x + g2 * mo


def kernel(x, c, ctx, c_ctx, w_ada, b_ada, norm_mix, norm_ffn, w_in, na_rpb, ssd_conv_w, ssd_conv_b, ssd_a_log,
           ssd_dt_bias, ssd_d, ssd_norm, gqa_q_norm, gqa_k_norm, w_branch, w_out, w_router, moe_w_gate, moe_w_up,
           moe_w_down, final_norm_w):
    n_lat = x.shape[1]
    n_ctx = ctx.shape[1]
    depth = w_ada.shape[0]
    cc = jnp.concatenate([c[0:1], c_ctx[None], jnp.zeros((6, D_MODEL), F32)], axis=0)
    mods = ada_all(cc, w_ada, b_ada)
    rope = rope_tables(n_ctx, n_lat)
    xs = jnp.concatenate([ctx[0], x[0]], axis=0)
    for l in range(depth):
        xs = _layer(xs, mods[l], n_ctx, l, norm_mix[l], norm_ffn[l], w_in, na_rpb[l], ssd_conv_w[l],
                    ssd_conv_b[l], ssd_a_log[l], ssd_dt_bias[l], ssd_d[l], ssd_norm[l], gqa_q_norm[l],
                    gqa_k_norm[l], w_branch, w_out, w_router[l], moe_w_gate, moe_w_up, moe_w_down, rope)
    out = final_norm(xs, final_norm_w, n_ctx, n_lat)
    return out[None]
```

```python
import functools
import math

import numpy as np
import jax
import jax.numpy as jnp
from jax import lax
from jax.experimental import pallas as pl
from jax.experimental.pallas import tpu as pltpu

F32 = jnp.float32
BF16 = jnp.bfloat16

D_MODEL = 2048
GRID_W = 64
HEAD_DIM = 128
RMS_EPS = 1e-6
N_BRANCH = 4
BRANCH_W = 512
NA_HEADS = 4
NA_WIN_R = 8
NA_WIN_C = 16
SSD_HEADS = 8
SSD_HEAD_DIM = 64
SSD_GROUPS = 2
SSD_STATE = 128
SSD_CONV = 5
SSD_CHUNK = 128
SSD_W = SSD_HEADS * SSD_HEAD_DIM
SSD_XBC_W = SSD_W + 2 * SSD_GROUPS * SSD_STATE
GQA_HEADS = 4
GQA_KV_HEADS = 2
ROPE_THETA = 10000.0
FNET_GROUPS = 4
FNET_GROUP_DIM = 128
N_EXPERTS = 16
EXPERT_FF = D_MODEL // 2
CAPACITY_FACTOR = 2
LOG2E = 1.4426950408889634

VMEM_LIMIT_BYTES = 56 * 1024 * 1024
LANES = 128

ZX_XBC, ZX_Z, ZX_W = 0, 1024, 1536
FR_Q, FR_K, FR_V, FR_FN, FR_DT, FR_W = 0, 512, 768, 1024, 1536, 1792


def _cparams(sem):
    return pltpu.CompilerParams(dimension_semantics=sem, vmem_limit_bytes=VMEM_LIMIT_BYTES)


def _silu(x):
    return x * jax.nn.sigmoid(x)


def _ada_kernel(ct_ref, w_ref, b_ref, o_ref):
    a = _silu(ct_ref[...])
    w = w_ref[0]
    rows = [jnp.sum(w * a[:, r:r + 1], axis=0, keepdims=True) + b_ref[0] for r in range(2)]
    o_ref[0] = jnp.concatenate(rows + [jnp.zeros((6, w.shape[1]), F32)], axis=0)


def ada_all(cc, w_ada, b_ada):
    depth, d, n = w_ada.shape
    tn = 1024
    return pl.pallas_call(
        _ada_kernel,
        grid=(depth, n // tn),
        in_specs=[pl.BlockSpec((d, 8), lambda l, j: (0, 0)),
                  pl.BlockSpec((1, d, tn), lambda l, j: (l, 0, j)),
                  pl.BlockSpec((1, 1, tn), lambda l, j: (l, 0, j))],
        out_specs=pl.BlockSpec((1, 8, tn), lambda l, j: (l, 0, j)),
        out_shape=jax.ShapeDtypeStruct((depth, 8, n), F32),
        compiler_params=_cparams(("arbitrary", "arbitrary")),
        name="ada_mod",
    )(cc.T, w_ada, b_ada.reshape(depth, 1, n))


def _row_select(mod_ref, chunk, row_is_ctx):
    lo = chunk * D_MODEL
    return jnp.where(row_is_ctx, mod_ref[1:2, lo:lo + D_MODEL], mod_ref[0:1, lo:lo + D_MODEL])


def _norm_body(x_ref, nw_ref):
    x = x_ref[...]
    ms = jnp.mean(x * x, axis=-1, keepdims=True)
    return x * lax.rsqrt(ms + RMS_EPS) * nw_ref[...]


def _norm_mod_kernel(x_ref, nw_ref, mod_ref, o_ref, *, n_ctx, tm, sh_chunk):
    y = _norm_body(x_ref, nw_ref)
    row = pl.program_id(0) * tm + lax.broadcasted_iota(jnp.int32, (tm, 1), 0)
    is_ctx = row < n_ctx
    sh = _row_select(mod_ref, sh_chunk, is_ctx)
    sc = _row_select(mod_ref, sh_chunk + 1, is_ctx)
    o_ref[...] = (y * (1.0 + sc) + sh).astype(o_ref.dtype)


def norm_mod(x, nw, mod, n_ctx, sh_chunk, tm=768):
    m, d = x.shape
    return pl.pallas_call(
        functools.partial(_norm_mod_kernel, n_ctx=n_ctx, tm=tm, sh_chunk=sh_chunk),
        grid=(m // tm,),
        in_specs=[pl.BlockSpec((tm, d), lambda i: (i, 0)),
                  pl.BlockSpec((1, d), lambda i: (0, 0)),
                  pl.BlockSpec(mod.shape, lambda i: (0, 0))],
        out_specs=pl.BlockSpec((tm, d), lambda i: (i, 0)),
        out_shape=jax.ShapeDtypeStruct((m, d), BF16),
        compiler_params=_cparams(("arbitrary",)),
        name="norm_mod",
    )(x, nw.reshape(1, d), mod)


def _norm_router_kernel(x_ref, nw_ref, mod_ref, wr_ref, h_ref, aff_ref, *, n_ctx, tm, sh_chunk):
    y = _norm_body(x_ref, nw_ref)
    row = pl.program_id(0) * tm + lax.broadcasted_iota(jnp.int32, (tm, 1), 0)
    is_ctx = row < n_ctx
    sh = _row_select(mod_ref, sh_chunk, is_ctx)
    sc = _row_select(mod_ref, sh_chunk + 1, is_ctx)
    h = y * (1.0 + sc) + sh
    h_ref[...] = h
    logits = jnp.dot(h.astype(BF16), wr_ref[...].astype(BF16), preferred_element_type=F32)
    lane = lax.broadcasted_iota(jnp.int32, logits.shape, 1)
    logits = jnp.where(lane < N_EXPERTS, logits, -jnp.inf)
    mx = jnp.max(logits, axis=-1, keepdims=True)
    e = jnp.exp(logits - mx)
    aff_ref[...] = e / jnp.sum(e, axis=-1, keepdims=True)


def norm_router(x, nw, mod, w_router_pad, n_ctx, sh_chunk, tm=768):
    m, d = x.shape
    return pl.pallas_call(
        functools.partial(_norm_router_kernel, n_ctx=n_ctx, tm=tm, sh_chunk=sh_chunk),
        grid=(m // tm,),
        in_specs=[pl.BlockSpec((tm, d), lambda i: (i, 0)),
                  pl.BlockSpec((1, d), lambda i: (0, 0)),
                  pl.BlockSpec(mod.shape, lambda i: (0, 0)),
                  pl.BlockSpec((d, LANES), lambda i: (0, 0))],
        out_specs=[pl.BlockSpec((tm, d), lambda i: (i, 0)),
                   pl.BlockSpec((tm, LANES), lambda i: (i, 0))],
        out_shape=[jax.ShapeDtypeStruct((m, d), F32),
                   jax.ShapeDtypeStruct((m, LANES), F32)],
        compiler_params=_cparams(("arbitrary",)),
        name="norm_router",
    )(x, nw.reshape(1, d), mod, w_router_pad)


def _final_norm_kernel(x_ref, nw_ref, o_ref):
    o_ref[...] = _norm_body(x_ref, nw_ref)


def final_norm(x, nw, row0, n_rows, tm=256):
    d = x.shape[1]
    off = row0 // tm
    return pl.pallas_call(
        _final_norm_kernel,
        grid=(n_rows // tm,),
        in_specs=[pl.BlockSpec((tm, d), lambda i: (i + off, 0)),
                  pl.BlockSpec((1, d), lambda i: (0, 0))],
        out_specs=pl.BlockSpec((tm, d), lambda i: (i, 0)),
        out_shape=jax.ShapeDtypeStruct((n_rows, d), F32),
        compiler_params=_cparams(("arbitrary",)),
        name="final_norm",
    )(x, nw.reshape(1, d))


def _mm_kernel(a_ref, w_ref, o_ref):
    o_ref[...] = jnp.dot(a_ref[...].astype(BF16), w_ref[...].astype(BF16),
                         preferred_element_type=F32).astype(o_ref.dtype)


def matmul(a, w, out_dtype, tm, tn, name):
    m, k = a.shape
    n = w.shape[1]
    return pl.pallas_call(
        _mm_kernel,
        grid=(n // tn, m // tm),
        in_specs=[pl.BlockSpec((tm, k), lambda j, i: (i, 0)),
                  pl.BlockSpec((k, tn), lambda j, i: (0, j))],
        out_specs=pl.BlockSpec((tm, tn), lambda j, i: (i, j)),
        out_shape=jax.ShapeDtypeStruct((m, n), out_dtype),
        compiler_params=_cparams(("arbitrary", "arbitrary")),
        name=name,
    )(a, w)


def _mm_layer_kernel(a_ref, w_ref, o_ref):
    o_ref[...] = jnp.dot(a_ref[...], w_ref[0].astype(BF16), preferred_element_type=F32).astype(o_ref.dtype)


def matmul_cols(a, w_all, layer, col0, ncols, out_dtype, tm, tn, name, out_perm=None):
    m, k = a.shape
    nb = ncols // tn
    c0 = col0 // tn
    perm = tuple(range(nb)) if out_perm is None else tuple(out_perm)

    def out_map(j, i):
        pj = j
        for src, dst in enumerate(perm):
            pj = jnp.where(j == src, dst, pj)
        return (i, pj)

    return pl.pallas_call(
        _mm_layer_kernel,
        grid=(nb, m // tm),
        in_specs=[pl.BlockSpec((tm, k), lambda j, i: (i, 0)),
                  pl.BlockSpec((1, k, tn), lambda j, i: (layer, 0, c0 + j))],
        out_specs=pl.BlockSpec((tm, tn), out_map),
        out_shape=jax.ShapeDtypeStruct((m, ncols), out_dtype),
        compiler_params=_cparams(("arbitrary", "arbitrary")),
        name=name,
    )(a, w_all)


def _mm_shift_kernel(a_ref, wa_ref, wb_ref, o_ref, ws_ref, *, shift, tn):
    @pl.when(pl.program_id(1) == 0)
    def _realign():
        wcat = jnp.concatenate([wa_ref[0], wb_ref[0]], axis=1)
        ws_ref[...] = wcat[:, shift:shift + tn].astype(BF16)

    o_ref[...] = jnp.dot(a_ref[...], ws_ref[...], preferred_element_type=F32).astype(o_ref.dtype)


def matmul_cols_unaligned(a, w_all, layer, col0, ncols, out_dtype, tm, tn, name):
    m, k = a.shape
    base = (col0 // LANES) * LANES
    shift = col0 - base
    assert base % tn == 0 and ncols % tn == 0 and 0 < shift < LANES
    c0 = base // tn
    r = tn // LANES
    return pl.pallas_call(
        functools.partial(_mm_shift_kernel, shift=shift, tn=tn),
        grid=(ncols // tn, m // tm),
        in_specs=[pl.BlockSpec((tm, k), lambda j, i: (i, 0)),
                  pl.BlockSpec((1, k, tn), lambda j, i: (layer, 0, c0 + j)),
                  pl.BlockSpec((1, k, LANES), lambda j, i: (layer, 0, (c0 + j + 1) * r))],
        out_specs=pl.BlockSpec((tm, tn), lambda j, i: (i, j)),
        out_shape=jax.ShapeDtypeStruct((m, ncols), out_dtype),
        scratch_shapes=[pltpu.VMEM((k, tn), BF16)],
        compiler_params=_cparams(("arbitrary", "arbitrary")),
        name=name,
    )(a, w_all, w_all)


def _mm_residual_kernel(a_ref, w_ref, x_ref, g_ref, o_ref, *, n_ctx, tm):
    acc = jnp.dot(a_ref[...].astype(BF16), w_ref[0].astype(BF16), preferred_element_type=F32)
    row = pl.program_id(1) * tm + lax.broadcasted_iota(jnp.int32, (tm, 1), 0)
    g = jnp.where(row < n_ctx, g_ref[1:2, :], g_ref[0:1, :])
    o_ref[...] = x_ref[...] + g * acc


def matmul_residual(a, w_all, layer, x, mod, gate_chunk, n_ctx, tm, tn, name):
    m, k = a.shape
    n = w_all.shape[2]
    goff = gate_chunk * D_MODEL // tn
    return pl.pallas_call(
        functools.partial(_mm_residual_kernel, n_ctx=n_ctx, tm=tm),
        grid=(n // tn, m // tm),
        in_specs=[pl.BlockSpec((tm, k), lambda j, i: (i, 0)),
                  pl.BlockSpec((1, k, tn), lambda j, i: (layer, 0, j)),
                  pl.BlockSpec((tm, tn), lambda j, i: (i, j)),
                  pl.BlockSpec((8, tn), lambda j, i: (0, goff + j))],
        out_specs=pl.BlockSpec((tm, tn), lambda j, i: (i, j)),
        out_shape=jax.ShapeDtypeStruct((m, n), F32),
        compiler_params=_cparams(("arbitrary", "arbitrary")),
        name=name,
    )(a, w_all, x, mod)


def _softmax_pv(parts):
    m = None
    for s, _ in parts:
        mi = jnp.max(s, axis=-1, keepdims=True)
        m = mi if m is None else jnp.maximum(m, mi)
    ps, l = [], None
    for s, _ in parts:
        p = jnp.exp(s - m)
        ps.append(p)
        li = jnp.sum(p, axis=-1, keepdims=True)
        l = li if l is None else l + li
    inv = 1.0 / l
    o = None
    for p, (_, v) in zip(ps, parts):
        oi = jnp.dot((p * inv).astype(BF16), v, preferred_element_type=F32)
        o = oi if o is None else o + oi
    return o


def _qk(q, k):
    return lax.dot_general(q, k, (((1,), (1,)), ((), ())), preferred_element_type=F32)


NA_QROWS = 4
NA_KROWS = NA_WIN_R + NA_QROWS - 1


def _na_kernel(tid_ref, q_ref, k_ref, v_ref, b_ref, o_ref, *, n_ctx, rows):
    del tid_ref
    step = pl.program_id(0)
    tq = NA_QROWS * GRID_W
    n_ctx_steps = n_ctx // tq
    scale = HEAD_DIM ** -0.5
    win = NA_KROWS * GRID_W

    @pl.when(step < n_ctx_steps)
    def _ctx():
        for h in range(NA_HEADS):
            cs = slice(h * HEAD_DIM, (h + 1) * HEAD_DIM)
            q = q_ref[:, cs]
            s_c = _qk(q, k_ref[0:n_ctx, cs]) * scale
            o_ref[:, cs] = _softmax_pv([(s_c, v_ref[0:n_ctx, cs])]).astype(o_ref.dtype)

    @pl.when(step >= n_ctx_steps)
    def _lat():
        p = step - n_ctx_steps
        row0 = jnp.clip(NA_QROWS * p - NA_WIN_R // 2, 0, rows - NA_KROWS)
        base = pl.multiple_of(n_ctx + row0 * GRID_W, GRID_W)
        for h in range(NA_HEADS):
            cs = slice(h * HEAD_DIM, (h + 1) * HEAD_DIM)
            q = q_ref[:, cs]
            s_w = _qk(q, k_ref[pl.ds(base, win), cs]) * scale + b_ref[0, h]
            s_c = _qk(q, k_ref[0:n_ctx, cs]) * scale
            o = _softmax_pv([(s_w, v_ref[pl.ds(base, win), cs]), (s_c, v_ref[0:n_ctx, cs])])
            o_ref[:, cs] = o.astype(o_ref.dtype)


def _na_patterns(rows):
    a = np.arange(NA_QROWS)[:, None]
    i = np.arange(NA_KROWS)[None, :]
    pats, keys, tid = [], [], []
    for p in range(rows // NA_QROWS):
        row0 = int(np.clip(NA_QROWS * p - NA_WIN_R // 2, 0, rows - NA_KROWS))
        r = NA_QROWS * p + a
        rs = np.clip(r - NA_WIN_R // 2, 0, rows - NA_WIN_R)
        krow = row0 + i
        valid = (krow >= rs) & (krow < rs + NA_WIN_R)
        ridx = np.clip(krow - r + NA_WIN_R - 1, 0, 2 * NA_WIN_R - 2)
        key = (valid.tobytes(), ridx.tobytes())
        if key not in keys:
            keys.append(key)
            pats.append((valid, ridx))
        tid.append(keys.index(key))
    return np.stack([v for v, _ in pats]), np.stack([x for _, x in pats]), np.asarray(tid, np.int32)


def na_bias_table(rpb, rows):
    valid_r, ridx, tid = _na_patterns(rows)
    kc = np.arange(GRID_W)[None, :]
    qc = np.arange(GRID_W)[:, None]
    col_start = np.clip(qc - NA_WIN_C // 2, 0, GRID_W - NA_WIN_C)
    valid_c = (kc >= col_start) & (kc < col_start + NA_WIN_C)
    cidx = np.clip(kc - qc + NA_WIN_C - 1, 0, 2 * NA_WIN_C - 2)
    nh, nbr, nbc = rpb.shape
    onehot = (np.arange(nbc)[:, None, None] == cidx[None]) & valid_c[None]
    toe = jnp.einsum('hbj,jqk->hbqk', rpb.astype(F32), jnp.asarray(onehot, F32), precision=lax.Precision.HIGHEST)
    toe = toe + jnp.asarray(np.where(valid_c, 0.0, -np.inf), F32)
    neg = jnp.full((nh, GRID_W, GRID_W), -jnp.inf, F32)
    npat = valid_r.shape[0]
    blocks = [toe[:, int(ridx[p, a, i])] if valid_r[p, a, i] else neg
              for p in range(npat) for a in range(NA_QROWS) for i in range(NA_KROWS)]
    t = jnp.stack(blocks, axis=0).reshape(npat, NA_QROWS, NA_KROWS, nh, GRID_W, GRID_W)
    t = jnp.transpose(t, (0, 3, 1, 4, 2, 5))
    return t.reshape(npat, nh, NA_QROWS * GRID_W, NA_KROWS * GRID_W), jnp.asarray(tid)


def na_attention(ab, bias, tid, n_ctx):
    m = ab.shape[0]
    rows = (m - n_ctx) // GRID_W
    tq = NA_QROWS * GRID_W
    n_ctx_steps = n_ctx // tq
    w = NA_HEADS * HEAD_DIM
    grid_spec = pltpu.PrefetchScalarGridSpec(
        num_scalar_prefetch=1,
        grid=(m // tq,),
        in_specs=[pl.BlockSpec((tq, w), lambda s, t: (s, 0)),
                  pl.BlockSpec((m, w), lambda s, t: (0, 1)),
                  pl.BlockSpec((m, w), lambda s, t: (0, 2)),
                  pl.BlockSpec((1, NA_HEADS, tq, NA_KROWS * GRID_W),
                               lambda s, t: (t[jnp.maximum(s - n_ctx_steps, 0)], 0, 0, 0))],
        out_specs=pl.BlockSpec((tq, w), lambda s, t: (s, 0)),
    )
    return pl.pallas_call(
        functools.partial(_na_kernel, n_ctx=n_ctx, rows=rows),
        grid_spec=grid_spec,
        out_shape=jax.ShapeDtypeStruct((m, w), BF16),
        compiler_params=_cparams(("arbitrary",)),
        name="na_attn",
    )(tid, ab, ab, ab, bias)


def rope_tables(n_ctx, n_lat):
    t = jnp.arange(n_lat)
    pos = jnp.stack([t // GRID_W, t % GRID_W], axis=-1).astype(F32)
    n_freq = HEAD_DIM // 4
    inv = ROPE_THETA ** (-jnp.arange(n_freq, dtype=F32) / n_freq)
    ang = pos[:, :, None] * inv
    c, s = jnp.cos(ang), jnp.sin(ang)
    cos = jnp.concatenate([c[:, 0], c[:, 0], c[:, 1], c[:, 1]], axis=-1)
    sin = jnp.concatenate([-s[:, 0], s[:, 0], -s[:, 1], s[:, 1]], axis=-1)
    cos = jnp.concatenate([jnp.ones((n_ctx, HEAD_DIM), F32), cos], axis=0)
    sin = jnp.concatenate([jnp.zeros((n_ctx, HEAD_DIM), F32), sin], axis=0)
    return cos, sin


def _qk_prep_kernel(x_ref, cos_ref, sin_ref, qg_ref, kg_ref, qt_ref, ko_ref, vt_ref):
    cos = cos_ref[...]
    sin = sin_ref[...]
    lane = lax.broadcasted_iota(jnp.int32, (1, HEAD_DIM), 1)
    first_half = (lane % (HEAD_DIM // 2)) < (HEAD_DIM // 4)
    for h in range(GQA_HEADS + GQA_KV_HEADS):
        x = x_ref[:, h * HEAD_DIM:(h + 1) * HEAD_DIM]
        gain = qg_ref[...] if h < GQA_HEADS else kg_ref[...]
        y = x * lax.rsqrt(jnp.mean(x * x, axis=-1, keepdims=True) + RMS_EPS) * gain
        sw = jnp.where(first_half, pltpu.roll(y, HEAD_DIM - HEAD_DIM // 4, 1), pltpu.roll(y, HEAD_DIM // 4, 1))
        out = y * cos + sw * sin
        if h < GQA_HEADS:
            qt_ref[h * HEAD_DIM:(h + 1) * HEAD_DIM, :] = out.T.astype(BF16)
        else:
            hk = h - GQA_HEADS
            ko_ref[:, hk * HEAD_DIM:(hk + 1) * HEAD_DIM] = out.astype(BF16)
    v0 = (GQA_HEADS + GQA_KV_HEADS) * HEAD_DIM
    for hv in range(GQA_KV_HEADS):
        v = x_ref[:, v0 + hv * HEAD_DIM:v0 + (hv + 1) * HEAD_DIM]
        vt_ref[hv * HEAD_DIM:(hv + 1) * HEAD_DIM, :] = v.T.astype(BF16)


def qk_prep(fr, cos, sin, q_gain, k_gain, tm=768):
    m = fr.shape[0]
    wq, wk = GQA_HEADS * HEAD_DIM, GQA_KV_HEADS * HEAD_DIM
    return pl.pallas_call(
        _qk_prep_kernel,
        grid=(m // tm,),
        in_specs=[pl.BlockSpec((tm, wq + 2 * wk), lambda i: (i, 0)),
                  pl.BlockSpec((tm, HEAD_DIM), lambda i: (i, 0)),
                  pl.BlockSpec((tm, HEAD_DIM), lambda i: (i, 0)),
                  pl.BlockSpec((1, HEAD_DIM), lambda i: (0, 0)),
                  pl.BlockSpec((1, HEAD_DIM), lambda i: (0, 0))],
        out_specs=[pl.BlockSpec((wq, tm), lambda i: (0, i)),
                   pl.BlockSpec((tm, wk), lambda i: (i, 0)),
                   pl.BlockSpec((wk, tm), lambda i: (0, i))],
        out_shape=[jax.ShapeDtypeStruct((wq, m), BF16),
                   jax.ShapeDtypeStruct((m, wk), BF16),
                   jax.ShapeDtypeStruct((wk, m), BF16)],
        compiler_params=_cparams(("arbitrary",)),
        name="qk_prep",
    )(fr, cos, sin, q_gain.reshape(1, HEAD_DIM), k_gain.reshape(1, HEAD_DIM))


def _gqa_kernel(qt_ref, k_ref, vt_ref, o_ref, *, n_ctx, n_all, tq, tk):
    qi = pl.program_id(1)
    c1 = (HEAD_DIM ** -0.5) * LOG2E
    n_ctx_tiles = n_ctx // tq
    rep = GQA_HEADS // GQA_KV_HEADS

    def attend(n_keys, tkk):
        nchunks = n_keys // tkk

        def scores(c):
            kc = k_ref[c * tkk:(c + 1) * tkk, :]
            return [jnp.dot(kc, qt_ref[r * HEAD_DIM:(r + 1) * HEAD_DIM, :], preferred_element_type=F32)
                    for r in range(rep)]

        state = [None] * rep
        nxt = scores(0)
        for c in range(nchunks):
            cur = nxt
            if c + 1 < nchunks:
                nxt = scores(c + 1)
            vtc = vt_ref[:, c * tkk:(c + 1) * tkk]
            for r in range(rep):
                st = cur[r]
                mc = jnp.max(st, axis=0, keepdims=True)
                if state[r] is None:
                    m_new = mc
                    p = jnp.exp2(st * c1 - m_new * c1)
                    l = jnp.sum(p, axis=0, keepdims=True)
                    acc = jnp.dot(vtc, p.astype(BF16), preferred_element_type=F32)
                else:
                    m, l, acc = state[r]
                    m_new = jnp.maximum(m, mc)
                    p = jnp.exp2(st * c1 - m_new * c1)
                    alpha = jnp.exp2((m - m_new) * c1)
                    l = alpha * l + jnp.sum(p, axis=0, keepdims=True)
                    acc = alpha * acc + jnp.dot(vtc, p.astype(BF16), preferred_element_type=F32)
                state[r] = (m_new, l, acc)
        for r in range(rep):
            _, l, acc = state[r]
            o_ref[:, r * HEAD_DIM:(r + 1) * HEAD_DIM] = (acc / l).T.astype(o_ref.dtype)

    @pl.when(qi < n_ctx_tiles)
    def _ctx():
        attend(n_ctx, n_ctx)

    @pl.when(qi >= n_ctx_tiles)
    def _lat():
        attend(n_all, tk)


def gqa_attention(qt, kn, vt, n_ctx, tq=256, tk=768):
    m = kn.shape[0]
    rep = GQA_HEADS // GQA_KV_HEADS
    if m % tk:
        tk = 256
    return pl.pallas_call(
        functools.partial(_gqa_kernel, n_ctx=n_ctx, n_all=m, tq=tq, tk=tk),
        grid=(GQA_KV_HEADS, m // tq),
        in_specs=[pl.BlockSpec((rep * HEAD_DIM, tq), lambda g, i: (g, i)),
                  pl.BlockSpec((m, HEAD_DIM), lambda g, i: (0, g)),
                  pl.BlockSpec((HEAD_DIM, m), lambda g, i: (g, 0))],
        out_specs=pl.BlockSpec((tq, rep * HEAD_DIM), lambda g, i: (i, g)),
        out_shape=jax.ShapeDtypeStruct((m, GQA_HEADS * HEAD_DIM), BF16),
        compiler_params=_cparams(("arbitrary", "arbitrary")),
        name="gqa_attn",
    )(qt, kn, vt)


def _ssd_prep_kernel(prev_ref, x_ref, next_ref, dtr_ref, cw_ref, cb_ref, dtb_ref, u_ref, dt_ref, *, n_ctx, n_all, tm):
    i = pl.program_id(0)
    lo = i * tm
    hi = lo + tm
    top_ok = jnp.logical_and(lo != 0, lo != n_ctx)
    bot_ok = jnp.logical_and(hi != n_ctx, hi != n_all)
    prev = jnp.where(top_ok, prev_ref[...], 0.0)
    nxt = jnp.where(bot_ok, next_ref[...], 0.0)
    ext = jnp.concatenate([prev, x_ref[...], nxt], axis=0)
    half = SSD_CONV // 2
    acc = None
    for j in range(SSD_CONV):
        sl = ext[8 - half + j:8 - half + j + tm, :]
        term = sl * cw_ref[j:j + 1, :]
        acc = term if acc is None else acc + term
    u_ref[...] = _silu(acc + cb_ref[...])
    dt_ref[...] = jax.nn.softplus(dtr_ref[...] + dtb_ref[...])


def ssd_prep(fzx, fr, conv_w, conv_b, dt_bias, n_ctx, tm=256):
    m = fzx.shape[0]
    nb8 = tm // 8
    last8 = m // 8 - 1
    cw = jnp.concatenate([conv_w, jnp.zeros((8 - SSD_CONV, SSD_XBC_W), F32)], axis=0)
    dtb = jnp.concatenate([dt_bias.reshape(-1), jnp.zeros((LANES - 2 * SSD_HEADS,), F32)]).reshape(1, LANES)
    return pl.pallas_call(
        functools.partial(_ssd_prep_kernel, n_ctx=n_ctx, n_all=m, tm=tm),
        grid=(m // tm,),
        in_specs=[pl.BlockSpec((8, SSD_XBC_W), lambda i: (jnp.maximum(i * nb8 - 1, 0), 0)),
                  pl.BlockSpec((tm, SSD_XBC_W), lambda i: (i, 0)),
                  pl.BlockSpec((8, SSD_XBC_W), lambda i: (jnp.minimum((i + 1) * nb8, last8), 0)),
                  pl.BlockSpec((tm, LANES), lambda i: (i, FR_DT // LANES)),
                  pl.BlockSpec((8, SSD_XBC_W), lambda i: (0, 0)),
                  pl.BlockSpec((1, SSD_XBC_W), lambda i: (0, 0)),
                  pl.BlockSpec((1, LANES), lambda i: (0, 0))],
        out_specs=[pl.BlockSpec((tm, SSD_XBC_W), lambda i: (i, 0)),
                   pl.BlockSpec((tm, LANES), lambda i: (i, 0))],
        out_shape=[jax.ShapeDtypeStruct((m, SSD_XBC_W), F32),
                   jax.ShapeDtypeStruct((m, LANES), F32)],
        compiler_params=_cparams(("arbitrary",)),
        name="ssd_prep",
    )(fzx, fzx, fzx, fr, cw, conv_b.reshape(1, SSD_XBC_W), dtb)


def _ssd_scan_kernel(u_ref, dt_ref, alog_ref, y_ref, ht_ref):
    d = pl.program_id(0)
    s = pl.program_id(1)
    ln = SSD_CHUNK
    p = SSD_HEAD_DIM
    ns = SSD_STATE
    epg = SSD_HEADS // SSD_GROUPS

    @pl.when(s == 0)
    def _init():
        ht_ref[...] = jnp.zeros_like(ht_ref)

    fwd = d == 0
    dt_all = dt_ref[...]
    dt = jnp.where(fwd, dt_all, pltpu.roll(dt_all, LANES - SSD_HEADS, 1))
    a = dt * (-jnp.exp(alog_ref[0]))
    li = lax.broadcasted_iota(jnp.int32, (ln, ln), 0)
    si = lax.broadcasted_iota(jnp.int32, (ln, ln), 1)
    mask = jnp.where(fwd, li - si, si - li) >= 0
    tri = mask.astype(F32)
    a_cum = jnp.dot(tri, a, preferred_element_type=F32, precision=lax.Precision.HIGHEST)
    a_cum_t = a_cum.T
    a_tot = jnp.where(fwd, a_cum[ln - 1:ln, :], a_cum[0:1, :])
    w_all = jnp.exp(a_tot - a_cum)
    ea_all = jnp.exp(a_cum)
    eat = jnp.exp(a_tot)
    for g in range(SSD_GROUPS):
        bg = u_ref[:, SSD_W + g * ns:SSD_W + (g + 1) * ns]
        cg = u_ref[:, SSD_W + SSD_GROUPS * ns + g * ns:SSD_W + SSD_GROUPS * ns + (g + 1) * ns]
        cgb = cg.astype(BF16)
        cb = _qk(cgb, bg.astype(BF16))
        bgt = bg.T.astype(BF16)
        for e in range(epg):
            h = g * epg + e
            ac = a_cum[:, h:h + 1]
            act = a_cum_t[h:h + 1, :]
            decay = jnp.exp(jnp.where(mask, ac - act, -jnp.inf))
            mm = (cb * decay).astype(BF16)
            xdt = u_ref[:, h * p:(h + 1) * p] * dt[:, h:h + 1]
            ht = ht_ref[h]
            y = (jnp.dot(mm, xdt.astype(BF16), preferred_element_type=F32)
                 + ea_all[:, h:h + 1] * jnp.dot(cgb, ht.astype(BF16), preferred_element_type=F32))
            st = jnp.dot(bgt, (xdt * w_all[:, h:h + 1]).astype(BF16), preferred_element_type=F32)
            ht_ref[h] = eat[:, h:h + 1] * ht + st
            y_ref[0, :, h * p:(h + 1) * p] = y


def ssd_scan(u, dt, a_log, n_ctx):
    m = u.shape[0]
    nc = m // SSD_CHUNK
    ncc = n_ctx // SSD_CHUNK

    def chunk_of(d, s):
        bwd = jnp.where(s < ncc, ncc - 1 - s, ncc + nc - 1 - s)
        return jnp.where(d == 0, s, bwd)

    al = jnp.concatenate([a_log, jnp.zeros((2, LANES - SSD_HEADS), F32)], axis=1).reshape(2, 1, LANES)
    return pl.pallas_call(
        _ssd_scan_kernel,
        grid=(2, nc),
        in_specs=[pl.BlockSpec((SSD_CHUNK, SSD_XBC_W), lambda d, s: (chunk_of(d, s), 0)),
                  pl.BlockSpec((SSD_CHUNK, LANES), lambda d, s: (chunk_of(d, s), 0)),
                  pl.BlockSpec((1, 1, LANES), lambda d, s: (d, 0, 0))],
        out_specs=pl.BlockSpec((1, SSD_CHUNK, SSD_W), lambda d, s: (d, chunk_of(d, s), 0)),
        out_shape=jax.ShapeDtypeStruct((2, m, SSD_W), F32),
        scratch_shapes=[pltpu.VMEM((SSD_HEADS, SSD_STATE, SSD_HEAD_DIM), F32)],
        compiler_params=_cparams(("arbitrary", "arbitrary")),
        name="ssd_scan",
    )(u, dt, al)


def _ssd_out_kernel(yf_ref, yb_ref, xs_ref, z_ref, dsk_ref, nw_ref, o_ref):
    y = yf_ref[0] + yb_ref[0] + dsk_ref[...] * xs_ref[...]
    gated = y * _silu(z_ref[...])
    gw = SSD_W // SSD_GROUPS
    for g in range(SSD_GROUPS):
        blk = gated[:, g * gw:(g + 1) * gw]
        nrm = blk * lax.rsqrt(jnp.mean(blk * blk, axis=-1, keepdims=True) + RMS_EPS)
        o_ref[:, g * gw:(g + 1) * gw] = (nrm * nw_ref[:, g * gw:(g + 1) * gw]).astype(o_ref.dtype)


def ssd_out(y2, u, fb, d_skip, norm_w, tm=768):
    m = u.shape[0]
    dsk = jnp.repeat(d_skip.astype(F32), SSD_HEAD_DIM).reshape(1, SSD_W)
    return pl.pallas_call(
        _ssd_out_kernel,
        grid=(m // tm,),
        in_specs=[pl.BlockSpec((1, tm, SSD_W), lambda i: (0, i, 0)),
                  pl.BlockSpec((1, tm, SSD_W), lambda i: (1, i, 0)),
                  pl.BlockSpec((tm, SSD_W), lambda i: (i, 0)),
                  pl.BlockSpec((tm, SSD_W), lambda i: (i, ZX_Z // SSD_W)),
                  pl.BlockSpec((1, SSD_W), lambda i: (0, 0)),
                  pl.BlockSpec((1, SSD_W), lambda i: (0, 0))],
        out_specs=pl.BlockSpec((tm, SSD_W), lambda i: (i, 0)),
        out_shape=jax.ShapeDtypeStruct((m, SSD_W), BF16),
        compiler_params=_cparams(("arbitrary",)),
        name="ssd_out",
    )(y2, y2, u, fb, dsk, norm_w.reshape(1, SSD_W))


def _dft_tables(n):
    ang = 2.0 * np.pi * np.outer(np.arange(n), np.arange(n)) / n
    return jnp.asarray(np.cos(ang), F32), jnp.asarray(np.sin(ang), F32)


def _hdot(a, b):
    return jnp.dot(a, b, preferred_element_type=F32, precision=lax.Precision.HIGHEST)


def _fnet_a_kernel(c_ref, s_ref, x_ref, yr_ref, yi_ref):
    x = x_ref[...]
    yr_ref[...] = _hdot(c_ref[...], x)
    yi_ref[...] = -_hdot(s_ref[...], x)


def _fnet_b_kernel(yr_ref, yi_ref, twc_ref, tws_ref, c1_ref, s1_ref, cc_ref, sc_ref, o_ref, *, scale, kb):
    c1, s1 = c1_ref[...], s1_ref[...]
    cc, sc = cc_ref[...], sc_ref[...]
    for j in range(kb):
        yr, yi = yr_ref[j], yi_ref[j]
        tc, ts = twc_ref[j], tws_ref[j]
        pr = yr * tc + yi * ts
        pi = yi * tc - yr * ts
        zr = _hdot(c1, pr) + _hdot(s1, pi)
        zi = _hdot(c1, pi) - _hdot(s1, pr)
        for g in range(FNET_GROUPS):
            cs = slice(g * FNET_GROUP_DIM, (g + 1) * FNET_GROUP_DIM)
            o_ref[:, j, cs] = (_hdot(zr[:, cs], cc) + _hdot(zi[:, cs], sc)) * scale


def _fnet_ctx_kernel(x_ref, cn_ref, sn_ref, cc_ref, sc_ref, o_ref, *, scale):
    x = x_ref[...]
    wr = _hdot(cn_ref[...], x)
    ws = _hdot(sn_ref[...], x)
    cc, sc = cc_ref[...], sc_ref[...]
    for g in range(FNET_GROUPS):
        cs = slice(g * FNET_GROUP_DIM, (g + 1) * FNET_GROUP_DIM)
        o_ref[:, cs] = (_hdot(wr[:, cs], cc) - _hdot(ws[:, cs], sc)) * scale


def fourier_mix(f_lat, f_ctx):
    n_lat, w = f_lat.shape
    n_ctx = f_ctx.shape[0]
    n2 = 128
    n1 = n_lat // n2
    c2, s2 = _dft_tables(n2)
    c1, s1 = _dft_tables(n1)
    cc, sc = _dft_tables(FNET_GROUP_DIM)
    tw = 2.0 * np.pi * np.outer(np.arange(n2), np.arange(n1)) / n_lat
    twc = jnp.asarray(np.cos(tw), F32).reshape(n2, n1, 1)
    tws = jnp.asarray(np.sin(tw), F32).reshape(n2, n1, 1)
    xr = f_lat.reshape(n2, n1 * w)
    tn = min(4096, n1 * w)
    yr, yi = pl.pallas_call(
        _fnet_a_kernel,
        grid=(n1 * w // tn,),
        in_specs=[pl.BlockSpec((n2, n2), lambda j: (0, 0)),
                  pl.BlockSpec((n2, n2), lambda j: (0, 0)),
                  pl.BlockSpec((n2, tn), lambda j: (0, j))],
        out_specs=[pl.BlockSpec((n2, tn), lambda j: (0, j)),
                   pl.BlockSpec((n2, tn), lambda j: (0, j))],
        out_shape=[jax.ShapeDtypeStruct((n2, n1 * w), F32)] * 2,
        compiler_params=_cparams(("arbitrary",)),
        name="fnet_stage_a",
    )(c2, s2, xr)
    kb = 8
    lat = pl.pallas_call(
        functools.partial(_fnet_b_kernel, scale=float(1.0 / math.sqrt(n_lat * FNET_GROUP_DIM)), kb=kb),
        grid=(n2 // kb,),
        in_specs=[pl.BlockSpec((kb, n1, w), lambda j: (j, 0, 0)),
                  pl.BlockSpec((kb, n1, w), lambda j: (j, 0, 0)),
                  pl.BlockSpec((kb, n1, 1), lambda j: (j, 0, 0)),
                  pl.BlockSpec((kb, n1, 1), lambda j: (j, 0, 0)),
                  pl.BlockSpec((n1, n1), lambda j: (0, 0)),
                  pl.BlockSpec((n1, n1), lambda j: (0, 0)),
                  pl.BlockSpec((FNET_GROUP_DIM, FNET_GROUP_DIM), lambda j: (0, 0)),
                  pl.BlockSpec((FNET_GROUP_DIM, FNET_GROUP_DIM), lambda j: (0, 0))],
        out_specs=pl.BlockSpec((n1, kb, w), lambda j: (0, j, 0)),
        out_shape=jax.ShapeDtypeStruct((n1, n2, w), F32),
        compiler_params=_cparams(("arbitrary",)),
        name="fnet_stage_b",
    )(yr.reshape(n2, n1, w), yi.reshape(n2, n1, w), twc, tws, c1, s1, cc, sc)
    cn, sn = _dft_tables(n_ctx)
    ctx = pl.pallas_call(
        functools.partial(_fnet_ctx_kernel, scale=float(1.0 / math.sqrt(n_ctx * FNET_GROUP_DIM))),
        out_shape=jax.ShapeDtypeStruct((n_ctx, w), F32),
        compiler_params=pltpu.CompilerParams(vmem_limit_bytes=VMEM_LIMIT_BYTES),
        name="fnet_ctx",
    )(f_ctx, cn, sn, cc, sc)
    return jnp.concatenate([ctx, lat.reshape(n_lat, w)], axis=0)


def _merge_kernel(o0_ref, o1_ref, o2_ref, o3_ref, g0_ref, g1_ref, g2_ref, g3_ref, wb_ref, y_ref):
    y = None
    for b, (o_ref, g_ref) in enumerate(((o0_ref, g0_ref), (o1_ref, g1_ref), (o2_ref, g2_ref), (o3_ref, g3_ref))):
        pr = jnp.dot(o_ref[...].astype(BF16), wb_ref[0, b].astype(BF16), preferred_element_type=F32)
        t = jax.nn.sigmoid(g_ref[...]) * pr
        y = t if y is None else y + t
    y_ref[...] = y.astype(y_ref.dtype)


def merge(branches, gates, w_branch_all, layer, tm=768, tn=512):
    m = gates.shape[0]
    nb = D_MODEL // tn
    o_specs = [pl.BlockSpec((tm, BRANCH_W), lambda j, i: (i, 0)) for _ in range(N_BRANCH)]
    g_specs = [pl.BlockSpec((tm, tn), functools.partial(lambda j, i, b: (i, b * nb + j), b=b)) for b in range(N_BRANCH)]
    return pl.pallas_call(
        _merge_kernel,
        grid=(nb, m // tm),
        in_specs=o_specs + g_specs + [pl.BlockSpec((1, N_BRANCH, BRANCH_W, tn), lambda j, i: (layer, 0, 0, j))],
        out_specs=pl.BlockSpec((tm, tn), lambda j, i: (i, j)),
        out_shape=jax.ShapeDtypeStruct((m, D_MODEL), BF16),
        compiler_params=_cparams(("arbitrary", "arbitrary")),
        name="merge",
    )(*branches, gates, gates, gates, gates, w_branch_all)


def _expert_kernel(idx_ref, h_hbm, wg_ref, wu_ref, wd_ref, gate_ref, o_ref, xs32_ref, xsb_ref, sem):
    e = pl.program_id(0)
    f = pl.program_id(1)
    cap = xs32_ref.shape[0]

    @pl.when(f == 0)
    def _gather():
        def issue(c, carry):
            row = idx_ref[e, c]
            pltpu.make_async_copy(h_hbm.at[pl.ds(row, 1)], xs32_ref.at[pl.ds(c, 1)], sem).start()
            return carry

        lax.fori_loop(0, cap, issue, 0, unroll=8)
        pltpu.make_async_copy(h_hbm.at[pl.ds(0, cap)], xs32_ref, sem).wait()
        xsb_ref[...] = xs32_ref[...].astype(BF16)

    xs = xsb_ref[...]
    hg = jnp.dot(xs, wg_ref[0, 0].astype(BF16), preferred_element_type=F32)
    hu = jnp.dot(xs, wu_ref[0, 0].astype(BF16), preferred_element_type=F32)
    hid = (_silu(hg) * hu).astype(BF16)
    part = jnp.dot(hid, wd_ref[0, 0].astype(BF16), preferred_element_type=F32)

    @pl.when(f == 0)
    def _first():
        o_ref[0] = part

    @pl.when(f != 0)
    def _rest():
        o_ref[0] += part

    @pl.when(f == pl.num_programs(1) - 1)
    def _done():
        o_ref[0] = o_ref[0] * gate_ref[0]


def expert_ffn(h2, idx, gate, w_gate, w_up, w_down, layer, tf=256):
    ne, cap = idx.shape
    d = h2.shape[1]
    ff = w_gate.shape[3]
    grid_spec = pltpu.PrefetchScalarGridSpec(
        num_scalar_prefetch=1,
        grid=(ne, ff // tf),
        in_specs=[pl.BlockSpec(memory_space=pl.ANY),
                  pl.BlockSpec((1, 1, d, tf), lambda e, f, ix: (layer, e, 0, f)),
                  pl.BlockSpec((1, 1, d, tf), lambda e, f, ix: (layer, e, 0, f)),
                  pl.BlockSpec((1, 1, tf, d), lambda e, f, ix: (layer, e, f, 0)),
                  pl.BlockSpec((1, cap, 1), lambda e, f, ix: (e, 0, 0))],
        out_specs=pl.BlockSpec((1, cap, d), lambda e, f, ix: (e, 0, 0)),
        scratch_shapes=[pltpu.VMEM((cap, d), F32), pltpu.VMEM((cap, d), BF16), pltpu.SemaphoreType.DMA(())],
    )
    return pl.pallas_call(
        _expert_kernel,
        grid_spec=grid_spec,
        out_shape=jax.ShapeDtypeStruct((ne, cap, d), F32),
        compiler_params=_cparams(("arbitrary", "arbitrary")),
        name="expert_ffn",
    )(idx, h2, w_gate, w_up, w_down, gate.reshape(ne, cap, 1))


def moe(h2, aff, w_gate, w_up, w_down, layer, n_ctx):
    m = h2.shape[0]
    n_lat = m - n_ctx
    a = aff[:, :N_EXPERTS]
    g_c, i_c = lax.top_k(a[:n_ctx].T, CAPACITY_FACTOR * n_ctx // N_EXPERTS)
    g_l, i_l = lax.top_k(a[n_ctx:].T, CAPACITY_FACTOR * n_lat // N_EXPERTS)
    idx = jnp.concatenate([i_c, i_l + n_ctx], axis=1).astype(jnp.int32)
    gate = jnp.concatenate([g_c, g_l], axis=1)
    ye = expert_ffn(h2, idx, gate, w_gate, w_up, w_down, layer)
    return jnp.zeros((m, D_MODEL), F32).at[idx.reshape(-1)].add(ye.reshape(-1, D_MODEL))


_P_NA = 0
_P_Z = _P_NA + 3 * NA_HEADS * HEAD_DIM
_P_XBC = _P_Z + SSD_W
_P_DT = _P_XBC + SSD_XBC_W
_P_GQ = _P_DT + 2 * SSD_HEADS
_P_FN = _P_GQ + (GQA_HEADS + 2 * GQA_KV_HEADS) * HEAD_DIM
_P_GATES = _P_FN + FNET_GROUPS * FNET_GROUP_DIM


def _rest_weight(w_in):
    d = w_in.shape[0]
    return jnp.concatenate([w_in[:, _P_GQ:_P_GATES], w_in[:, _P_DT:_P_GQ],
                            jnp.zeros((d, FR_W - FR_DT - 2 * SSD_HEADS), w_in.dtype)], axis=1)


def _layer(x, mod, n_ctx, layer, norm_mix, norm_ffn, w_in_all, na_rpb, conv_w, conv_b, a_log, dt_bias, d_skip,
           ssd_gn, q_gain, k_gain, w_branch_all, w_out_all, w_router, w_gate_all, w_up_all, w_down_all, rope):
    m = x.shape[0]
    big = m % 2112 == 0
    tm = 768 if m % 768 == 0 else 256
    tmg = 2112 if big else 256
    h = norm_mod(x, norm_mix, mod, n_ctx, 0, tm=tm)
    ab = matmul_cols(h, w_in_all, layer, _P_NA, _P_Z - _P_NA, BF16, tm, 768, "proj_na")
    fzx = matmul_cols(h, w_in_all, layer, _P_Z, _P_DT - _P_Z, F32, tm, 512, "proj_zx", out_perm=(2, 0, 1))
    fr = matmul(h, _rest_weight(w_in_all[layer]), F32, tm, FR_W // 2, "proj_rest")
    gates = matmul_cols_unaligned(h, w_in_all, layer, _P_GATES, N_BRANCH * D_MODEL, F32, tmg, 512, "proj_gates")

    rows = (m - n_ctx) // GRID_W
    bias, tid = na_bias_table(na_rpb, rows)
    o_na = na_attention(ab, bias, tid, n_ctx)

    u, dt = ssd_prep(fzx, fr, conv_w, conv_b, dt_bias, n_ctx)
    y2 = ssd_scan(u, dt, a_log, n_ctx)
    o_ssd = ssd_out(y2, u, fzx, d_skip, ssd_gn, tm=tm)

    qt, kn, vt = qk_prep(fr, rope[0], rope[1], q_gain, k_gain, tm=tm)
    o_gqa = gqa_attention(qt, kn, vt, n_ctx)

    f_in = fr[:, FR_FN:FR_FN + FNET_GROUPS * FNET_GROUP_DIM]
    o_fn = fourier_mix(f_in[n_ctx:], f_in[:n_ctx])

    y = merge((o_na, o_ssd, o_gqa, o_fn), gates, w_branch_all, layer, tm=tm)
    x = matmul_residual(y, w_out_all, layer, x, mod, 2, n_ctx, tm, 1024, "out_proj")

    wr = jnp.concatenate([w_router, jnp.zeros((D_MODEL, LANES - N_EXPERTS), w_router.dtype)], axis=1)
    h2, aff = norm_router(x, norm_ffn, mod, wr, n_ctx, 3, tm=tm)
    mo = moe(h2, aff, w_gate_all, w_up_all, w_down_all, layer, n_ctx)
    row = jnp.arange(m)[:, None]
    g2 = jnp.where(row < n_ctx, mod[1, 5 * D_MODEL:][None], mod[0, 5 * D_MODEL:][None])
    return x + g2 * mo


def kernel(x, c, ctx, c_ctx, w_ada, b_ada, norm_mix, norm_ffn, w_in, na_rpb, ssd_conv_w, ssd_conv_b, ssd_a_log,
           ssd_dt_bias, ssd_d, ssd_norm, gqa_q_norm, gqa_k_norm, w_branch, w_out, w_router, moe_w_gate, moe_w_up,
           moe_w_down, final_norm_w):
    n_lat = x.shape[1]
    n_ctx = ctx.shape[1]
    depth = w_ada.shape[0]
    cc = jnp.concatenate([c[0:1], c_ctx[None], jnp.zeros((6, D_MODEL), F32)], axis=0)
    mods = ada_all(cc, w_ada, b_ada)
    rope = rope_tables(n_ctx, n_lat)
    xs = jnp.concatenate([ctx[0], x[0]], axis=0)
    for l in range(depth):
        xs = _layer(xs, mods[l], n_ctx, l, norm_mix[l], norm_ffn[l], w_in, na_rpb[l], ssd_conv_w[l],
                    ssd_conv_b[l], ssd_a_log[l], ssd_dt_bias[l], ssd_d[l], ssd_norm[l], gqa_q_norm[l],
                    gqa_k_norm[l], w_branch, w_out, w_router[l], moe_w_gate, moe_w_up, moe_w_down, rope)
    out = final_norm(xs, final_norm_w, n_ctx, n_lat)
    return out[None]
```

```python
import functools
import math

import numpy as np
import jax
import jax.numpy as jnp
from jax import lax
from jax.experimental import pallas as pl
from jax.experimental.pallas import tpu as pltpu

F32 = jnp.float32
BF16 = jnp.bfloat16

D_MODEL = 2048
GRID_W = 64
HEAD_DIM = 128
RMS_EPS = 1e-6
N_BRANCH = 4
BRANCH_W = 512
NA_HEADS = 4
NA_WIN_R = 8
NA_WIN_C = 16
SSD_HEADS = 8
SSD_HEAD_DIM = 64
SSD_GROUPS = 2
SSD_STATE = 128
SSD_CONV = 5
SSD_CHUNK = 128
SSD_W = SSD_HEADS * SSD_HEAD_DIM
SSD_XBC_W = SSD_W + 2 * SSD_GROUPS * SSD_STATE
GQA_HEADS = 4
GQA_KV_HEADS = 2
ROPE_THETA = 10000.0
FNET_GROUPS = 4
FNET_GROUP_DIM = 128
N_EXPERTS = 16
EXPERT_FF = D_MODEL // 2
CAPACITY_FACTOR = 2
LOG2E = 1.4426950408889634

VMEM_LIMIT_BYTES = 56 * 1024 * 1024
LANES = 128

ZX_XBC, ZX_Z, ZX_W = 0, 1024, 1536
FR_Q, FR_K, FR_V, FR_FN, FR_W = 0, 512, 768, 1024, 1536


def _cparams(sem):
    return pltpu.CompilerParams(dimension_semantics=sem, vmem_limit_bytes=VMEM_LIMIT_BYTES)


def _silu(x):
    return x * jax.nn.sigmoid(x)


def _ada_kernel(ct_ref, w_ref, b_ref, o_ref):
    a = _silu(ct_ref[...])
    w = w_ref[0]
    rows = [jnp.sum(w * a[:, r:r + 1], axis=0, keepdims=True) + b_ref[0] for r in range(2)]
    o_ref[0] = jnp.concatenate(rows + [jnp.zeros((6, w.shape[1]), F32)], axis=0)


def ada_all(cc, w_ada, b_ada):
    depth, d, n = w_ada.shape
    tn = 1024
    return pl.pallas_call(
        _ada_kernel,
        grid=(depth, n // tn),
        in_specs=[pl.BlockSpec((d, 8), lambda l, j: (0, 0)),
                  pl.BlockSpec((1, d, tn), lambda l, j: (l, 0, j)),
                  pl.BlockSpec((1, 1, tn), lambda l, j: (l, 0, j))],
        out_specs=pl.BlockSpec((1, 8, tn), lambda l, j: (l, 0, j)),
        out_shape=jax.ShapeDtypeStruct((depth, 8, n), F32),
        compiler_params=_cparams(("arbitrary", "arbitrary")),
        name="ada_mod",
    )(cc.T, w_ada, b_ada.reshape(depth, 1, n))


def _row_select(mod_ref, chunk, row_is_ctx):
    lo = chunk * D_MODEL
    return jnp.where(row_is_ctx, mod_ref[1:2, lo:lo + D_MODEL], mod_ref[0:1, lo:lo + D_MODEL])


def _norm_body(x_ref, nw_ref):
    x = x_ref[...]
    ms = jnp.mean(x * x, axis=-1, keepdims=True)
    return x * lax.rsqrt(ms + RMS_EPS) * nw_ref[...]


def _norm_mod_kernel(x_ref, nw_ref, mod_ref, o_ref, *, n_ctx, tm, sh_chunk):
    y = _norm_body(x_ref, nw_ref)
    row = pl.program_id(0) * tm + lax.broadcasted_iota(jnp.int32, (tm, 1), 0)
    is_ctx = row < n_ctx
    sh = _row_select(mod_ref, sh_chunk, is_ctx)
    sc = _row_select(mod_ref, sh_chunk + 1, is_ctx)
    o_ref[...] = (y * (1.0 + sc) + sh).astype(o_ref.dtype)


def norm_mod(x, nw, mod, n_ctx, sh_chunk, tm=768):
    m, d = x.shape
    return pl.pallas_call(
        functools.partial(_norm_mod_kernel, n_ctx=n_ctx, tm=tm, sh_chunk=sh_chunk),
        grid=(m // tm,),
        in_specs=[pl.BlockSpec((tm, d), lambda i: (i, 0)),
                  pl.BlockSpec((1, d), lambda i: (0, 0)),
                  pl.BlockSpec(mod.shape, lambda i: (0, 0))],
        out_specs=pl.BlockSpec((tm, d), lambda i: (i, 0)),
        out_shape=jax.ShapeDtypeStruct((m, d), BF16),
        compiler_params=_cparams(("arbitrary",)),
        name="norm_mod",
    )(x, nw.reshape(1, d), mod)


def _norm_router_kernel(x_ref, nw_ref, mod_ref, wr_ref, h_ref, aff_ref, *, n_ctx, tm, sh_chunk):
    y = _norm_body(x_ref, nw_ref)
    row = pl.program_id(0) * tm + lax.broadcasted_iota(jnp.int32, (tm, 1), 0)
    is_ctx = row < n_ctx
    sh = _row_select(mod_ref, sh_chunk, is_ctx)
    sc = _row_select(mod_ref, sh_chunk + 1, is_ctx)
    h = y * (1.0 + sc) + sh
    h_ref[...] = h
    logits = jnp.dot(h.astype(BF16), wr_ref[...].astype(BF16), preferred_element_type=F32)
    lane = lax.broadcasted_iota(jnp.int32, logits.shape, 1)
    logits = jnp.where(lane < N_EXPERTS, logits, -jnp.inf)
    mx = jnp.max(logits, axis=-1, keepdims=True)
    e = jnp.exp(logits - mx)
    aff_ref[...] = e / jnp.sum(e, axis=-1, keepdims=True)


def norm_router(x, nw, mod, w_router_pad, n_ctx, sh_chunk, tm=768):
    m, d = x.shape
    return pl.pallas_call(
        functools.partial(_norm_router_kernel, n_ctx=n_ctx, tm=tm, sh_chunk=sh_chunk),
        grid=(m // tm,),
        in_specs=[pl.BlockSpec((tm, d), lambda i: (i, 0)),
                  pl.BlockSpec((1, d), lambda i: (0, 0)),
                  pl.BlockSpec(mod.shape, lambda i: (0, 0)),
                  pl.BlockSpec((d, LANES), lambda i: (0, 0))],
        out_specs=[pl.BlockSpec((tm, d), lambda i: (i, 0)),
                   pl.BlockSpec((tm, LANES), lambda i: (i, 0))],
        out_shape=[jax.ShapeDtypeStruct((m, d), F32),
                   jax.ShapeDtypeStruct((m, LANES), F32)],
        compiler_params=_cparams(("arbitrary",)),
        name="norm_router",
    )(x, nw.reshape(1, d), mod, w_router_pad)


def _final_norm_kernel(x_ref, nw_ref, o_ref):
    o_ref[...] = _norm_body(x_ref, nw_ref)


def final_norm(x, nw, row0, n_rows, tm=256):
    d = x.shape[1]
    off = row0 // tm
    return pl.pallas_call(
        _final_norm_kernel,
        grid=(n_rows // tm,),
        in_specs=[pl.BlockSpec((tm, d), lambda i: (i + off, 0)),
                  pl.BlockSpec((1, d), lambda i: (0, 0))],
        out_specs=pl.BlockSpec((tm, d), lambda i: (i, 0)),
        out_shape=jax.ShapeDtypeStruct((n_rows, d), F32),
        compiler_params=_cparams(("arbitrary",)),
        name="final_norm",
    )(x, nw.reshape(1, d))


def _mm_kernel(a_ref, w_ref, o_ref):
    o_ref[...] = jnp.dot(a_ref[...].astype(BF16), w_ref[...].astype(BF16),
                         preferred_element_type=F32).astype(o_ref.dtype)


def matmul(a, w, out_dtype, tm, tn, name):
    m, k = a.shape
    n = w.shape[1]
    return pl.pallas_call(
        _mm_kernel,
        grid=(n // tn, m // tm),
        in_specs=[pl.BlockSpec((tm, k), lambda j, i: (i, 0)),
                  pl.BlockSpec((k, tn), lambda j, i: (0, j))],
        out_specs=pl.BlockSpec((tm, tn), lambda j, i: (i, j)),
        out_shape=jax.ShapeDtypeStruct((m, n), out_dtype),
        compiler_params=_cparams(("arbitrary", "arbitrary")),
        name=name,
    )(a, w)


def _mm_layer_kernel(a_ref, w_ref, o_ref):
    o_ref[...] = jnp.dot(a_ref[...], w_ref[0].astype(BF16), preferred_element_type=F32).astype(o_ref.dtype)


def matmul_cols(a, w_all, layer, col0, ncols, out_dtype, tm, tn, name, out_perm=None):
    m, k = a.shape
    nb = ncols // tn
    c0 = col0 // tn
    perm = tuple(range(nb)) if out_perm is None else tuple(out_perm)

    def out_map(j, i):
        pj = j
        for src, dst in enumerate(perm):
            pj = jnp.where(j == src, dst, pj)
        return (i, pj)

    return pl.pallas_call(
        _mm_layer_kernel,
        grid=(nb, m // tm),
        in_specs=[pl.BlockSpec((tm, k), lambda j, i: (i, 0)),
                  pl.BlockSpec((1, k, tn), lambda j, i: (layer, 0, c0 + j))],
        out_specs=pl.BlockSpec((tm, tn), out_map),
        out_shape=jax.ShapeDtypeStruct((m, ncols), out_dtype),
        compiler_params=_cparams(("arbitrary", "arbitrary")),
        name=name,
    )(a, w_all)


def _mm_shift_kernel(a_ref, wa_ref, wb_ref, o_ref, ws_ref, *, shift, tn):
    @pl.when(pl.program_id(1) == 0)
    def _realign():
        wcat = jnp.concatenate([wa_ref[0], wb_ref[0]], axis=1)
        ws_ref[...] = wcat[:, shift:shift + tn].astype(BF16)

    o_ref[...] = jnp.dot(a_ref[...], ws_ref[...], preferred_element_type=F32).astype(o_ref.dtype)


def matmul_cols_unaligned(a, w_all, layer, col0, ncols, out_dtype, tm, tn, name):
    m, k = a.shape
    base = (col0 // LANES) * LANES
    shift = col0 - base
    assert base % tn == 0 and ncols % tn == 0 and 0 < shift < LANES
    c0 = base // tn
    r = tn // LANES
    return pl.pallas_call(
        functools.partial(_mm_shift_kernel, shift=shift, tn=tn),
        grid=(ncols // tn, m // tm),
        in_specs=[pl.BlockSpec((tm, k), lambda j, i: (i, 0)),
                  pl.BlockSpec((1, k, tn), lambda j, i: (layer, 0, c0 + j)),
                  pl.BlockSpec((1, k, LANES), lambda j, i: (layer, 0, (c0 + j + 1) * r))],
        out_specs=pl.BlockSpec((tm, tn), lambda j, i: (i, j)),
        out_shape=jax.ShapeDtypeStruct((m, ncols), out_dtype),
        scratch_shapes=[pltpu.VMEM((k, tn), BF16)],
        compiler_params=_cparams(("arbitrary", "arbitrary")),
        name=name,
    )(a, w_all, w_all)


def _mm_residual_kernel(a_ref, w_ref, x_ref, g_ref, o_ref, *, n_ctx, tm):
    acc = jnp.dot(a_ref[...].astype(BF16), w_ref[0].astype(BF16), preferred_element_type=F32)
    row = pl.program_id(1) * tm + lax.broadcasted_iota(jnp.int32, (tm, 1), 0)
    g = jnp.where(row < n_ctx, g_ref[1:2, :], g_ref[0:1, :])
    o_ref[...] = x_ref[...] + g * acc


def matmul_residual(a, w_all, layer, x, mod, gate_chunk, n_ctx, tm, tn, name):
    m, k = a.shape
    n = w_all.shape[2]
    goff = gate_chunk * D_MODEL // tn
    return pl.pallas_call(
        functools.partial(_mm_residual_kernel, n_ctx=n_ctx, tm=tm),
        grid=(n // tn, m // tm),
        in_specs=[pl.BlockSpec((tm, k), lambda j, i: (i, 0)),
                  pl.BlockSpec((1, k, tn), lambda j, i: (layer, 0, j)),
                  pl.BlockSpec((tm, tn), lambda j, i: (i, j)),
                  pl.BlockSpec((8, tn), lambda j, i: (0, goff + j))],
        out_specs=pl.BlockSpec((tm, tn), lambda j, i: (i, j)),
        out_shape=jax.ShapeDtypeStruct((m, n), F32),
        compiler_params=_cparams(("arbitrary", "arbitrary")),
        name=name,
    )(a, w_all, x, mod)


def _softmax_pv(parts):
    m = None
    for s, _ in parts:
        mi = jnp.max(s, axis=-1, keepdims=True)
        m = mi if m is None else jnp.maximum(m, mi)
    ps, l = [], None
    for s, _ in parts:
        p = jnp.exp(s - m)
        ps.append(p)
        li = jnp.sum(p, axis=-1, keepdims=True)
        l = li if l is None else l + li
    inv = 1.0 / l
    o = None
    for p, (_, v) in zip(ps, parts):
        oi = jnp.dot((p * inv).astype(BF16), v, preferred_element_type=F32)
        o = oi if o is None else o + oi
    return o


def _qk(q, k):
    return lax.dot_general(q, k, (((1,), (1,)), ((), ())), preferred_element_type=F32)


NA_QROWS = 4
NA_KROWS = NA_WIN_R + NA_QROWS - 1


def _na_kernel(tid_ref, q_ref, k_ref, v_ref, b_ref, o_ref, *, n_ctx, rows):
    del tid_ref
    step = pl.program_id(0)
    tq = NA_QROWS * GRID_W
    n_ctx_steps = n_ctx // tq
    scale = HEAD_DIM ** -0.5
    win = NA_KROWS * GRID_W

    @pl.when(step < n_ctx_steps)
    def _ctx():
        for h in range(NA_HEADS):
            cs = slice(h * HEAD_DIM, (h + 1) * HEAD_DIM)
            q = q_ref[:, cs]
            s_c = _qk(q, k_ref[0:n_ctx, cs]) * scale
            o_ref[:, cs] = _softmax_pv([(s_c, v_ref[0:n_ctx, cs])]).astype(o_ref.dtype)

    @pl.when(step >= n_ctx_steps)
    def _lat():
        p = step - n_ctx_steps
        row0 = jnp.clip(NA_QROWS * p - NA_WIN_R // 2, 0, rows - NA_KROWS)
        base = pl.multiple_of(n_ctx + row0 * GRID_W, GRID_W)
        for h in range(NA_HEADS):
            cs = slice(h * HEAD_DIM, (h + 1) * HEAD_DIM)
            q = q_ref[:, cs]
            s_w = _qk(q, k_ref[pl.ds(base, win), cs]) * scale + b_ref[0, h]
            s_c = _qk(q, k_ref[0:n_ctx, cs]) * scale
            o = _softmax_pv([(s_w, v_ref[pl.ds(base, win), cs]), (s_c, v_ref[0:n_ctx, cs])])
            o_ref[:, cs] = o.astype(o_ref.dtype)


def _na_patterns(rows):
    a = np.arange(NA_QROWS)[:, None]
    i = np.arange(NA_KROWS)[None, :]
    pats, keys, tid = [], [], []
    for p in range(rows // NA_QROWS):
        row0 = int(np.clip(NA_QROWS * p - NA_WIN_R // 2, 0, rows - NA_KROWS))
        r = NA_QROWS * p + a
        rs = np.clip(r - NA_WIN_R // 2, 0, rows - NA_WIN_R)
        krow = row0 + i
        valid = (krow >= rs) & (krow < rs + NA_WIN_R)
        ridx = np.clip(krow - r + NA_WIN_R - 1, 0, 2 * NA_WIN_R - 2)
        key = (valid.tobytes(), ridx.tobytes())
        if key not in keys:
            keys.append(key)
            pats.append((valid, ridx))
        tid.append(keys.index(key))
    return np.stack([v for v, _ in pats]), np.stack([x for _, x in pats]), np.asarray(tid, np.int32)


def na_bias_table(rpb, rows):
    valid_r, ridx, tid = _na_patterns(rows)
    kc = np.arange(GRID_W)[None, :]
    qc = np.arange(GRID_W)[:, None]
    col_start = np.clip(qc - NA_WIN_C // 2, 0, GRID_W - NA_WIN_C)
    valid_c = (kc >= col_start) & (kc < col_start + NA_WIN_C)
    cidx = np.clip(kc - qc + NA_WIN_C - 1, 0, 2 * NA_WIN_C - 2)
    nh, nbr, nbc = rpb.shape
    onehot = (np.arange(nbc)[:, None, None] == cidx[None]) & valid_c[None]
    toe = jnp.einsum('hbj,jqk->hbqk', rpb.astype(F32), jnp.asarray(onehot, F32), precision=lax.Precision.HIGHEST)
    toe = toe + jnp.asarray(np.where(valid_c, 0.0, -np.inf), F32)
    neg = jnp.full((nh, GRID_W, GRID_W), -jnp.inf, F32)
    npat = valid_r.shape[0]
    blocks = [toe[:, int(ridx[p, a, i])] if valid_r[p, a, i] else neg
              for p in range(npat) for a in range(NA_QROWS) for i in range(NA_KROWS)]
    t = jnp.stack(blocks, axis=0).reshape(npat, NA_QROWS, NA_KROWS, nh, GRID_W, GRID_W)
    t = jnp.transpose(t, (0, 3, 1, 4, 2, 5))
    return t.reshape(npat, nh, NA_QROWS * GRID_W, NA_KROWS * GRID_W), jnp.asarray(tid)


def na_attention(ab, bias, tid, n_ctx):
    m = ab.shape[0]
    rows = (m - n_ctx) // GRID_W
    tq = NA_QROWS * GRID_W
    n_ctx_steps = n_ctx // tq
    w = NA_HEADS * HEAD_DIM
    grid_spec = pltpu.PrefetchScalarGridSpec(
        num_scalar_prefetch=1,
        grid=(m // tq,),
        in_specs=[pl.BlockSpec((tq, w), lambda s, t: (s, 0)),
                  pl.BlockSpec((m, w), lambda s, t: (0, 1)),
                  pl.BlockSpec((m, w), lambda s, t: (0, 2)),
                  pl.BlockSpec((1, NA_HEADS, tq, NA_KROWS * GRID_W),
                               lambda s, t: (t[jnp.maximum(s - n_ctx_steps, 0)], 0, 0, 0))],
        out_specs=pl.BlockSpec((tq, w), lambda s, t: (s, 0)),
    )
    return pl.pallas_call(
        functools.partial(_na_kernel, n_ctx=n_ctx, rows=rows),
        grid_spec=grid_spec,
        out_shape=jax.ShapeDtypeStruct((m, w), BF16),
        compiler_params=_cparams(("arbitrary",)),
        name="na_attn",
    )(tid, ab, ab, ab, bias)


def rope_tables(n_ctx, n_lat):
    t = jnp.arange(n_lat)
    pos = jnp.stack([t // GRID_W, t % GRID_W], axis=-1).astype(F32)
    n_freq = HEAD_DIM // 4
    inv = ROPE_THETA ** (-jnp.arange(n_freq, dtype=F32) / n_freq)
    ang = pos[:, :, None] * inv
    c, s = jnp.cos(ang), jnp.sin(ang)
    cos = jnp.concatenate([c[:, 0], c[:, 0], c[:, 1], c[:, 1]], axis=-1)
    sin = jnp.concatenate([-s[:, 0], s[:, 0], -s[:, 1], s[:, 1]], axis=-1)
    cos = jnp.concatenate([jnp.ones((n_ctx, HEAD_DIM), F32), cos], axis=0)
    sin = jnp.concatenate([jnp.zeros((n_ctx, HEAD_DIM), F32), sin], axis=0)
    return cos, sin


def _qk_prep_kernel(x_ref, cos_ref, sin_ref, qg_ref, kg_ref, qt_ref, ko_ref, vt_ref):
    cos = cos_ref[...]
    sin = sin_ref[...]
    lane = lax.broadcasted_iota(jnp.int32, (1, HEAD_DIM), 1)
    first_half = (lane % (HEAD_DIM // 2)) < (HEAD_DIM // 4)
    for h in range(GQA_HEADS + GQA_KV_HEADS):
        x = x_ref[:, h * HEAD_DIM:(h + 1) * HEAD_DIM]
        gain = qg_ref[...] if h < GQA_HEADS else kg_ref[...]
        y = x * lax.rsqrt(jnp.mean(x * x, axis=-1, keepdims=True) + RMS_EPS) * gain
        sw = jnp.where(first_half, pltpu.roll(y, HEAD_DIM - HEAD_DIM // 4, 1), pltpu.roll(y, HEAD_DIM // 4, 1))
        out = y * cos + sw * sin
        if h < GQA_HEADS:
            qt_ref[h * HEAD_DIM:(h + 1) * HEAD_DIM, :] = out.T.astype(BF16)
        else:
            hk = h - GQA_HEADS
            ko_ref[:, hk * HEAD_DIM:(hk + 1) * HEAD_DIM] = out.astype(BF16)
    v0 = (GQA_HEADS + GQA_KV_HEADS) * HEAD_DIM
    for hv in range(GQA_KV_HEADS):
        v = x_ref[:, v0 + hv * HEAD_DIM:v0 + (hv + 1) * HEAD_DIM]
        vt_ref[hv * HEAD_DIM:(hv + 1) * HEAD_DIM, :] = v.T.astype(BF16)


def qk_prep(fr, cos, sin, q_gain, k_gain, tm=768):
    m = fr.shape[0]
    wq, wk = GQA_HEADS * HEAD_DIM, GQA_KV_HEADS * HEAD_DIM
    return pl.pallas_call(
        _qk_prep_kernel,
        grid=(m // tm,),
        in_specs=[pl.BlockSpec((tm, wq + 2 * wk), lambda i: (i, 0)),
                  pl.BlockSpec((tm, HEAD_DIM), lambda i: (i, 0)),
                  pl.BlockSpec((tm, HEAD_DIM), lambda i: (i, 0)),
                  pl.BlockSpec((1, HEAD_DIM), lambda i: (0, 0)),
                  pl.BlockSpec((1, HEAD_DIM), lambda i: (0, 0))],
        out_specs=[pl.BlockSpec((wq, tm), lambda i: (0, i)),
                   pl.BlockSpec((tm, wk), lambda i: (i, 0)),
                   pl.BlockSpec((wk, tm), lambda i: (0, i))],
        out_shape=[jax.ShapeDtypeStruct((wq, m), BF16),
                   jax.ShapeDtypeStruct((m, wk), BF16),
                   jax.ShapeDtypeStruct((wk, m), BF16)],
        compiler_params=_cparams(("arbitrary",)),
        name="qk_prep",
    )(fr, cos, sin, q_gain.reshape(1, HEAD_DIM), k_gain.reshape(1, HEAD_DIM))


def _gqa_kernel(qt_ref, k_ref, vt_ref, o_ref, *, n_ctx, n_all, tq, tk):
    qi = pl.program_id(1)
    c1 = (HEAD_DIM ** -0.5) * LOG2E
    n_ctx_tiles = n_ctx // tq
    rep = GQA_HEADS // GQA_KV_HEADS

    def attend(n_keys, tkk):
        nchunks = n_keys // tkk

        def scores(c):
            kc = k_ref[c * tkk:(c + 1) * tkk, :]
            return [jnp.dot(kc, qt_ref[r * HEAD_DIM:(r + 1) * HEAD_DIM, :], preferred_element_type=F32)
                    for r in range(rep)]

        state = [None] * rep
        nxt = scores(0)
        for c in range(nchunks):
            cur = nxt
            if c + 1 < nchunks:
                nxt = scores(c + 1)
            vtc = vt_ref[:, c * tkk:(c + 1) * tkk]
            for r in range(rep):
                st = cur[r]
                mc = jnp.max(st, axis=0, keepdims=True)
                if state[r] is None:
                    m_new = mc
                    p = jnp.exp2(st * c1 - m_new * c1)
                    l = jnp.sum(p, axis=0, keepdims=True)
                    acc = jnp.dot(vtc, p.astype(BF16), preferred_element_type=F32)
                else:
                    m, l, acc = state[r]
                    m_new = jnp.maximum(m, mc)
                    p = jnp.exp2(st * c1 - m_new * c1)
                    alpha = jnp.exp2((m - m_new) * c1)
                    l = alpha * l + jnp.sum(p, axis=0, keepdims=True)
                    acc = alpha * acc + jnp.dot(vtc, p.astype(BF16), preferred_element_type=F32)
                state[r] = (m_new, l, acc)
        for r in range(rep):
            _, l, acc = state[r]
            o_ref[:, r * HEAD_DIM:(r + 1) * HEAD_DIM] = (acc / l).T.astype(o_ref.dtype)

    @pl.when(qi < n_ctx_tiles)
    def _ctx():
        attend(n_ctx, n_ctx)

    @pl.when(qi >= n_ctx_tiles)
    def _lat():
        attend(n_all, tk)


def gqa_attention(qt, kn, vt, n_ctx, tq=256, tk=768):
    m = kn.shape[0]
    rep = GQA_HEADS // GQA_KV_HEADS
    if m % tk:
        tk = 256
    return pl.pallas_call(
        functools.partial(_gqa_kernel, n_ctx=n_ctx, n_all=m, tq=tq, tk=tk),
        grid=(GQA_KV_HEADS, m // tq),
        in_specs=[pl.BlockSpec((rep * HEAD_DIM, tq), lambda g, i: (g, i)),
                  pl.BlockSpec((m, HEAD_DIM), lambda g, i: (0, g)),
                  pl.BlockSpec((HEAD_DIM, m), lambda g, i: (g, 0))],
        out_specs=pl.BlockSpec((tq, rep * HEAD_DIM), lambda g, i: (i, g)),
        out_shape=jax.ShapeDtypeStruct((m, GQA_HEADS * HEAD_DIM), BF16),
        compiler_params=_cparams(("arbitrary", "arbitrary")),
        name="gqa_attn",
    )(qt, kn, vt)


def _ssd_prep_kernel(prev_ref, x_ref, next_ref, dtr_ref, cw_ref, cb_ref, dtb_ref, u_ref, dt_ref, *, n_ctx, n_all, tm):
    i = pl.program_id(0)
    lo = i * tm
    hi = lo + tm
    top_ok = jnp.logical_and(lo != 0, lo != n_ctx)
    bot_ok = jnp.logical_and(hi != n_ctx, hi != n_all)
    prev = jnp.where(top_ok, prev_ref[...], 0.0)
    nxt = jnp.where(bot_ok, next_ref[...], 0.0)
    ext = jnp.concatenate([prev, x_ref[...], nxt], axis=0)
    half = SSD_CONV // 2
    acc = None
    for j in range(SSD_CONV):
        sl = ext[8 - half + j:8 - half + j + tm, :]
        term = sl * cw_ref[j:j + 1, :]
        acc = term if acc is None else acc + term
    u_ref[...] = _silu(acc + cb_ref[...])
    dt_ref[...] = jax.nn.softplus(dtr_ref[...] + dtb_ref[...])


def ssd_prep(fzx, fr, conv_w, conv_b, dt_bias, n_ctx, tm=256):
    m = fzx.shape[0]
    nb8 = tm // 8
    last8 = m // 8 - 1
    cw = jnp.concatenate([conv_w, jnp.zeros((8 - SSD_CONV, SSD_XBC_W), F32)], axis=0)
    dtb = jnp.concatenate([dt_bias.reshape(-1), jnp.zeros((LANES - 2 * SSD_HEADS,), F32)]).reshape(1, LANES)
    return pl.pallas_call(
        functools.partial(_ssd_prep_kernel, n_ctx=n_ctx, n_all=m, tm=tm),
        grid=(m // tm,),
        in_specs=[pl.BlockSpec((8, SSD_XBC_W), lambda i: (jnp.maximum(i * nb8 - 1, 0), 0)),
                  pl.BlockSpec((tm, SSD_XBC_W), lambda i: (i, 0)),
                  pl.BlockSpec((8, SSD_XBC_W), lambda i: (jnp.minimum((i + 1) * nb8, last8), 0)),
                  pl.BlockSpec((tm, LANES), lambda i: (i, 0)),
                  pl.BlockSpec((8, SSD_XBC_W), lambda i: (0, 0)),
                  pl.BlockSpec((1, SSD_XBC_W), lambda i: (0, 0)),
                  pl.BlockSpec((1, LANES), lambda i: (0, 0))],
        out_specs=[pl.BlockSpec((tm, SSD_XBC_W), lambda i: (i, 0)),
                   pl.BlockSpec((tm, LANES), lambda i: (i, 0))],
        out_shape=[jax.ShapeDtypeStruct((m, SSD_XBC_W), F32),
                   jax.ShapeDtypeStruct((m, LANES), F32)],
        compiler_params=_cparams(("arbitrary",)),
        name="ssd_prep",
    )(fzx, fzx, fzx, fr, cw, conv_b.reshape(1, SSD_XBC_W), dtb)


def _ssd_chunk(u_ref, dt_ref, alog_ref, y_ref, ht_ref, d):
    ln = SSD_CHUNK
    p = SSD_HEAD_DIM
    ns = SSD_STATE
    epg = SSD_HEADS // SSD_GROUPS
    dt = dt_ref[...]
    if d == 1:
        dt = pltpu.roll(dt, LANES - SSD_HEADS, 1)
    a = dt * (-jnp.exp(alog_ref[d:d + 1, :]))
    li = lax.broadcasted_iota(jnp.int32, (ln, ln), 0)
    si = lax.broadcasted_iota(jnp.int32, (ln, ln), 1)
    mask = (li >= si) if d == 0 else (li <= si)
    a_cum = jnp.dot(mask.astype(F32), a, preferred_element_type=F32, precision=lax.Precision.HIGHEST)
    a_cum_t = a_cum.T
    a_tot = a_cum[ln - 1:ln, :] if d == 0 else a_cum[0:1, :]
    w_all = jnp.exp(a_tot - a_cum)
    ea_all = jnp.exp(a_cum)
    eat = jnp.exp(a_tot)
    for g in range(SSD_GROUPS):
        bg = u_ref[:, SSD_W + g * ns:SSD_W + (g + 1) * ns]
        cg = u_ref[:, SSD_W + SSD_GROUPS * ns + g * ns:SSD_W + SSD_GROUPS * ns + (g + 1) * ns]
        cgb = cg.astype(BF16)
        cb = _qk(cgb, bg.astype(BF16))
        bgt = bg.T.astype(BF16)
        for e in range(epg):
            h = g * epg + e
            ac = a_cum[:, h:h + 1]
            act = a_cum_t[h:h + 1, :]
            decay = jnp.exp(jnp.where(mask, ac - act, -jnp.inf))
            mm = (cb * decay).astype(BF16)
            xdt = u_ref[:, h * p:(h + 1) * p] * dt[:, h:h + 1]
            ht = ht_ref[d, h]
            y = (jnp.dot(mm, xdt.astype(BF16), preferred_element_type=F32)
                 + ea_all[:, h:h + 1] * jnp.dot(cgb, ht.astype(BF16), preferred_element_type=F32))
            st = jnp.dot(bgt, (xdt * w_all[:, h:h + 1]).astype(BF16), preferred_element_type=F32)
            ht_ref[d, h] = eat[:, h:h + 1] * ht + st
            y_ref[:, h * p:(h + 1) * p] = y


def _ssd_scan_kernel(uf_ref, dtf_ref, ub_ref, dtb_ref, alog_ref, yf_ref, yb_ref, ht_ref):
    @pl.when(pl.program_id(0) == 0)
    def _init():
        ht_ref[...] = jnp.zeros_like(ht_ref)

    _ssd_chunk(uf_ref, dtf_ref, alog_ref, yf_ref, ht_ref, 0)
    _ssd_chunk(ub_ref, dtb_ref, alog_ref, yb_ref, ht_ref, 1)


def ssd_scan(u, dt, a_log, n_ctx):
    m = u.shape[0]
    nc = m // SSD_CHUNK
    ncc = n_ctx // SSD_CHUNK

    def bwd_chunk(s):
        return jnp.where(s < ncc, ncc - 1 - s, ncc + nc - 1 - s)

    return pl.pallas_call(
        _ssd_scan_kernel,
        grid=(nc,),
        in_specs=[pl.BlockSpec((SSD_CHUNK, SSD_XBC_W), lambda s: (s, 0)),
                  pl.BlockSpec((SSD_CHUNK, LANES), lambda s: (s, 0)),
                  pl.BlockSpec((SSD_CHUNK, SSD_XBC_W), lambda s: (bwd_chunk(s), 0)),
                  pl.BlockSpec((SSD_CHUNK, LANES), lambda s: (bwd_chunk(s), 0)),
                  pl.BlockSpec((2, LANES), lambda s: (0, 0))],
        out_specs=[pl.BlockSpec((SSD_CHUNK, SSD_W), lambda s: (s, 0)),
                   pl.BlockSpec((SSD_CHUNK, SSD_W), lambda s: (bwd_chunk(s), 0))],
        out_shape=[jax.ShapeDtypeStruct((m, SSD_W), F32), jax.ShapeDtypeStruct((m, SSD_W), F32)],
        scratch_shapes=[pltpu.VMEM((2, SSD_HEADS, SSD_STATE, SSD_HEAD_DIM), F32)],
        compiler_params=_cparams(("arbitrary",)),
        name="ssd_scan",
    )(u, dt, u, dt, jnp.concatenate([a_log, jnp.zeros((2, LANES - SSD_HEADS), F32)], axis=1))


def _ssd_out_kernel(yf_ref, yb_ref, xs_ref, z_ref, dsk_ref, nw_ref, o_ref):
    y = yf_ref[...] + yb_ref[...] + dsk_ref[...] * xs_ref[...]
    gated = y * _silu(z_ref[...])
    gw = SSD_W // SSD_GROUPS
    for g in range(SSD_GROUPS):
        blk = gated[:, g * gw:(g + 1) * gw]
        nrm = blk * lax.rsqrt(jnp.mean(blk * blk, axis=-1, keepdims=True) + RMS_EPS)
        o_ref[:, g * gw:(g + 1) * gw] = (nrm * nw_ref[:, g * gw:(g + 1) * gw]).astype(o_ref.dtype)


def ssd_out(yf, yb, u, fb, d_skip, norm_w, tm=768):
    m = u.shape[0]
    dsk = jnp.repeat(d_skip.astype(F32), SSD_HEAD_DIM).reshape(1, SSD_W)
    return pl.pallas_call(
        _ssd_out_kernel,
        grid=(m // tm,),
        in_specs=[pl.BlockSpec((tm, SSD_W), lambda i: (i, 0)),
                  pl.BlockSpec((tm, SSD_W), lambda i: (i, 0)),
                  pl.BlockSpec((tm, SSD_W), lambda i: (i, 0)),
                  pl.BlockSpec((tm, SSD_W), lambda i: (i, ZX_Z // SSD_W)),
                  pl.BlockSpec((1, SSD_W), lambda i: (0, 0)),
                  pl.BlockSpec((1, SSD_W), lambda i: (0, 0))],
        out_specs=pl.BlockSpec((tm, SSD_W), lambda i: (i, 0)),
        out_shape=jax.ShapeDtypeStruct((m, SSD_W), BF16),
        compiler_params=_cparams(("arbitrary",)),
        name="ssd_out",
    )(yf, yb, u, fb, dsk, norm_w.reshape(1, SSD_W))


def _dft_tables(n):
    ang = 2.0 * np.pi * np.outer(np.arange(n), np.arange(n)) / n
    return jnp.asarray(np.cos(ang), F32), jnp.asarray(np.sin(ang), F32)


def _split_bf16(a):
    hi = a.astype(BF16)
    return hi, (a - hi.astype(F32)).astype(BF16)


def _hdot(a, b):
    ah, al = _split_bf16(a)
    bh, bl = _split_bf16(b)
    return (jnp.dot(ah, bh, preferred_element_type=F32)
            + (jnp.dot(ah, bl, preferred_element_type=F32) + jnp.dot(al, bh, preferred_element_type=F32)))


def _fnet_a_kernel(c_ref, s_ref, x_ref, yr_ref, yi_ref):
    x = x_ref[...]
    yr_ref[...] = _hdot(c_ref[...], x)
    yi_ref[...] = -_hdot(s_ref[...], x)


def _fnet_b_kernel(yr_ref, yi_ref, twc_ref, tws_ref, c1_ref, s1_ref, cc_ref, sc_ref, o_ref, *, scale, kb):
    c1, s1 = c1_ref[...], s1_ref[...]
    cc, sc = cc_ref[...], sc_ref[...]
    for j in range(kb):
        yr, yi = yr_ref[j], yi_ref[j]
        tc, ts = twc_ref[j], tws_ref[j]
        pr = yr * tc + yi * ts
        pi = yi * tc - yr * ts
        zr = _hdot(c1, pr) + _hdot(s1, pi)
        zi = _hdot(c1, pi) - _hdot(s1, pr)
        for g in range(FNET_GROUPS):
            cs = slice(g * FNET_GROUP_DIM, (g + 1) * FNET_GROUP_DIM)
            o_ref[:, j, cs] = (_hdot(zr[:, cs], cc) + _hdot(zi[:, cs], sc)) * scale


def _fnet_ctx_kernel(x_ref, cn_ref, sn_ref, cc_ref, sc_ref, o_ref, *, scale):
    x = x_ref[...]
    wr = _hdot(cn_ref[...], x)
    ws = _hdot(sn_ref[...], x)
    cc, sc = cc_ref[...], sc_ref[...]
    for g in range(FNET_GROUPS):
        cs = slice(g * FNET_GROUP_DIM, (g + 1) * FNET_GROUP_DIM)
        o_ref[:, cs] = (_hdot(wr[:, cs], cc) - _hdot(ws[:, cs], sc)) * scale


def fourier_mix(f_lat, f_ctx):
    n_lat, w = f_lat.shape
    n_ctx = f_ctx.shape[0]
    n2 = 128
    n1 = n_lat // n2
    c2, s2 = _dft_tables(n2)
    c1, s1 = _dft_tables(n1)
    cc, sc = _dft_tables(FNET_GROUP_DIM)
    tw = 2.0 * np.pi * np.outer(np.arange(n2), np.arange(n1)) / n_lat
    twc = jnp.asarray(np.cos(tw), F32).reshape(n2, n1, 1)
    tws = jnp.asarray(np.sin(tw), F32).reshape(n2, n1, 1)
    xr = f_lat.reshape(n2, n1 * w)
    tn = min(4096, n1 * w)
    yr, yi = pl.pallas_call(
        _fnet_a_kernel,
        grid=(n1 * w // tn,),
        in_specs=[pl.BlockSpec((n2, n2), lambda j: (0, 0)),
                  pl.BlockSpec((n2, n2), lambda j: (0, 0)),
                  pl.BlockSpec((n2, tn), lambda j: (0, j))],
        out_specs=[pl.BlockSpec((n2, tn), lambda j: (0, j)),
                   pl.BlockSpec((n2, tn), lambda j: (0, j))],
        out_shape=[jax.ShapeDtypeStruct((n2, n1 * w), F32)] * 2,
        compiler_params=_cparams(("arbitrary",)),
        name="fnet_stage_a",
    )(c2, s2, xr)
    kb = 8
    lat = pl.pallas_call(
        functools.partial(_fnet_b_kernel, scale=float(1.0 / math.sqrt(n_lat * FNET_GROUP_DIM)), kb=kb),
        grid=(n2 // kb,),
        in_specs=[pl.BlockSpec((kb, n1, w), lambda j: (j, 0, 0)),
                  pl.BlockSpec((kb, n1, w), lambda j: (j, 0, 0)),
                  pl.BlockSpec((kb, n1, 1), lambda j: (j, 0, 0)),
                  pl.BlockSpec((kb, n1, 1), lambda j: (j, 0, 0)),
                  pl.BlockSpec((n1, n1), lambda j: (0, 0)),
                  pl.BlockSpec((n1, n1), lambda j: (0, 0)),
                  pl.BlockSpec((FNET_GROUP_DIM, FNET_GROUP_DIM), lambda j: (0, 0)),
                  pl.BlockSpec((FNET_GROUP_DIM, FNET_GROUP_DIM), lambda j: (0, 0))],
        out_specs=pl.BlockSpec((n1, kb, w), lambda j: (0, j, 0)),
        out_shape=jax.ShapeDtypeStruct((n1, n2, w), F32),
        compiler_params=_cparams(("arbitrary",)),
        name="fnet_stage_b",
    )(yr.reshape(n2, n1, w), yi.reshape(n2, n1, w), twc, tws, c1, s1, cc, sc)
    cn, sn = _dft_tables(n_ctx)
    ctx = pl.pallas_call(
        functools.partial(_fnet_ctx_kernel, scale=float(1.0 / math.sqrt(n_ctx * FNET_GROUP_DIM))),
        out_shape=jax.ShapeDtypeStruct((n_ctx, w), F32),
        compiler_params=pltpu.CompilerParams(vmem_limit_bytes=VMEM_LIMIT_BYTES),
        name="fnet_ctx",
    )(f_ctx, cn, sn, cc, sc)
    return jnp.concatenate([ctx, lat.reshape(n_lat, w)], axis=0)


def _merge_kernel(h_ref, *refs, shift, tn):
    wa_refs, wb_refs, o_refs = refs[0:4], refs[4:8], refs[8:12]
    wbr_ref, y_ref, ws_ref = refs[12], refs[13], refs[14]

    @pl.when(pl.program_id(1) == 0)
    def _realign():
        for b in range(N_BRANCH):
            wcat = jnp.concatenate([wa_refs[b][0], wb_refs[b][0]], axis=1)
            ws_ref[b] = wcat[:, shift:shift + tn].astype(BF16)

    h = h_ref[...]
    y = None
    for b in range(N_BRANCH):
        g = jnp.dot(h, ws_ref[b], preferred_element_type=F32)
        pr = jnp.dot(o_refs[b][...].astype(BF16), wbr_ref[0, b].astype(BF16), preferred_element_type=F32)
        t = jax.nn.sigmoid(g) * pr
        y = t if y is None else y + t
    y_ref[...] = y.astype(y_ref.dtype)


def merge(h, w_in_all, gate_col0, branches, w_branch_all, layer, tm=768, tn=256):
    m, k = h.shape
    nb = D_MODEL // tn
    base = (gate_col0 // LANES) * LANES
    shift = gate_col0 - base
    assert base % tn == 0 and 0 < shift < LANES
    r = tn // LANES
    c0 = base // tn

    def wa_map(j, i, b):
        return (layer, 0, c0 + b * nb + j)

    def wb_map(j, i, b):
        return (layer, 0, (c0 + b * nb + j + 1) * r)

    wa_specs = [pl.BlockSpec((1, k, tn), functools.partial(wa_map, b=b)) for b in range(N_BRANCH)]
    wb_specs = [pl.BlockSpec((1, k, LANES), functools.partial(wb_map, b=b)) for b in range(N_BRANCH)]
    o_specs = [pl.BlockSpec((tm, BRANCH_W), lambda j, i: (i, 0)) for _ in range(N_BRANCH)]
    return pl.pallas_call(
        functools.partial(_merge_kernel, shift=shift, tn=tn),
        grid=(nb, m // tm),
        in_specs=([pl.BlockSpec((tm, k), lambda j, i: (i, 0))] + wa_specs + wb_specs + o_specs
                  + [pl.BlockSpec((1, N_BRANCH, BRANCH_W, tn), lambda j, i: (layer, 0, 0, j))]),
        out_specs=pl.BlockSpec((tm, tn), lambda j, i: (i, j)),
        out_shape=jax.ShapeDtypeStruct((m, D_MODEL), BF16),
        scratch_shapes=[pltpu.VMEM((N_BRANCH, k, tn), BF16)],
        compiler_params=_cparams(("arbitrary", "arbitrary")),
        name="merge",
    )(h, *([w_in_all] * (2 * N_BRANCH)), *branches, w_branch_all)


def _expert_kernel(idx_ref, h_hbm, wg_ref, wu_ref, wd_ref, gate_ref, o_ref, xs_ref, sem):
    e = pl.program_id(0)
    f = pl.program_id(1)
    cap = xs_ref.shape[1]
    slot = e % 2

    def start_gather(ee, sl):
        def issue(c, carry):
            row = idx_ref[ee, c]
            pltpu.make_async_copy(h_hbm.at[pl.ds(row, 1)], xs_ref.at[sl, pl.ds(c, 1)], sem.at[sl]).start()
            return carry

        lax.fori_loop(0, cap, issue, 0, unroll=8)

    @pl.when(jnp.logical_and(e == 0, f == 0))
    def _cold_start():
        start_gather(0, 0)

    @pl.when(f == 0)
    def _wait_rows():
        pltpu.make_async_copy(h_hbm.at[pl.ds(0, cap)], xs_ref.at[slot], sem.at[slot]).wait()

    @pl.when(jnp.logical_and(f == 0, e + 1 < pl.num_programs(0)))
    def _prefetch_next():
        start_gather(e + 1, 1 - slot)

    xs = xs_ref[slot].astype(BF16)
    hg = jnp.dot(xs, wg_ref[0, 0].astype(BF16), preferred_element_type=F32)
    hu = jnp.dot(xs, wu_ref[0, 0].astype(BF16), preferred_element_type=F32)
    hid = (_silu(hg) * hu).astype(BF16)
    part = jnp.dot(hid, wd_ref[0, 0].astype(BF16), preferred_element_type=F32)

    @pl.when(f == 0)
    def _first():
        o_ref[0] = part

    @pl.when(f != 0)
    def _rest():
        o_ref[0] += part

    @pl.when(f == pl.num_programs(1) - 1)
    def _done():
        o_ref[0] = o_ref[0] * gate_ref[0]


def expert_ffn(h2, idx, gate, w_gate, w_up, w_down, layer, tf=256):
    ne, cap = idx.shape
    d = h2.shape[1]
    ff = w_gate.shape[3]
    grid_spec = pltpu.PrefetchScalarGridSpec(
        num_scalar_prefetch=1,
        grid=(ne, ff // tf),
        in_specs=[pl.BlockSpec(memory_space=pl.ANY),
                  pl.BlockSpec((1, 1, d, tf), lambda e, f, ix: (layer, e, 0, f)),
                  pl.BlockSpec((1, 1, d, tf), lambda e, f, ix: (layer, e, 0, f)),
                  pl.BlockSpec((1, 1, tf, d), lambda e, f, ix: (layer, e, f, 0)),
                  pl.BlockSpec((1, cap, 1), lambda e, f, ix: (e, 0, 0))],
        out_specs=pl.BlockSpec((1, cap, d), lambda e, f, ix: (e, 0, 0)),
        scratch_shapes=[pltpu.VMEM((2, cap, d), F32), pltpu.SemaphoreType.DMA((2,))],
    )
    return pl.pallas_call(
        _expert_kernel,
        grid_spec=grid_spec,
        out_shape=jax.ShapeDtypeStruct((ne, cap, d), F32),
        compiler_params=_cparams(("arbitrary", "arbitrary")),
        name="expert_ffn",
    )(idx, h2, w_gate, w_up, w_down, gate.reshape(ne, cap, 1))


def moe(h2, aff, w_gate, w_up, w_down, layer, n_ctx):
    m = h2.shape[0]
    n_lat = m - n_ctx
    a = aff[:, :N_EXPERTS]
    g_c, i_c = lax.top_k(a[:n_ctx].T, CAPACITY_FACTOR * n_ctx // N_EXPERTS)
    g_l, i_l = lax.top_k(a[n_ctx:].T, CAPACITY_FACTOR * n_lat // N_EXPERTS)
    idx = jnp.concatenate([i_c, i_l + n_ctx], axis=1).astype(jnp.int32)
    gate = jnp.concatenate([g_c, g_l], axis=1)
    ye = expert_ffn(h2, idx, gate, w_gate, w_up, w_down, layer)
    return jnp.zeros((m, D_MODEL), F32).at[idx.reshape(-1)].add(ye.reshape(-1, D_MODEL))


_P_NA = 0
_P_Z = _P_NA + 3 * NA_HEADS * HEAD_DIM
_P_XBC = _P_Z + SSD_W
_P_DT = _P_XBC + SSD_XBC_W
_P_GQ = _P_DT + 2 * SSD_HEADS
_P_FN = _P_GQ + (GQA_HEADS + 2 * GQA_KV_HEADS) * HEAD_DIM
_P_GATES = _P_FN + FNET_GROUPS * FNET_GROUP_DIM


def _layer(x, mod, n_ctx, layer, norm_mix, norm_ffn, w_in_all, na_rpb, conv_w, conv_b, a_log, dt_bias, d_skip,
           ssd_gn, q_gain, k_gain, w_branch_all, w_out_all, w_router, w_gate_all, w_up_all, w_down_all, rope):
    m = x.shape[0]
    tm = 768 if m % 768 == 0 else 256
    h = norm_mod(x, norm_mix, mod, n_ctx, 0, tm=tm)
    ab = matmul_cols(h, w_in_all, layer, _P_NA, _P_Z - _P_NA, BF16, tm, 768, "proj_na")
    fzx = matmul_cols(h, w_in_all, layer, _P_Z, _P_DT - _P_Z, F32, tm, 512, "proj_zx", out_perm=(2, 0, 1))
    fr = matmul_cols_unaligned(h, w_in_all, layer, _P_GQ, _P_GATES - _P_GQ, F32, tm, 512, "proj_rest")
    fdt = matmul_cols(h, w_in_all, layer, _P_DT, LANES, F32, tm, LANES, "proj_dt")

    rows = (m - n_ctx) // GRID_W
    bias, tid = na_bias_table(na_rpb, rows)
    o_na = na_attention(ab, bias, tid, n_ctx)

    u, dt = ssd_prep(fzx, fdt, conv_w, conv_b, dt_bias, n_ctx)
    yf, yb = ssd_scan(u, dt, a_log, n_ctx)
    o_ssd = ssd_out(yf, yb, u, fzx, d_skip, ssd_gn, tm=tm)

    qt, kn, vt = qk_prep(fr, rope[0], rope[1], q_gain, k_gain, tm=tm)
    o_gqa = gqa_attention(qt, kn, vt, n_ctx)

    f_in = fr[:, FR_FN:FR_FN + FNET_GROUPS * FNET_GROUP_DIM]
    o_fn = fourier_mix(f_in[n_ctx:], f_in[:n_ctx])

    y = merge(h, w_in_all, _P_GATES, (o_na, o_ssd, o_gqa, o_fn), w_branch_all, layer, tm=tm)
    x = matmul_residual(y, w_out_all, layer, x, mod, 2, n_ctx, tm, 1024, "out_proj")

    wr = jnp.concatenate([w_router, jnp.zeros((D_MODEL, LANES - N_EXPERTS), w_router.dtype)], axis=1)
    h2, aff = norm_router(x, norm_ffn, mod, wr, n_ctx, 3, tm=tm)
    mo = moe(h2, aff, w_gate_all, w_up_all, w_down_all, layer, n_ctx)
    row = jnp.arange(m)[:, None]
    g2 = jnp.where(row < n_ctx, mod[1, 5 * D_MODEL:][None], mod[0, 5 * D_MODEL:][None])
    return x + g2 * mo


def kernel(x, c, ctx, c_ctx, w_ada, b_ada, norm_mix, norm_ffn, w_in, na_rpb, ssd_conv_w, ssd_conv_b, ssd_a_log,
           ssd_dt_bias, ssd_d, ssd_norm, gqa_q_norm, gqa_k_norm, w_branch, w_out, w_router, moe_w_gate, moe_w_up,
           moe_w_down, final_norm_w):
    n_lat = x.shape[1]
    n_ctx = ctx.shape[1]
    depth = w_ada.shape[0]
    cc = jnp.concatenate([c[0:1], c_ctx[None], jnp.zeros((6, D_MODEL), F32)], axis=0)
    mods = ada_all(cc, w_ada, b_ada)
    rope = rope_tables(n_ctx, n_lat)
    xs = jnp.concatenate([ctx[0], x[0]], axis=0)
    for l in range(depth):
        xs = _layer(xs, mods[l], n_ctx, l, norm_mix[l], norm_ffn[l], w_in, na_rpb[l], ssd_conv_w[l],
                    ssd_conv_b[l], ssd_a_log[l], ssd_dt_bias[l], ssd_d[l], ssd_norm[l], gqa_q_norm[l],
                    gqa_k_norm[l], w_branch, w_out, w_router[l], moe_w_gate, moe_w_up, moe_w_down, rope)
    out = final_norm(xs, final_norm_w, n_ctx, n_lat)
    return out[None]
```

```python
import functools
import math

import numpy as np
import jax
import jax.numpy as jnp
from jax import lax
from jax.experimental import pallas as pl
from jax.experimental.pallas import tpu as pltpu

F32 = jnp.float32
BF16 = jnp.bfloat16

D_MODEL = 2048
GRID_W = 64
HEAD_DIM = 128
RMS_EPS = 1e-6
N_BRANCH = 4
BRANCH_W = 512
NA_HEADS = 4
NA_WIN_R = 8
NA_WIN_C = 16
SSD_HEADS = 8
SSD_HEAD_DIM = 64
SSD_GROUPS = 2
SSD_STATE = 128
SSD_CONV = 5
SSD_CHUNK = 128
SSD_W = SSD_HEADS * SSD_HEAD_DIM
SSD_XBC_W = SSD_W + 2 * SSD_GROUPS * SSD_STATE
GQA_HEADS = 4
GQA_KV_HEADS = 2
ROPE_THETA = 10000.0
FNET_GROUPS = 4
FNET_GROUP_DIM = 128
N_EXPERTS = 16
EXPERT_FF = D_MODEL // 2
CAPACITY_FACTOR = 2
LOG2E = 1.4426950408889634

VMEM_LIMIT_BYTES = 56 * 1024 * 1024
LANES = 128

ZX_XBC, ZX_Z, ZX_W = 0, 1024, 1536
FR_Q, FR_K, FR_V, FR_FN, FR_W = 0, 512, 768, 1024, 1536


def _cparams(sem):
    return pltpu.CompilerParams(dimension_semantics=sem, vmem_limit_bytes=VMEM_LIMIT_BYTES)


def _silu(x):
    return x * jax.nn.sigmoid(x)


def _ada_kernel(ct_ref, w_ref, b_ref, o_ref):
    a = _silu(ct_ref[...])
    w = w_ref[0]
    rows = [jnp.sum(w * a[:, r:r + 1], axis=0, keepdims=True) + b_ref[0] for r in range(2)]
    o_ref[0] = jnp.concatenate(rows + [jnp.zeros((6, w.shape[1]), F32)], axis=0)


def ada_all(cc, w_ada, b_ada):
    depth, d, n = w_ada.shape
    tn = 1024
    return pl.pallas_call(
        _ada_kernel,
        grid=(depth, n // tn),
        in_specs=[pl.BlockSpec((d, 8), lambda l, j: (0, 0)),
                  pl.BlockSpec((1, d, tn), lambda l, j: (l, 0, j)),
                  pl.BlockSpec((1, 1, tn), lambda l, j: (l, 0, j))],
        out_specs=pl.BlockSpec((1, 8, tn), lambda l, j: (l, 0, j)),
        out_shape=jax.ShapeDtypeStruct((depth, 8, n), F32),
        compiler_params=_cparams(("arbitrary", "arbitrary")),
        name="ada_mod",
    )(cc.T, w_ada, b_ada.reshape(depth, 1, n))


def _row_select(mod_ref, chunk, row_is_ctx):
    lo = chunk * D_MODEL
    return jnp.where(row_is_ctx, mod_ref[1:2, lo:lo + D_MODEL], mod_ref[0:1, lo:lo + D_MODEL])


def _norm_body(x_ref, nw_ref):
    x = x_ref[...]
    ms = jnp.mean(x * x, axis=-1, keepdims=True)
    return x * lax.rsqrt(ms + RMS_EPS) * nw_ref[...]


def _norm_mod_kernel(x_ref, nw_ref, mod_ref, o_ref, *, n_ctx, tm, sh_chunk):
    y = _norm_body(x_ref, nw_ref)
    row = pl.program_id(0) * tm + lax.broadcasted_iota(jnp.int32, (tm, 1), 0)
    is_ctx = row < n_ctx
    sh = _row_select(mod_ref, sh_chunk, is_ctx)
    sc = _row_select(mod_ref, sh_chunk + 1, is_ctx)
    o_ref[...] = (y * (1.0 + sc) + sh).astype(o_ref.dtype)


def norm_mod(x, nw, mod, n_ctx, sh_chunk, tm=768):
    m, d = x.shape
    return pl.pallas_call(
        functools.partial(_norm_mod_kernel, n_ctx=n_ctx, tm=tm, sh_chunk=sh_chunk),
        grid=(m // tm,),
        in_specs=[pl.BlockSpec((tm, d), lambda i: (i, 0)),
                  pl.BlockSpec((1, d), lambda i: (0, 0)),
                  pl.BlockSpec(mod.shape, lambda i: (0, 0))],
        out_specs=pl.BlockSpec((tm, d), lambda i: (i, 0)),
        out_shape=jax.ShapeDtypeStruct((m, d), BF16),
        compiler_params=_cparams(("arbitrary",)),
        name="norm_mod",
    )(x, nw.reshape(1, d), mod)


def _norm_router_kernel(x_ref, nw_ref, mod_ref, wr_ref, h_ref, aff_ref, *, n_ctx, tm, sh_chunk):
    y = _norm_body(x_ref, nw_ref)
    row = pl.program_id(0) * tm + lax.broadcasted_iota(jnp.int32, (tm, 1), 0)
    is_ctx = row < n_ctx
    sh = _row_select(mod_ref, sh_chunk, is_ctx)
    sc = _row_select(mod_ref, sh_chunk + 1, is_ctx)
    h = y * (1.0 + sc) + sh
    h_ref[...] = h
    logits = jnp.dot(h.astype(BF16), wr_ref[...].astype(BF16), preferred_element_type=F32)
    lane = lax.broadcasted_iota(jnp.int32, logits.shape, 1)
    logits = jnp.where(lane < N_EXPERTS, logits, -jnp.inf)
    mx = jnp.max(logits, axis=-1, keepdims=True)
    e = jnp.exp(logits - mx)
    aff_ref[...] = e / jnp.sum(e, axis=-1, keepdims=True)


def norm_router(x, nw, mod, w_router_pad, n_ctx, sh_chunk, tm=768):
    m, d = x.shape
    return pl.pallas_call(
        functools.partial(_norm_router_kernel, n_ctx=n_ctx, tm=tm, sh_chunk=sh_chunk),
        grid=(m // tm,),
        in_specs=[pl.BlockSpec((tm, d), lambda i: (i, 0)),
                  pl.BlockSpec((1, d), lambda i: (0, 0)),
                  pl.BlockSpec(mod.shape, lambda i: (0, 0)),
                  pl.BlockSpec((d, LANES), lambda i: (0, 0))],
        out_specs=[pl.BlockSpec((tm, d), lambda i: (i, 0)),
                   pl.BlockSpec((tm, LANES), lambda i: (i, 0))],
        out_shape=[jax.ShapeDtypeStruct((m, d), F32),
                   jax.ShapeDtypeStruct((m, LANES), F32)],
        compiler_params=_cparams(("arbitrary",)),
        name="norm_router",
    )(x, nw.reshape(1, d), mod, w_router_pad)


def _final_norm_kernel(x_ref, nw_ref, o_ref):
    o_ref[...] = _norm_body(x_ref, nw_ref)


def final_norm(x, nw, row0, n_rows, tm=256):
    d = x.shape[1]
    off = row0 // tm
    return pl.pallas_call(
        _final_norm_kernel,
        grid=(n_rows // tm,),
        in_specs=[pl.BlockSpec((tm, d), lambda i: (i + off, 0)),
                  pl.BlockSpec((1, d), lambda i: (0, 0))],
        out_specs=pl.BlockSpec((tm, d), lambda i: (i, 0)),
        out_shape=jax.ShapeDtypeStruct((n_rows, d), F32),
        compiler_params=_cparams(("arbitrary",)),
        name="final_norm",
    )(x, nw.reshape(1, d))


def _mm_kernel(a_ref, w_ref, o_ref):
    o_ref[...] = jnp.dot(a_ref[...].astype(BF16), w_ref[...].astype(BF16),
                         preferred_element_type=F32).astype(o_ref.dtype)


def matmul(a, w, out_dtype, tm, tn, name):
    m, k = a.shape
    n = w.shape[1]
    return pl.pallas_call(
        _mm_kernel,
        grid=(n // tn, m // tm),
        in_specs=[pl.BlockSpec((tm, k), lambda j, i: (i, 0)),
                  pl.BlockSpec((k, tn), lambda j, i: (0, j))],
        out_specs=pl.BlockSpec((tm, tn), lambda j, i: (i, j)),
        out_shape=jax.ShapeDtypeStruct((m, n), out_dtype),
        compiler_params=_cparams(("arbitrary", "arbitrary")),
        name=name,
    )(a, w)


def _mm_layer_kernel(a_ref, w_ref, o_ref):
    o_ref[...] = jnp.dot(a_ref[...], w_ref[0].astype(BF16), preferred_element_type=F32).astype(o_ref.dtype)


def matmul_cols(a, w_all, layer, col0, ncols, out_dtype, tm, tn, name, out_perm=None):
    m, k = a.shape
    nb = ncols // tn
    c0 = col0 // tn
    perm = tuple(range(nb)) if out_perm is None else tuple(out_perm)

    def out_map(j, i):
        pj = j
        for src, dst in enumerate(perm):
            pj = jnp.where(j == src, dst, pj)
        return (i, pj)

    return pl.pallas_call(
        _mm_layer_kernel,
        grid=(nb, m // tm),
        in_specs=[pl.BlockSpec((tm, k), lambda j, i: (i, 0)),
                  pl.BlockSpec((1, k, tn), lambda j, i: (layer, 0, c0 + j))],
        out_specs=pl.BlockSpec((tm, tn), out_map),
        out_shape=jax.ShapeDtypeStruct((m, ncols), out_dtype),
        compiler_params=_cparams(("arbitrary", "arbitrary")),
        name=name,
    )(a, w_all)


def _mm_shift_kernel(a_ref, wa_ref, wb_ref, o_ref, ws_ref, *, shift, tn):
    @pl.when(pl.program_id(1) == 0)
    def _realign():
        wcat = jnp.concatenate([wa_ref[0], wb_ref[0]], axis=1)
        ws_ref[...] = wcat[:, shift:shift + tn].astype(BF16)

    o_ref[...] = jnp.dot(a_ref[...], ws_ref[...], preferred_element_type=F32).astype(o_ref.dtype)


def matmul_cols_unaligned(a, w_all, layer, col0, ncols, out_dtype, tm, tn, name):
    m, k = a.shape
    base = (col0 // LANES) * LANES
    shift = col0 - base
    assert base % tn == 0 and ncols % tn == 0 and 0 < shift < LANES
    c0 = base // tn
    r = tn // LANES
    return pl.pallas_call(
        functools.partial(_mm_shift_kernel, shift=shift, tn=tn),
        grid=(ncols // tn, m // tm),
        in_specs=[pl.BlockSpec((tm, k), lambda j, i: (i, 0)),
                  pl.BlockSpec((1, k, tn), lambda j, i: (layer, 0, c0 + j)),
                  pl.BlockSpec((1, k, LANES), lambda j, i: (layer, 0, (c0 + j + 1) * r))],
        out_specs=pl.BlockSpec((tm, tn), lambda j, i: (i, j)),
        out_shape=jax.ShapeDtypeStruct((m, ncols), out_dtype),
        scratch_shapes=[pltpu.VMEM((k, tn), BF16)],
        compiler_params=_cparams(("arbitrary", "arbitrary")),
        name=name,
    )(a, w_all, w_all)


def _mm_residual_kernel(a_ref, w_ref, x_ref, g_ref, o_ref, *, n_ctx, tm):
    acc = jnp.dot(a_ref[...].astype(BF16), w_ref[0].astype(BF16), preferred_element_type=F32)
    row = pl.program_id(1) * tm + lax.broadcasted_iota(jnp.int32, (tm, 1), 0)
    g = jnp.where(row < n_ctx, g_ref[1:2, :], g_ref[0:1, :])
    o_ref[...] = x_ref[...] + g * acc


def matmul_residual(a, w_all, layer, x, mod, gate_chunk, n_ctx, tm, tn, name):
    m, k = a.shape
    n = w_all.shape[2]
    goff = gate_chunk * D_MODEL // tn
    return pl.pallas_call(
        functools.partial(_mm_residual_kernel, n_ctx=n_ctx, tm=tm),
        grid=(n // tn, m // tm),
        in_specs=[pl.BlockSpec((tm, k), lambda j, i: (i, 0)),
                  pl.BlockSpec((1, k, tn), lambda j, i: (layer, 0, j)),
                  pl.BlockSpec((tm, tn), lambda j, i: (i, j)),
                  pl.BlockSpec((8, tn), lambda j, i: (0, goff + j))],
        out_specs=pl.BlockSpec((tm, tn), lambda j, i: (i, j)),
        out_shape=jax.ShapeDtypeStruct((m, n), F32),
        compiler_params=_cparams(("arbitrary", "arbitrary")),
        name=name,
    )(a, w_all, x, mod)


def _softmax_pv(parts):
    m = None
    for s, _ in parts:
        mi = jnp.max(s, axis=-1, keepdims=True)
        m = mi if m is None else jnp.maximum(m, mi)
    ps, l = [], None
    for s, _ in parts:
        p = jnp.exp(s - m)
        ps.append(p)
        li = jnp.sum(p, axis=-1, keepdims=True)
        l = li if l is None else l + li
    inv = 1.0 / l
    o = None
    for p, (_, v) in zip(ps, parts):
        oi = jnp.dot((p * inv).astype(BF16), v, preferred_element_type=F32)
        o = oi if o is None else o + oi
    return o


def _qk(q, k):
    return lax.dot_general(q, k, (((1,), (1,)), ((), ())), preferred_element_type=F32)


NA_QROWS = 4
NA_KROWS = NA_WIN_R + NA_QROWS - 1


def _na_kernel(tid_ref, q_ref, k_ref, v_ref, b_ref, o_ref, *, n_ctx, rows):
    del tid_ref
    step = pl.program_id(0)
    tq = NA_QROWS * GRID_W
    n_ctx_steps = n_ctx // tq
    scale = HEAD_DIM ** -0.5
    win = NA_KROWS * GRID_W

    @pl.when(step < n_ctx_steps)
    def _ctx():
        for h in range(NA_HEADS):
            cs = slice(h * HEAD_DIM, (h + 1) * HEAD_DIM)
            q = q_ref[:, cs]
            s_c = _qk(q, k_ref[0:n_ctx, cs]) * scale
            o_ref[:, cs] = _softmax_pv([(s_c, v_ref[0:n_ctx, cs])]).astype(o_ref.dtype)

    @pl.when(step >= n_ctx_steps)
    def _lat():
        p = step - n_ctx_steps
        row0 = jnp.clip(NA_QROWS * p - NA_WIN_R // 2, 0, rows - NA_KROWS)
        base = pl.multiple_of(n_ctx + row0 * GRID_W, GRID_W)
        for h in range(NA_HEADS):
            cs = slice(h * HEAD_DIM, (h + 1) * HEAD_DIM)
            q = q_ref[:, cs]
            s_w = _qk(q, k_ref[pl.ds(base, win), cs]) * scale + b_ref[0, h]
            s_c = _qk(q, k_ref[0:n_ctx, cs]) * scale
            o = _softmax_pv([(s_w, v_ref[pl.ds(base, win), cs]), (s_c, v_ref[0:n_ctx, cs])])
            o_ref[:, cs] = o.astype(o_ref.dtype)


def _na_patterns(rows):
    a = np.arange(NA_QROWS)[:, None]
    i = np.arange(NA_KROWS)[None, :]
    pats, keys, tid = [], [], []
    for p in range(rows // NA_QROWS):
        row0 = int(np.clip(NA_QROWS * p - NA_WIN_R // 2, 0, rows - NA_KROWS))
        r = NA_QROWS * p + a
        rs = np.clip(r - NA_WIN_R // 2, 0, rows - NA_WIN_R)
        krow = row0 + i
        valid = (krow >= rs) & (krow < rs + NA_WIN_R)
        ridx = np.clip(krow - r + NA_WIN_R - 1, 0, 2 * NA_WIN_R - 2)
        key = (valid.tobytes(), ridx.tobytes())
        if key not in keys:
            keys.append(key)
            pats.append((valid, ridx))
        tid.append(keys.index(key))
    return np.stack([v for v, _ in pats]), np.stack([x for _, x in pats]), np.asarray(tid, np.int32)


def na_bias_table(rpb, rows):
    valid_r, ridx, tid = _na_patterns(rows)
    kc = np.arange(GRID_W)[None, :]
    qc = np.arange(GRID_W)[:, None]
    col_start = np.clip(qc - NA_WIN_C // 2, 0, GRID_W - NA_WIN_C)
    valid_c = (kc >= col_start) & (kc < col_start + NA_WIN_C)
    cidx = np.clip(kc - qc + NA_WIN_C - 1, 0, 2 * NA_WIN_C - 2)
    nh, nbr, nbc = rpb.shape
    onehot = (np.arange(nbc)[:, None, None] == cidx[None]) & valid_c[None]
    toe = jnp.einsum('hbj,jqk->hbqk', rpb.astype(F32), jnp.asarray(onehot, F32), precision=lax.Precision.HIGHEST)
    toe = toe + jnp.asarray(np.where(valid_c, 0.0, -np.inf), F32)
    neg = jnp.full((nh, GRID_W, GRID_W), -jnp.inf, F32)
    npat = valid_r.shape[0]
    blocks = [toe[:, int(ridx[p, a, i])] if valid_r[p, a, i] else neg
              for p in range(npat) for a in range(NA_QROWS) for i in range(NA_KROWS)]
    t = jnp.stack(blocks, axis=0).reshape(npat, NA_QROWS, NA_KROWS, nh, GRID_W, GRID_W)
    t = jnp.transpose(t, (0, 3, 1, 4, 2, 5))
    return t.reshape(npat, nh, NA_QROWS * GRID_W, NA_KROWS * GRID_W), jnp.asarray(tid)


def na_attention(ab, bias, tid, n_ctx):
    m = ab.shape[0]
    rows = (m - n_ctx) // GRID_W
    tq = NA_QROWS * GRID_W
    n_ctx_steps = n_ctx // tq
    w = NA_HEADS * HEAD_DIM
    grid_spec = pltpu.PrefetchScalarGridSpec(
        num_scalar_prefetch=1,
        grid=(m // tq,),
        in_specs=[pl.BlockSpec((tq, w), lambda s, t: (s, 0)),
                  pl.BlockSpec((m, w), lambda s, t: (0, 1)),
                  pl.BlockSpec((m, w), lambda s, t: (0, 2)),
                  pl.BlockSpec((1, NA_HEADS, tq, NA_KROWS * GRID_W),
                               lambda s, t: (t[jnp.maximum(s - n_ctx_steps, 0)], 0, 0, 0))],
        out_specs=pl.BlockSpec((tq, w), lambda s, t: (s, 0)),
    )
    return pl.pallas_call(
        functools.partial(_na_kernel, n_ctx=n_ctx, rows=rows),
        grid_spec=grid_spec,
        out_shape=jax.ShapeDtypeStruct((m, w), BF16),
        compiler_params=_cparams(("arbitrary",)),
        name="na_attn",
    )(tid, ab, ab, ab, bias)


def rope_tables(n_ctx, n_lat):
    t = jnp.arange(n_lat)
    pos = jnp.stack([t // GRID_W, t % GRID_W], axis=-1).astype(F32)
    n_freq = HEAD_DIM // 4
    inv = ROPE_THETA ** (-jnp.arange(n_freq, dtype=F32) / n_freq)
    ang = pos[:, :, None] * inv
    c, s = jnp.cos(ang), jnp.sin(ang)
    cos = jnp.concatenate([c[:, 0], c[:, 0], c[:, 1], c[:, 1]], axis=-1)
    sin = jnp.concatenate([-s[:, 0], s[:, 0], -s[:, 1], s[:, 1]], axis=-1)
    cos = jnp.concatenate([jnp.ones((n_ctx, HEAD_DIM), F32), cos], axis=0)
    sin = jnp.concatenate([jnp.zeros((n_ctx, HEAD_DIM), F32), sin], axis=0)
    return cos, sin


def _qk_prep_kernel(x_ref, cos_ref, sin_ref, qg_ref, kg_ref, qt_ref, ko_ref, vt_ref):
    cos = cos_ref[...]
    sin = sin_ref[...]
    lane = lax.broadcasted_iota(jnp.int32, (1, HEAD_DIM), 1)
    first_half = (lane % (HEAD_DIM // 2)) < (HEAD_DIM // 4)
    for h in range(GQA_HEADS + GQA_KV_HEADS):
        x = x_ref[:, h * HEAD_DIM:(h + 1) * HEAD_DIM]
        gain = qg_ref[...] if h < GQA_HEADS else kg_ref[...]
        y = x * lax.rsqrt(jnp.mean(x * x, axis=-1, keepdims=True) + RMS_EPS) * gain
        sw = jnp.where(first_half, pltpu.roll(y, HEAD_DIM - HEAD_DIM // 4, 1), pltpu.roll(y, HEAD_DIM // 4, 1))
        out = y * cos + sw * sin
        if h < GQA_HEADS:
            qt_ref[h * HEAD_DIM:(h + 1) * HEAD_DIM, :] = out.T.astype(BF16)
        else:
            hk = h - GQA_HEADS
            ko_ref[:, hk * HEAD_DIM:(hk + 1) * HEAD_DIM] = out.astype(BF16)
    v0 = (GQA_HEADS + GQA_KV_HEADS) * HEAD_DIM
    for hv in range(GQA_KV_HEADS):
        v = x_ref[:, v0 + hv * HEAD_DIM:v0 + (hv + 1) * HEAD_DIM]
        vt_ref[hv * HEAD_DIM:(hv + 1) * HEAD_DIM, :] = v.T.astype(BF16)


def qk_prep(fr, cos, sin, q_gain, k_gain, tm=768):
    m = fr.shape[0]
    wq, wk = GQA_HEADS * HEAD_DIM, GQA_KV_HEADS * HEAD_DIM
    return pl.pallas_call(
        _qk_prep_kernel,
        grid=(m // tm,),
        in_specs=[pl.BlockSpec((tm, wq + 2 * wk), lambda i: (i, 0)),
                  pl.BlockSpec((tm, HEAD_DIM), lambda i: (i, 0)),
                  pl.BlockSpec((tm, HEAD_DIM), lambda i: (i, 0)),
                  pl.BlockSpec((1, HEAD_DIM), lambda i: (0, 0)),
                  pl.BlockSpec((1, HEAD_DIM), lambda i: (0, 0))],
        out_specs=[pl.BlockSpec((wq, tm), lambda i: (0, i)),
                   pl.BlockSpec((tm, wk), lambda i: (i, 0)),
                   pl.BlockSpec((wk, tm), lambda i: (0, i))],
        out_shape=[jax.ShapeDtypeStruct((wq, m), BF16),
                   jax.ShapeDtypeStruct((m, wk), BF16),
                   jax.ShapeDtypeStruct((wk, m), BF16)],
        compiler_params=_cparams(("arbitrary",)),
        name="qk_prep",
    )(fr, cos, sin, q_gain.reshape(1, HEAD_DIM), k_gain.reshape(1, HEAD_DIM))


def _gqa_kernel(qt_ref, k_ref, vt_ref, o_ref, *, n_ctx, n_all, tq, tk):
    qi = pl.program_id(1)
    c1 = (HEAD_DIM ** -0.5) * LOG2E
    n_ctx_tiles = n_ctx // tq
    rep = GQA_HEADS // GQA_KV_HEADS

    def attend(n_keys, tkk):
        nchunks = n_keys // tkk

        def scores(c):
            kc = k_ref[c * tkk:(c + 1) * tkk, :]
            return [jnp.dot(kc, qt_ref[r * HEAD_DIM:(r + 1) * HEAD_DIM, :], preferred_element_type=F32)
                    for r in range(rep)]

        state = [None] * rep
        nxt = scores(0)
        for c in range(nchunks):
            cur = nxt
            if c + 1 < nchunks:
                nxt = scores(c + 1)
            vtc = vt_ref[:, c * tkk:(c + 1) * tkk]
            for r in range(rep):
                st = cur[r]
                mc = jnp.max(st, axis=0, keepdims=True)
                if state[r] is None:
                    m_new = mc
                    p = jnp.exp2(st * c1 - m_new * c1)
                    l = jnp.sum(p, axis=0, keepdims=True)
                    acc = jnp.dot(vtc, p.astype(BF16), preferred_element_type=F32)
                else:
                    m, l, acc = state[r]
                    m_new = jnp.maximum(m, mc)
                    p = jnp.exp2(st * c1 - m_new * c1)
                    alpha = jnp.exp2((m - m_new) * c1)
                    l = alpha * l + jnp.sum(p, axis=0, keepdims=True)
                    acc = alpha * acc + jnp.dot(vtc, p.astype(BF16), preferred_element_type=F32)
                state[r] = (m_new, l, acc)
        for r in range(rep):
            _, l, acc = state[r]
            o_ref[:, r * HEAD_DIM:(r + 1) * HEAD_DIM] = (acc / l).T.astype(o_ref.dtype)

    @pl.when(qi < n_ctx_tiles)
    def _ctx():
        attend(n_ctx, n_ctx)

    @pl.when(qi >= n_ctx_tiles)
    def _lat():
        attend(n_all, tk)


def gqa_attention(qt, kn, vt, n_ctx, tq=256, tk=1408):
    m = kn.shape[0]
    rep = GQA_HEADS // GQA_KV_HEADS
    if m % tk:
        tk = 256
    return pl.pallas_call(
        functools.partial(_gqa_kernel, n_ctx=n_ctx, n_all=m, tq=tq, tk=tk),
        grid=(GQA_KV_HEADS, m // tq),
        in_specs=[pl.BlockSpec((rep * HEAD_DIM, tq), lambda g, i: (g, i)),
                  pl.BlockSpec((m, HEAD_DIM), lambda g, i: (0, g)),
                  pl.BlockSpec((HEAD_DIM, m), lambda g, i: (g, 0))],
        out_specs=pl.BlockSpec((tq, rep * HEAD_DIM), lambda g, i: (i, g)),
        out_shape=jax.ShapeDtypeStruct((m, GQA_HEADS * HEAD_DIM), BF16),
        compiler_params=_cparams(("arbitrary", "arbitrary")),
        name="gqa_attn",
    )(qt, kn, vt)


def _ssd_prep_kernel(prev_ref, x_ref, next_ref, dtr_ref, cw_ref, cb_ref, dtb_ref, u_ref, dt_ref, *, n_ctx, n_all, tm):
    i = pl.program_id(0)
    lo = i * tm
    hi = lo + tm
    top_ok = jnp.logical_and(lo != 0, lo != n_ctx)
    bot_ok = jnp.logical_and(hi != n_ctx, hi != n_all)
    prev = jnp.where(top_ok, prev_ref[...], 0.0)
    nxt = jnp.where(bot_ok, next_ref[...], 0.0)
    ext = jnp.concatenate([prev, x_ref[...], nxt], axis=0)
    half = SSD_CONV // 2
    acc = None
    for j in range(SSD_CONV):
        sl = ext[8 - half + j:8 - half + j + tm, :]
        term = sl * cw_ref[j:j + 1, :]
        acc = term if acc is None else acc + term
    u_ref[...] = _silu(acc + cb_ref[...])
    dt_ref[...] = jax.nn.softplus(dtr_ref[...] + dtb_ref[...])


def ssd_prep(fzx, fr, conv_w, conv_b, dt_bias, n_ctx, tm=256):
    m = fzx.shape[0]
    nb8 = tm // 8
    last8 = m // 8 - 1
    cw = jnp.concatenate([conv_w, jnp.zeros((8 - SSD_CONV, SSD_XBC_W), F32)], axis=0)
    dtb = jnp.concatenate([dt_bias.reshape(-1), jnp.zeros((LANES - 2 * SSD_HEADS,), F32)]).reshape(1, LANES)
    return pl.pallas_call(
        functools.partial(_ssd_prep_kernel, n_ctx=n_ctx, n_all=m, tm=tm),
        grid=(m // tm,),
        in_specs=[pl.BlockSpec((8, SSD_XBC_W), lambda i: (jnp.maximum(i * nb8 - 1, 0), 0)),
                  pl.BlockSpec((tm, SSD_XBC_W), lambda i: (i, 0)),
                  pl.BlockSpec((8, SSD_XBC_W), lambda i: (jnp.minimum((i + 1) * nb8, last8), 0)),
                  pl.BlockSpec((tm, LANES), lambda i: (i, 0)),
                  pl.BlockSpec((8, SSD_XBC_W), lambda i: (0, 0)),
                  pl.BlockSpec((1, SSD_XBC_W), lambda i: (0, 0)),
                  pl.BlockSpec((1, LANES), lambda i: (0, 0))],
        out_specs=[pl.BlockSpec((tm, SSD_XBC_W), lambda i: (i, 0)),
                   pl.BlockSpec((tm, LANES), lambda i: (i, 0))],
        out_shape=[jax.ShapeDtypeStruct((m, SSD_XBC_W), F32),
                   jax.ShapeDtypeStruct((m, LANES), F32)],
        compiler_params=_cparams(("arbitrary",)),
        name="ssd_prep",
    )(fzx, fzx, fzx, fr, cw, conv_b.reshape(1, SSD_XBC_W), dtb)


def _ssd_chunk(u_ref, dt_ref, alog_ref, y_ref, ht_ref, d):
    ln = SSD_CHUNK
    p = SSD_HEAD_DIM
    ns = SSD_STATE
    epg = SSD_HEADS // SSD_GROUPS
    dt = dt_ref[...]
    if d == 1:
        dt = pltpu.roll(dt, LANES - SSD_HEADS, 1)
    a = dt * (-jnp.exp(alog_ref[d:d + 1, :]))
    li = lax.broadcasted_iota(jnp.int32, (ln, ln), 0)
    si = lax.broadcasted_iota(jnp.int32, (ln, ln), 1)
    mask = (li >= si) if d == 0 else (li <= si)
    a_cum = jnp.dot(mask.astype(F32), a, preferred_element_type=F32, precision=lax.Precision.HIGHEST)
    a_cum_t = a_cum.T
    a_tot = a_cum[ln - 1:ln, :] if d == 0 else a_cum[0:1, :]
    w_all = jnp.exp(a_tot - a_cum)
    ea_all = jnp.exp(a_cum)
    eat = jnp.exp(a_tot)
    for g in range(SSD_GROUPS):
        bg = u_ref[:, SSD_W + g * ns:SSD_W + (g + 1) * ns]
        cg = u_ref[:, SSD_W + SSD_GROUPS * ns + g * ns:SSD_W + SSD_GROUPS * ns + (g + 1) * ns]
        cgb = cg.astype(BF16)
        cb = _qk(cgb, bg.astype(BF16))
        bgt = bg.T.astype(BF16)
        for e in range(epg):
            h = g * epg + e
            ac = a_cum[:, h:h + 1]
            act = a_cum_t[h:h + 1, :]
            decay = jnp.exp(jnp.where(mask, ac - act, -jnp.inf))
            mm = (cb * decay).astype(BF16)
            xdt = u_ref[:, h * p:(h + 1) * p] * dt[:, h:h + 1]
            ht = ht_ref[d, h]
            y = (jnp.dot(mm, xdt.astype(BF16), preferred_element_type=F32)
                 + ea_all[:, h:h + 1] * jnp.dot(cgb, ht.astype(BF16), preferred_element_type=F32))
            st = jnp.dot(bgt, (xdt * w_all[:, h:h + 1]).astype(BF16), preferred_element_type=F32)
            ht_ref[d, h] = eat[:, h:h + 1] * ht + st
            y_ref[:, h * p:(h + 1) * p] = y


def _ssd_scan_kernel(uf_ref, dtf_ref, ub_ref, dtb_ref, alog_ref, yf_ref, yb_ref, ht_ref):
    @pl.when(pl.program_id(0) == 0)
    def _init():
        ht_ref[...] = jnp.zeros_like(ht_ref)

    _ssd_chunk(uf_ref, dtf_ref, alog_ref, yf_ref, ht_ref, 0)
    _ssd_chunk(ub_ref, dtb_ref, alog_ref, yb_ref, ht_ref, 1)


def ssd_scan(u, dt, a_log, n_ctx):
    m = u.shape[0]
    nc = m // SSD_CHUNK
    ncc = n_ctx // SSD_CHUNK

    def bwd_chunk(s):
        return jnp.where(s < ncc, ncc - 1 - s, ncc + nc - 1 - s)

    return pl.pallas_call(
        _ssd_scan_kernel,
        grid=(nc,),
        in_specs=[pl.BlockSpec((SSD_CHUNK, SSD_XBC_W), lambda s: (s, 0)),
                  pl.BlockSpec((SSD_CHUNK, LANES), lambda s: (s, 0)),
                  pl.BlockSpec((SSD_CHUNK, SSD_XBC_W), lambda s: (bwd_chunk(s), 0)),
                  pl.BlockSpec((SSD_CHUNK, LANES), lambda s: (bwd_chunk(s), 0)),
                  pl.BlockSpec((2, LANES), lambda s: (0, 0))],
        out_specs=[pl.BlockSpec((SSD_CHUNK, SSD_W), lambda s: (s, 0)),
                   pl.BlockSpec((SSD_CHUNK, SSD_W), lambda s: (bwd_chunk(s), 0))],
        out_shape=[jax.ShapeDtypeStruct((m, SSD_W), F32), jax.ShapeDtypeStruct((m, SSD_W), F32)],
        scratch_shapes=[pltpu.VMEM((2, SSD_HEADS, SSD_STATE, SSD_HEAD_DIM), F32)],
        compiler_params=_cparams(("arbitrary",)),
        name="ssd_scan",
    )(u, dt, u, dt, jnp.concatenate([a_log, jnp.zeros((2, LANES - SSD_HEADS), F32)], axis=1))


def _ssd_out_kernel(yf_ref, yb_ref, xs_ref, z_ref, dsk_ref, nw_ref, o_ref):
    y = yf_ref[...] + yb_ref[...] + dsk_ref[...] * xs_ref[...]
    gated = y * _silu(z_ref[...])
    gw = SSD_W // SSD_GROUPS
    for g in range(SSD_GROUPS):
        blk = gated[:, g * gw:(g + 1) * gw]
        nrm = blk * lax.rsqrt(jnp.mean(blk * blk, axis=-1, keepdims=True) + RMS_EPS)
        o_ref[:, g * gw:(g + 1) * gw] = (nrm * nw_ref[:, g * gw:(g + 1) * gw]).astype(o_ref.dtype)


def ssd_out(yf, yb, u, fb, d_skip, norm_w, tm=768):
    m = u.shape[0]
    dsk = jnp.repeat(d_skip.astype(F32), SSD_HEAD_DIM).reshape(1, SSD_W)
    return pl.pallas_call(
        _ssd_out_kernel,
        grid=(m // tm,),
        in_specs=[pl.BlockSpec((tm, SSD_W), lambda i: (i, 0)),
                  pl.BlockSpec((tm, SSD_W), lambda i: (i, 0)),
                  pl.BlockSpec((tm, SSD_W), lambda i: (i, 0)),
                  pl.BlockSpec((tm, SSD_W), lambda i: (i, ZX_Z // SSD_W)),
                  pl.BlockSpec((1, SSD_W), lambda i: (0, 0)),
                  pl.BlockSpec((1, SSD_W), lambda i: (0, 0))],
        out_specs=pl.BlockSpec((tm, SSD_W), lambda i: (i, 0)),
        out_shape=jax.ShapeDtypeStruct((m, SSD_W), BF16),
        compiler_params=_cparams(("arbitrary",)),
        name="ssd_out",
    )(yf, yb, u, fb, dsk, norm_w.reshape(1, SSD_W))


def _dft_tables(n):
    ang = 2.0 * np.pi * np.outer(np.arange(n), np.arange(n)) / n
    return jnp.asarray(np.cos(ang), F32), jnp.asarray(np.sin(ang), F32)


def _split_bf16(a):
    hi = a.astype(BF16)
    return hi, (a - hi.astype(F32)).astype(BF16)


def _hdot(a, b):
    ah, al = _split_bf16(a)
    bh, bl = _split_bf16(b)
    return (jnp.dot(ah, bh, preferred_element_type=F32)
            + (jnp.dot(ah, bl, preferred_element_type=F32) + jnp.dot(al, bh, preferred_element_type=F32)))


def _fnet_a_kernel(c_ref, s_ref, x_ref, yr_ref, yi_ref):
    x = x_ref[...]
    yr_ref[...] = _hdot(c_ref[...], x)
    yi_ref[...] = -_hdot(s_ref[...], x)


def _fnet_b_kernel(yr_ref, yi_ref, twc_ref, tws_ref, c1_ref, s1_ref, cc_ref, sc_ref, o_ref, *, scale, kb):
    c1, s1 = c1_ref[...], s1_ref[...]
    cc, sc = cc_ref[...], sc_ref[...]
    for j in range(kb):
        yr, yi = yr_ref[j], yi_ref[j]
        tc, ts = twc_ref[j], tws_ref[j]
        pr = yr * tc + yi * ts
        pi = yi * tc - yr * ts
        zr = _hdot(c1, pr) + _hdot(s1, pi)
        zi = _hdot(c1, pi) - _hdot(s1, pr)
        for g in range(FNET_GROUPS):
            cs = slice(g * FNET_GROUP_DIM, (g + 1) * FNET_GROUP_DIM)
            o_ref[:, j, cs] = (_hdot(zr[:, cs], cc) + _hdot(zi[:, cs], sc)) * scale


def _fnet_ctx_kernel(x_ref, cn_ref, sn_ref, cc_ref, sc_ref, o_ref, *, scale):
    x = x_ref[...]
    wr = _hdot(cn_ref[...], x)
    ws = _hdot(sn_ref[...], x)
    cc, sc = cc_ref[...], sc_ref[...]
    for g in range(FNET_GROUPS):
        cs = slice(g * FNET_GROUP_DIM, (g + 1) * FNET_GROUP_DIM)
        o_ref[:, cs] = (_hdot(wr[:, cs], cc) - _hdot(ws[:, cs], sc)) * scale


def fourier_mix(f_lat, f_ctx):
    n_lat, w = f_lat.shape
    n_ctx = f_ctx.shape[0]
    n2 = 128
    n1 = n_lat // n2
    c2, s2 = _dft_tables(n2)
    c1, s1 = _dft_tables(n1)
    cc, sc = _dft_tables(FNET_GROUP_DIM)
    tw = 2.0 * np.pi * np.outer(np.arange(n2), np.arange(n1)) / n_lat
    twc = jnp.asarray(np.cos(tw), F32).reshape(n2, n1, 1)
    tws = jnp.asarray(np.sin(tw), F32).reshape(n2, n1, 1)
    xr = f_lat.reshape(n2, n1 * w)
    tn = min(4096, n1 * w)
    yr, yi = pl.pallas_call(
        _fnet_a_kernel,
        grid=(n1 * w // tn,),
        in_specs=[pl.BlockSpec((n2, n2), lambda j: (0, 0)),
                  pl.BlockSpec((n2, n2), lambda j: (0, 0)),
                  pl.BlockSpec((n2, tn), lambda j: (0, j))],
        out_specs=[pl.BlockSpec((n2, tn), lambda j: (0, j)),
                   pl.BlockSpec((n2, tn), lambda j: (0, j))],
        out_shape=[jax.ShapeDtypeStruct((n2, n1 * w), F32)] * 2,
        compiler_params=_cparams(("arbitrary",)),
        name="fnet_stage_a",
    )(c2, s2, xr)
    kb = 8
    lat = pl.pallas_call(
        functools.partial(_fnet_b_kernel, scale=float(1.0 / math.sqrt(n_lat * FNET_GROUP_DIM)), kb=kb),
        grid=(n2 // kb,),
        in_specs=[pl.BlockSpec((kb, n1, w), lambda j: (j, 0, 0)),
                  pl.BlockSpec((kb, n1, w), lambda j: (j, 0, 0)),
                  pl.BlockSpec((kb, n1, 1), lambda j: (j, 0, 0)),
                  pl.BlockSpec((kb, n1, 1), lambda j: (j, 0, 0)),
                  pl.BlockSpec((n1, n1), lambda j: (0, 0)),
                  pl.BlockSpec((n1, n1), lambda j: (0, 0)),
                  pl.BlockSpec((FNET_GROUP_DIM, FNET_GROUP_DIM), lambda j: (0, 0)),
                  pl.BlockSpec((FNET_GROUP_DIM, FNET_GROUP_DIM), lambda j: (0, 0))],
        out_specs=pl.BlockSpec((n1, kb, w), lambda j: (0, j, 0)),
        out_shape=jax.ShapeDtypeStruct((n1, n2, w), F32),
        compiler_params=_cparams(("arbitrary",)),
        name="fnet_stage_b",
    )(yr.reshape(n2, n1, w), yi.reshape(n2, n1, w), twc, tws, c1, s1, cc, sc)
    cn, sn = _dft_tables(n_ctx)
    ctx = pl.pallas_call(
        functools.partial(_fnet_ctx_kernel, scale=float(1.0 / math.sqrt(n_ctx * FNET_GROUP_DIM))),
        out_shape=jax.ShapeDtypeStruct((n_ctx, w), F32),
        compiler_params=pltpu.CompilerParams(vmem_limit_bytes=VMEM_LIMIT_BYTES),
        name="fnet_ctx",
    )(f_ctx, cn, sn, cc, sc)
    return jnp.concatenate([ctx, lat.reshape(n_lat, w)], axis=0)


def _merge_kernel(h_ref, *refs, shift, tn):
    wa_refs, wb_refs, o_refs = refs[0:4], refs[4:8], refs[8:12]
    wbr_ref, y_ref, ws_ref = refs[12], refs[13], refs[14]

    @pl.when(pl.program_id(1) == 0)
    def _realign():
        for b in range(N_BRANCH):
            wcat = jnp.concatenate([wa_refs[b][0], wb_refs[b][0]], axis=1)
            ws_ref[b] = wcat[:, shift:shift + tn].astype(BF16)

    h = h_ref[...]
    y = None
    for b in range(N_BRANCH):
        g = jnp.dot(h, ws_ref[b], preferred_element_type=F32)
        pr = jnp.dot(o_refs[b][...].astype(BF16), wbr_ref[0, b].astype(BF16), preferred_element_type=F32)
        t = jax.nn.sigmoid(g) * pr
        y = t if y is None else y + t
    y_ref[...] = y.astype(y_ref.dtype)


def merge(h, w_in_all, gate_col0, branches, w_branch_all, layer, tm=768, tn=256):
    m, k = h.shape
    nb = D_MODEL // tn
    base = (gate_col0 // LANES) * LANES
    shift = gate_col0 - base
    assert base % tn == 0 and 0 < shift < LANES
    r = tn // LANES
    c0 = base // tn

    def wa_map(j, i, b):
        return (layer, 0, c0 + b * nb + j)

    def wb_map(j, i, b):
        return (layer, 0, (c0 + b * nb + j + 1) * r)

    wa_specs = [pl.BlockSpec((1, k, tn), functools.partial(wa_map, b=b)) for b in range(N_BRANCH)]
    wb_specs = [pl.BlockSpec((1, k, LANES), functools.partial(wb_map, b=b)) for b in range(N_BRANCH)]
    o_specs = [pl.BlockSpec((tm, BRANCH_W), lambda j, i: (i, 0)) for _ in range(N_BRANCH)]
    return pl.pallas_call(
        functools.partial(_merge_kernel, shift=shift, tn=tn),
        grid=(nb, m // tm),
        in_specs=([pl.BlockSpec((tm, k), lambda j, i: (i, 0))] + wa_specs + wb_specs + o_specs
                  + [pl.BlockSpec((1, N_BRANCH, BRANCH_W, tn), lambda j, i: (layer, 0, 0, j))]),
        out_specs=pl.BlockSpec((tm, tn), lambda j, i: (i, j)),
        out_shape=jax.ShapeDtypeStruct((m, D_MODEL), BF16),
        scratch_shapes=[pltpu.VMEM((N_BRANCH, k, tn), BF16)],
        compiler_params=_cparams(("arbitrary", "arbitrary")),
        name="merge",
    )(h, *([w_in_all] * (2 * N_BRANCH)), *branches, w_branch_all)


def _expert_kernel(idx_ref, h_hbm, wg_ref, wu_ref, wd_ref, gate_ref, o_ref, xs_ref, sem, *, nf):
    e = pl.program_id(0)
    f = pl.program_id(1)
    cap = xs_ref.shape[1]
    slot = e % 2

    def start_gather(ee, sl):
        def issue(c, carry):
            row = idx_ref[ee, c]
            pltpu.make_async_copy(h_hbm.at[pl.ds(row, 1)], xs_ref.at[sl, pl.ds(c, 1)], sem.at[sl]).start()
            return carry

        lax.fori_loop(0, cap, issue, 0, unroll=8)

    @pl.when(jnp.logical_and(e == 0, f == 0))
    def _cold_start():
        start_gather(0, 0)

    @pl.when(f == 0)
    def _wait_rows():
        pltpu.make_async_copy(h_hbm.at[pl.ds(0, cap)], xs_ref.at[slot], sem.at[slot]).wait()

    @pl.when(jnp.logical_and(f == 0, e + 1 < pl.num_programs(0)))
    def _prefetch_next():
        start_gather(e + 1, 1 - slot)

    xs = xs_ref[slot].astype(BF16)
    hg = jnp.dot(xs, wg_ref[0, 0].astype(BF16), preferred_element_type=F32)
    hu = jnp.dot(xs, wu_ref[0, 0].astype(BF16), preferred_element_type=F32)
    hid = (_silu(hg) * hu).astype(BF16)
    part = jnp.dot(hid, wd_ref[0, 0].astype(BF16), preferred_element_type=F32)

    @pl.when(f == 0)
    def _first():
        o_ref[0] = part

    @pl.when(f != 0)
    def _rest():
        o_ref[0] += part

    @pl.when(f == nf - 1)
    def _done():
        o_ref[0] = o_ref[0] * gate_ref[0]


def expert_ffn(h2, idx, gate, w_gate, w_up, w_down, layer, tf=256):
    ne, cap = idx.shape
    d = h2.shape[1]
    ff = w_gate.shape[3]
    grid_spec = pltpu.PrefetchScalarGridSpec(
        num_scalar_prefetch=1,
        grid=(ne, ff // tf),
        in_specs=[pl.BlockSpec(memory_space=pl.ANY),
                  pl.BlockSpec((1, 1, d, tf), lambda e, f, ix: (layer, e, 0, f)),
                  pl.BlockSpec((1, 1, d, tf), lambda e, f, ix: (layer, e, 0, f)),
                  pl.BlockSpec((1, 1, tf, d), lambda e, f, ix: (layer, e, f, 0)),
                  pl.BlockSpec((1, cap, 1), lambda e, f, ix: (e, 0, 0))],
        out_specs=pl.BlockSpec((1, cap, d), lambda e, f, ix: (e, 0, 0)),
        scratch_shapes=[pltpu.VMEM((2, cap, d), F32), pltpu.SemaphoreType.DMA((2,))],
    )
    return pl.pallas_call(
        functools.partial(_expert_kernel, nf=ff // tf),
        grid_spec=grid_spec,
        out_shape=jax.ShapeDtypeStruct((ne, cap, d), F32),
        compiler_params=_cparams(("arbitrary", "arbitrary")),
        name="expert_ffn",
    )(idx, h2, w_gate, w_up, w_down, gate.reshape(ne, cap, 1))


COMBINE_TOKENS = 64


def _combine_kernel(src_ref, dst_ref, tstart_ref, rmax_ref, ye_hbm, x_ref, g_ref, o_ref, s_ref, sem, *, n_ctx):
    t = pl.program_id(0)
    tt = COMBINE_TOKENS
    p0 = tstart_ref[t]
    p1 = tstart_ref[t + 1]
    nplanes = rmax_ref[t]

    def zero(r, carry):
        s_ref[pl.ds(pl.multiple_of(r * tt, tt), tt), :] = jnp.zeros((tt, s_ref.shape[1]), F32)
        return carry

    lax.fori_loop(0, nplanes, zero, 0)

    def issue(p, carry):
        pltpu.make_async_copy(ye_hbm.at[pl.ds(src_ref[p], 1)], s_ref.at[pl.ds(dst_ref[p], 1)], sem).start()
        return carry

    lax.fori_loop(p0, p1, issue, 0)

    def wait(p, carry):
        pltpu.make_async_copy(ye_hbm.at[pl.ds(0, 1)], s_ref.at[pl.ds(0, 1)], sem).wait()
        return carry

    lax.fori_loop(p0, p1, wait, 0)

    def add(r, acc):
        return acc + s_ref[pl.ds(pl.multiple_of(r * tt, tt), tt), :]

    acc = lax.fori_loop(0, nplanes, add, jnp.zeros((tt, s_ref.shape[1]), F32))
    row = t * tt + lax.broadcasted_iota(jnp.int32, (tt, 1), 0)
    g = jnp.where(row < n_ctx, g_ref[1:2, :], g_ref[0:1, :])
    o_ref[...] = x_ref[...] + g * acc


def moe_combine(x, ye, idx, mod, gate_chunk, n_ctx):
    m, d = x.shape
    ne, cap = idx.shape
    npairs = ne * cap
    tt = COMBINE_TOKENS
    ntiles = m // tt
    tok = idx.reshape(-1)
    tok_s, src_s = lax.sort((tok, jnp.arange(npairs, dtype=jnp.int32)), num_keys=1)
    pos = jnp.arange(npairs, dtype=jnp.int32)
    is_first = jnp.concatenate([jnp.ones((1,), bool), tok_s[1:] != tok_s[:-1]])
    first = lax.cummax(jnp.where(is_first, pos, 0), axis=0)
    rank = pos - first
    dst = rank * tt + tok_s % tt
    tile_s = tok_s // tt
    tiles = jnp.arange(ntiles + 1, dtype=jnp.int32)
    tstart = jnp.sum((tile_s[None, :] < tiles[:, None]).astype(jnp.int32), axis=1)
    rmax = jnp.max(jnp.where(tile_s[None, :] == tiles[:ntiles, None], rank[None, :] + 1, 0), axis=1)
    grid_spec = pltpu.PrefetchScalarGridSpec(
        num_scalar_prefetch=4,
        grid=(ntiles,),
        in_specs=[pl.BlockSpec(memory_space=pl.ANY),
                  pl.BlockSpec((tt, d), lambda t, *_: (t, 0)),
                  pl.BlockSpec((8, d), lambda t, *_: (0, gate_chunk))],
        out_specs=pl.BlockSpec((tt, d), lambda t, *_: (t, 0)),
        scratch_shapes=[pltpu.VMEM((ne * tt, d), F32), pltpu.SemaphoreType.DMA(())],
    )
    return pl.pallas_call(
        functools.partial(_combine_kernel, n_ctx=n_ctx),
        grid_spec=grid_spec,
        out_shape=jax.ShapeDtypeStruct((m, d), F32),
        compiler_params=_cparams(("arbitrary",)),
        name="moe_combine",
    )(src_s, dst, tstart, rmax, ye.reshape(npairs, d), x, mod)


def moe(x, h2, aff, mod, w_gate, w_up, w_down, layer, n_ctx):
    m = h2.shape[0]
    n_lat = m - n_ctx
    a = aff[:, :N_EXPERTS]
    g_c, i_c = lax.top_k(a[:n_ctx].T, CAPACITY_FACTOR * n_ctx // N_EXPERTS)
    g_l, i_l = lax.top_k(a[n_ctx:].T, CAPACITY_FACTOR * n_lat // N_EXPERTS)
    idx = jnp.concatenate([i_c, i_l + n_ctx], axis=1).astype(jnp.int32)
    gate = jnp.concatenate([g_c, g_l], axis=1)
    ye = expert_ffn(h2, idx, gate, w_gate, w_up, w_down, layer)
    return moe_combine(x, ye, idx, mod, 5, n_ctx)


_P_NA = 0
_P_Z = _P_NA + 3 * NA_HEADS * HEAD_DIM
_P_XBC = _P_Z + SSD_W
_P_DT = _P_XBC + SSD_XBC_W
_P_GQ = _P_DT + 2 * SSD_HEADS
_P_FN = _P_GQ + (GQA_HEADS + 2 * GQA_KV_HEADS) * HEAD_DIM
_P_GATES = _P_FN + FNET_GROUPS * FNET_GROUP_DIM


def _layer(x, mod, n_ctx, layer, norm_mix, norm_ffn, w_in_all, na_rpb, conv_w, conv_b, a_log, dt_bias, d_skip,
           ssd_gn, q_gain, k_gain, w_branch_all, w_out_all, w_router, w_gate_all, w_up_all, w_down_all, rope):
    m = x.shape[0]
    tm = 768 if m % 768 == 0 else 256
    h = norm_mod(x, norm_mix, mod, n_ctx, 0, tm=tm)
    ab = matmul_cols(h, w_in_all, layer, _P_NA, _P_Z - _P_NA, BF16, tm, 768, "proj_na")
    fzx = matmul_cols(h, w_in_all, layer, _P_Z, _P_DT - _P_Z, F32, tm, 512, "proj_zx", out_perm=(2, 0, 1))
    fr = matmul_cols_unaligned(h, w_in_all, layer, _P_GQ, _P_GATES - _P_GQ, F32, tm, 512, "proj_rest")
    fdt = matmul_cols(h, w_in_all, layer, _P_DT, LANES, F32, tm, LANES, "proj_dt")

    rows = (m - n_ctx) // GRID_W
    bias, tid = na_bias_table(na_rpb, rows)
    o_na = na_attention(ab, bias, tid, n_ctx)

    u, dt = ssd_prep(fzx, fdt, conv_w, conv_b, dt_bias, n_ctx)
    yf, yb = ssd_scan(u, dt, a_log, n_ctx)
    o_ssd = ssd_out(yf, yb, u, fzx, d_skip, ssd_gn, tm=tm)

    qt, kn, vt = qk_prep(fr, rope[0], rope[1], q_gain, k_gain, tm=tm)
    o_gqa = gqa_attention(qt, kn, vt, n_ctx)

    f_in = fr[:, FR_FN:FR_FN + FNET_GROUPS * FNET_GROUP_DIM]
    o_fn = fourier_mix(f_in[n_ctx:], f_in[:n_ctx])

    y = merge(h, w_in_all, _P_GATES, (o_na, o_ssd, o_gqa, o_fn), w_branch_all, layer, tm=tm)
    x = matmul_residual(y, w_out_all, layer, x, mod, 2, n_ctx, tm, 1024, "out_proj")

    wr = jnp.concatenate([w_router, jnp.zeros((D_MODEL, LANES - N_EXPERTS), w_router.dtype)], axis=1)
    h2, aff = norm_router(x, norm_ffn, mod, wr, n_ctx, 3, tm=tm)
    return moe(x, h2, aff, mod, w_gate_all, w_up_all, w_down_all, layer, n_ctx)


def kernel(x, c, ctx, c_ctx, w_ada, b_ada, norm_mix, norm_ffn, w_in, na_rpb, ssd_conv_w, ssd_conv_b, ssd_a_log,
           ssd_dt_bias, ssd_d, ssd_norm, gqa_q_norm, gqa_k_norm, w_branch, w_out, w_router, moe_w_gate, moe_w_up,
           moe_w_down, final_norm_w):
    n_lat = x.shape[1]
    n_ctx = ctx.shape[1]
    depth = w_ada.shape[0]
    cc = jnp.concatenate([c[0:1], c_ctx[None], jnp.zeros((6, D_MODEL), F32)], axis=0)
    mods = ada_all(cc, w_ada, b_ada)
    rope = rope_tables(n_ctx, n_lat)
    xs = jnp.concatenate([ctx[0], x[0]], axis=0)
    for l in range(depth):
        xs = _layer(xs, mods[l], n_ctx, l, norm_mix[l], norm_ffn[l], w_in, na_rpb[l], ssd_conv_w[l],
                    ssd_conv_b[l], ssd_a_log[l], ssd_dt_bias[l], ssd_d[l], ssd_norm[l], gqa_q_norm[l],
                    gqa_k_norm[l], w_branch, w_out, w_router[l], moe_w_gate, moe_w_up, moe_w_down, rope)
    out = final_norm(xs, final_norm_w, n_ctx, n_lat)
    return out[None]
```

```python
import functools
import math

import numpy as np
import jax
import jax.numpy as jnp
from jax import lax
from jax.experimental import pallas as pl
from jax.experimental.pallas import tpu as pltpu

F32 = jnp.float32
BF16 = jnp.bfloat16

D_MODEL = 2048
GRID_W = 64
HEAD_DIM = 128
RMS_EPS = 1e-6
N_BRANCH = 4
BRANCH_W = 512
NA_HEADS = 4
NA_WIN_R = 8
NA_WIN_C = 16
SSD_HEADS = 8
SSD_HEAD_DIM = 64
SSD_GROUPS = 2
SSD_STATE = 128
SSD_CONV = 5
SSD_CHUNK = 128
SSD_W = SSD_HEADS * SSD_HEAD_DIM
SSD_XBC_W = SSD_W + 2 * SSD_GROUPS * SSD_STATE
GQA_HEADS = 4
GQA_KV_HEADS = 2
ROPE_THETA = 10000.0
FNET_GROUPS = 4
FNET_GROUP_DIM = 128
N_EXPERTS = 16
EXPERT_FF = D_MODEL // 2
CAPACITY_FACTOR = 2
LOG2E = 1.4426950408889634

VMEM_LIMIT_BYTES = 56 * 1024 * 1024
LANES = 128

ZX_XBC, ZX_Z, ZX_W = 0, 1024, 1536
FR_Q, FR_K, FR_V, FR_FN, FR_W = 0, 512, 768, 1024, 1536


def _cparams(sem):
    return pltpu.CompilerParams(dimension_semantics=sem, vmem_limit_bytes=VMEM_LIMIT_BYTES)


def _silu(x):
    return x * jax.nn.sigmoid(x)


def _ada_kernel(ct_ref, w_ref, b_ref, o_ref):
    a = _silu(ct_ref[...])
    w = w_ref[0]
    rows = [jnp.sum(w * a[:, r:r + 1], axis=0, keepdims=True) + b_ref[0] for r in range(2)]
    o_ref[0] = jnp.concatenate(rows + [jnp.zeros((6, w.shape[1]), F32)], axis=0)


def ada_all(cc, w_ada, b_ada):
    depth, d, n = w_ada.shape
    tn = 1024
    return pl.pallas_call(
        _ada_kernel,
        grid=(depth, n // tn),
        in_specs=[pl.BlockSpec((d, 8), lambda l, j: (0, 0)),
                  pl.BlockSpec((1, d, tn), lambda l, j: (l, 0, j)),
                  pl.BlockSpec((1, 1, tn), lambda l, j: (l, 0, j))],
        out_specs=pl.BlockSpec((1, 8, tn), lambda l, j: (l, 0, j)),
        out_shape=jax.ShapeDtypeStruct((depth, 8, n), F32),
        compiler_params=_cparams(("arbitrary", "arbitrary")),
        name="ada_mod",
    )(cc.T, w_ada, b_ada.reshape(depth, 1, n))


def _row_select(mod_ref, chunk, row_is_ctx):
    lo = chunk * D_MODEL
    return jnp.where(row_is_ctx, mod_ref[1:2, lo:lo + D_MODEL], mod_ref[0:1, lo:lo + D_MODEL])


def _norm_body(x_ref, nw_ref):
    x = x_ref[...]
    ms = jnp.mean(x * x, axis=-1, keepdims=True)
    return x * lax.rsqrt(ms + RMS_EPS) * nw_ref[...]


def _norm_mod_kernel(x_ref, nw_ref, mod_ref, o_ref, *, n_ctx, tm, sh_chunk):
    y = _norm_body(x_ref, nw_ref)
    row = pl.program_id(0) * tm + lax.broadcasted_iota(jnp.int32, (tm, 1), 0)
    is_ctx = row < n_ctx
    sh = _row_select(mod_ref, sh_chunk, is_ctx)
    sc = _row_select(mod_ref, sh_chunk + 1, is_ctx)
    o_ref[...] = (y * (1.0 + sc) + sh).astype(o_ref.dtype)


def norm_mod(x, nw, mod, n_ctx, sh_chunk, tm=768):
    m, d = x.shape
    return pl.pallas_call(
        functools.partial(_norm_mod_kernel, n_ctx=n_ctx, tm=tm, sh_chunk=sh_chunk),
        grid=(m // tm,),
        in_specs=[pl.BlockSpec((tm, d), lambda i: (i, 0)),
                  pl.BlockSpec((1, d), lambda i: (0, 0)),
                  pl.BlockSpec(mod.shape, lambda i: (0, 0))],
        out_specs=pl.BlockSpec((tm, d), lambda i: (i, 0)),
        out_shape=jax.ShapeDtypeStruct((m, d), BF16),
        compiler_params=_cparams(("arbitrary",)),
        name="norm_mod",
    )(x, nw.reshape(1, d), mod)


def _norm_router_kernel(x_ref, nw_ref, mod_ref, wr_ref, h_ref, aff_ref, *, n_ctx, tm, sh_chunk):
    y = _norm_body(x_ref, nw_ref)
    row = pl.program_id(0) * tm + lax.broadcasted_iota(jnp.int32, (tm, 1), 0)
    is_ctx = row < n_ctx
    sh = _row_select(mod_ref, sh_chunk, is_ctx)
    sc = _row_select(mod_ref, sh_chunk + 1, is_ctx)
    h = y * (1.0 + sc) + sh
    h_ref[...] = h
    logits = jnp.dot(h.astype(BF16), wr_ref[...].astype(BF16), preferred_element_type=F32)
    lane = lax.broadcasted_iota(jnp.int32, logits.shape, 1)
    logits = jnp.where(lane < N_EXPERTS, logits, -jnp.inf)
    mx = jnp.max(logits, axis=-1, keepdims=True)
    e = jnp.exp(logits - mx)
    aff_ref[...] = e / jnp.sum(e, axis=-1, keepdims=True)


def norm_router(x, nw, mod, w_router_pad, n_ctx, sh_chunk, tm=768):
    m, d = x.shape
    return pl.pallas_call(
        functools.partial(_norm_router_kernel, n_ctx=n_ctx, tm=tm, sh_chunk=sh_chunk),
        grid=(m // tm,),
        in_specs=[pl.BlockSpec((tm, d), lambda i: (i, 0)),
                  pl.BlockSpec((1, d), lambda i: (0, 0)),
                  pl.BlockSpec(mod.shape, lambda i: (0, 0)),
                  pl.BlockSpec((d, LANES), lambda i: (0, 0))],
        out_specs=[pl.BlockSpec((tm, d), lambda i: (i, 0)),
                   pl.BlockSpec((tm, LANES), lambda i: (i, 0))],
        out_shape=[jax.ShapeDtypeStruct((m, d), F32),
                   jax.ShapeDtypeStruct((m, LANES), F32)],
        compiler_params=_cparams(("arbitrary",)),
        name="norm_router",
    )(x, nw.reshape(1, d), mod, w_router_pad)


def _final_norm_kernel(x_ref, nw_ref, o_ref):
    o_ref[...] = _norm_body(x_ref, nw_ref)


def final_norm(x, nw, row0, n_rows, tm=256):
    d = x.shape[1]
    off = row0 // tm
    return pl.pallas_call(
        _final_norm_kernel,
        grid=(n_rows // tm,),
        in_specs=[pl.BlockSpec((tm, d), lambda i: (i + off, 0)),
                  pl.BlockSpec((1, d), lambda i: (0, 0))],
        out_specs=pl.BlockSpec((tm, d), lambda i: (i, 0)),
        out_shape=jax.ShapeDtypeStruct((n_rows, d), F32),
        compiler_params=_cparams(("arbitrary",)),
        name="final_norm",
    )(x, nw.reshape(1, d))


_NT = (((1,), (1,)), ((), ()))


def _wt_block(layer, row0, tn, k):
    return pl.BlockSpec((pl.Element(1), pl.Element(tn), pl.Element(k)),
                        lambda j, i: (layer, pl.multiple_of(row0 + j * tn, 16), 0))


def _mm_wt_kernel(a_ref, wt_ref, o_ref):
    w = wt_ref[0].astype(BF16)
    o_ref[...] = lax.dot_general(a_ref[...], w, _NT, preferred_element_type=F32).astype(o_ref.dtype)


def matmul_wt(a, wt_all, layer, col0, ncols, out_dtype, tm, tn, name, out_perm=None):
    m, k = a.shape
    nb = ncols // tn
    perm = tuple(range(nb)) if out_perm is None else tuple(out_perm)

    def out_map(j, i):
        pj = j
        for src, dst in enumerate(perm):
            pj = jnp.where(j == src, dst, pj)
        return (i, pj)

    return pl.pallas_call(
        _mm_wt_kernel,
        grid=(nb, m // tm),
        in_specs=[pl.BlockSpec((tm, k), lambda j, i: (i, 0)), _wt_block(layer, col0, tn, k)],
        out_specs=pl.BlockSpec((tm, tn), out_map),
        out_shape=jax.ShapeDtypeStruct((m, ncols), out_dtype),
        compiler_params=_cparams(("arbitrary", "arbitrary")),
        name=name,
    )(a, wt_all)


def _mm_residual_kernel(a_ref, w_ref, x_ref, g_ref, o_ref, *, n_ctx, tm):
    acc = jnp.dot(a_ref[...].astype(BF16), w_ref[0].astype(BF16), preferred_element_type=F32)
    row = pl.program_id(1) * tm + lax.broadcasted_iota(jnp.int32, (tm, 1), 0)
    g = jnp.where(row < n_ctx, g_ref[1:2, :], g_ref[0:1, :])
    o_ref[...] = x_ref[...] + g * acc


def matmul_residual(a, w_all, layer, x, mod, gate_chunk, n_ctx, tm, tn, name):
    m, k = a.shape
    n = w_all.shape[2]
    goff = gate_chunk * D_MODEL // tn
    return pl.pallas_call(
        functools.partial(_mm_residual_kernel, n_ctx=n_ctx, tm=tm),
        grid=(n // tn, m // tm),
        in_specs=[pl.BlockSpec((tm, k), lambda j, i: (i, 0)),
                  pl.BlockSpec((1, k, tn), lambda j, i: (layer, 0, j)),
                  pl.BlockSpec((tm, tn), lambda j, i: (i, j)),
                  pl.BlockSpec((8, tn), lambda j, i: (0, goff + j))],
        out_specs=pl.BlockSpec((tm, tn), lambda j, i: (i, j)),
        out_shape=jax.ShapeDtypeStruct((m, n), F32),
        compiler_params=_cparams(("arbitrary", "arbitrary")),
        name=name,
    )(a, w_all, x, mod)


def _softmax_pv(parts):
    m = None
    for s, _ in parts:
        mi = jnp.max(s, axis=-1, keepdims=True)
        m = mi if m is None else jnp.maximum(m, mi)
    ps, l = [], None
    for s, _ in parts:
        p = jnp.exp(s - m)
        ps.append(p)
        li = jnp.sum(p, axis=-1, keepdims=True)
        l = li if l is None else l + li
    inv = 1.0 / l
    o = None
    for p, (_, v) in zip(ps, parts):
        oi = jnp.dot((p * inv).astype(BF16), v, preferred_element_type=F32)
        o = oi if o is None else o + oi
    return o


def _qk(q, k):
    return lax.dot_general(q, k, (((1,), (1,)), ((), ())), preferred_element_type=F32)


NA_QROWS = 4
NA_KROWS = NA_WIN_R + NA_QROWS - 1


def _na_kernel(tid_ref, q_ref, k_ref, v_ref, b_ref, o_ref, *, n_ctx, rows):
    del tid_ref
    step = pl.program_id(0)
    tq = NA_QROWS * GRID_W
    n_ctx_steps = n_ctx // tq
    scale = HEAD_DIM ** -0.5
    win = NA_KROWS * GRID_W

    @pl.when(step < n_ctx_steps)
    def _ctx():
        for h in range(NA_HEADS):
            cs = slice(h * HEAD_DIM, (h + 1) * HEAD_DIM)
            q = q_ref[:, cs]
            s_c = _qk(q, k_ref[0:n_ctx, cs]) * scale
            o_ref[:, cs] = _softmax_pv([(s_c, v_ref[0:n_ctx, cs])]).astype(o_ref.dtype)

    @pl.when(step >= n_ctx_steps)
    def _lat():
        p = step - n_ctx_steps
        row0 = jnp.clip(NA_QROWS * p - NA_WIN_R // 2, 0, rows - NA_KROWS)
        base = pl.multiple_of(n_ctx + row0 * GRID_W, GRID_W)
        for h in range(NA_HEADS):
            cs = slice(h * HEAD_DIM, (h + 1) * HEAD_DIM)
            q = q_ref[:, cs]
            s_w = _qk(q, k_ref[pl.ds(base, win), cs]) * scale + b_ref[0, h]
            s_c = _qk(q, k_ref[0:n_ctx, cs]) * scale
            o = _softmax_pv([(s_w, v_ref[pl.ds(base, win), cs]), (s_c, v_ref[0:n_ctx, cs])])
            o_ref[:, cs] = o.astype(o_ref.dtype)


def _na_patterns(rows):
    a = np.arange(NA_QROWS)[:, None]
    i = np.arange(NA_KROWS)[None, :]
    pats, keys, tid = [], [], []
    for p in range(rows // NA_QROWS):
        row0 = int(np.clip(NA_QROWS * p - NA_WIN_R // 2, 0, rows - NA_KROWS))
        r = NA_QROWS * p + a
        rs = np.clip(r - NA_WIN_R // 2, 0, rows - NA_WIN_R)
        krow = row0 + i
        valid = (krow >= rs) & (krow < rs + NA_WIN_R)
        ridx = np.clip(krow - r + NA_WIN_R - 1, 0, 2 * NA_WIN_R - 2)
        key = (valid.tobytes(), ridx.tobytes())
        if key not in keys:
            keys.append(key)
            pats.append((valid, ridx))
        tid.append(keys.index(key))
    return np.stack([v for v, _ in pats]), np.stack([x for _, x in pats]), np.asarray(tid, np.int32)


def na_bias_table(rpb, rows):
    valid_r, ridx, tid = _na_patterns(rows)
    kc = np.arange(GRID_W)[None, :]
    qc = np.arange(GRID_W)[:, None]
    col_start = np.clip(qc - NA_WIN_C // 2, 0, GRID_W - NA_WIN_C)
    valid_c = (kc >= col_start) & (kc < col_start + NA_WIN_C)
    cidx = np.clip(kc - qc + NA_WIN_C - 1, 0, 2 * NA_WIN_C - 2)
    nh, nbr, nbc = rpb.shape
    onehot = (np.arange(nbc)[:, None, None] == cidx[None]) & valid_c[None]
    toe = jnp.einsum('hbj,jqk->hbqk', rpb.astype(F32), jnp.asarray(onehot, F32), precision=lax.Precision.HIGHEST)
    toe = toe + jnp.asarray(np.where(valid_c, 0.0, -np.inf), F32)
    neg = jnp.full((nh, GRID_W, GRID_W), -jnp.inf, F32)
    npat = valid_r.shape[0]
    blocks = [toe[:, int(ridx[p, a, i])] if valid_r[p, a, i] else neg
              for p in range(npat) for a in range(NA_QROWS) for i in range(NA_KROWS)]
    t = jnp.stack(blocks, axis=0).reshape(npat, NA_QROWS, NA_KROWS, nh, GRID_W, GRID_W)
    t = jnp.transpose(t, (0, 3, 1, 4, 2, 5))
    return t.reshape(npat, nh, NA_QROWS * GRID_W, NA_KROWS * GRID_W), jnp.asarray(tid)


def na_attention(ab, bias, tid, n_ctx):
    m = ab.shape[0]
    rows = (m - n_ctx) // GRID_W
    tq = NA_QROWS * GRID_W
    n_ctx_steps = n_ctx // tq
    w = NA_HEADS * HEAD_DIM
    grid_spec = pltpu.PrefetchScalarGridSpec(
        num_scalar_prefetch=1,
        grid=(m // tq,),
        in_specs=[pl.BlockSpec((tq, w), lambda s, t: (s, 0)),
                  pl.BlockSpec((m, w), lambda s, t: (0, 1)),
                  pl.BlockSpec((m, w), lambda s, t: (0, 2)),
                  pl.BlockSpec((1, NA_HEADS, tq, NA_KROWS * GRID_W),
                               lambda s, t: (t[jnp.maximum(s - n_ctx_steps, 0)], 0, 0, 0))],
        out_specs=pl.BlockSpec((tq, w), lambda s, t: (s, 0)),
    )
    return pl.pallas_call(
        functools.partial(_na_kernel, n_ctx=n_ctx, rows=rows),
        grid_spec=grid_spec,
        out_shape=jax.ShapeDtypeStruct((m, w), BF16),
        compiler_params=_cparams(("arbitrary",)),
        name="na_attn",
    )(tid, ab, ab, ab, bias)


def rope_tables(n_ctx, n_lat):
    t = jnp.arange(n_lat)
    pos = jnp.stack([t // GRID_W, t % GRID_W], axis=-1).astype(F32)
    n_freq = HEAD_DIM // 4
    inv = ROPE_THETA ** (-jnp.arange(n_freq, dtype=F32) / n_freq)
    ang = pos[:, :, None] * inv
    c, s = jnp.cos(ang), jnp.sin(ang)
    cos = jnp.concatenate([c[:, 0], c[:, 0], c[:, 1], c[:, 1]], axis=-1)
    sin = jnp.concatenate([-s[:, 0], s[:, 0], -s[:, 1], s[:, 1]], axis=-1)
    cos = jnp.concatenate([jnp.ones((n_ctx, HEAD_DIM), F32), cos], axis=0)
    sin = jnp.concatenate([jnp.zeros((n_ctx, HEAD_DIM), F32), sin], axis=0)
    return cos, sin


def _qk_prep_kernel(x_ref, cos_ref, sin_ref, qg_ref, kg_ref, qt_ref, ko_ref, vt_ref):
    cos = cos_ref[...]
    sin = sin_ref[...]
    lane = lax.broadcasted_iota(jnp.int32, (1, HEAD_DIM), 1)
    first_half = (lane % (HEAD_DIM // 2)) < (HEAD_DIM // 4)
    for h in range(GQA_HEADS + GQA_KV_HEADS):
        x = x_ref[:, h * HEAD_DIM:(h + 1) * HEAD_DIM]
        gain = qg_ref[...] if h < GQA_HEADS else kg_ref[...]
        y = x * lax.rsqrt(jnp.mean(x * x, axis=-1, keepdims=True) + RMS_EPS) * gain
        sw = jnp.where(first_half, pltpu.roll(y, HEAD_DIM - HEAD_DIM // 4, 1), pltpu.roll(y, HEAD_DIM // 4, 1))
        out = y * cos + sw * sin
        if h < GQA_HEADS:
            qt_ref[h * HEAD_DIM:(h + 1) * HEAD_DIM, :] = out.T.astype(BF16)
        else:
            hk = h - GQA_HEADS
            ko_ref[:, hk * HEAD_DIM:(hk + 1) * HEAD_DIM] = out.astype(BF16)
    v0 = (GQA_HEADS + GQA_KV_HEADS) * HEAD_DIM
    for hv in range(GQA_KV_HEADS):
        v = x_ref[:, v0 + hv * HEAD_DIM:v0 + (hv + 1) * HEAD_DIM]
        vt_ref[hv * HEAD_DIM:(hv + 1) * HEAD_DIM, :] = v.T.astype(BF16)


def qk_prep(fr, cos, sin, q_gain, k_gain, tm=768):
    m = fr.shape[0]
    wq, wk = GQA_HEADS * HEAD_DIM, GQA_KV_HEADS * HEAD_DIM
    return pl.pallas_call(
        _qk_prep_kernel,
        grid=(m // tm,),
        in_specs=[pl.BlockSpec((tm, wq + 2 * wk), lambda i: (i, 0)),
                  pl.BlockSpec((tm, HEAD_DIM), lambda i: (i, 0)),
                  pl.BlockSpec((tm, HEAD_DIM), lambda i: (i, 0)),
                  pl.BlockSpec((1, HEAD_DIM), lambda i: (0, 0)),
                  pl.BlockSpec((1, HEAD_DIM), lambda i: (0, 0))],
        out_specs=[pl.BlockSpec((wq, tm), lambda i: (0, i)),
                   pl.BlockSpec((tm, wk), lambda i: (i, 0)),
                   pl.BlockSpec((wk, tm), lambda i: (0, i))],
        out_shape=[jax.ShapeDtypeStruct((wq, m), BF16),
                   jax.ShapeDtypeStruct((m, wk), BF16),
                   jax.ShapeDtypeStruct((wk, m), BF16)],
        compiler_params=_cparams(("arbitrary",)),
        name="qk_prep",
    )(fr, cos, sin, q_gain.reshape(1, HEAD_DIM), k_gain.reshape(1, HEAD_DIM))


def _gqa_kernel(qt_ref, k_ref, vt_ref, o_ref, *, n_ctx, n_all, tq, tk):
    qi = pl.program_id(1)
    c1 = (HEAD_DIM ** -0.5) * LOG2E
    n_ctx_tiles = n_ctx // tq
    rep = GQA_HEADS // GQA_KV_HEADS

    def attend(n_keys, tkk):
        nchunks = n_keys // tkk

        def scores(c):
            kc = k_ref[c * tkk:(c + 1) * tkk, :]
            return [jnp.dot(kc, qt_ref[r * HEAD_DIM:(r + 1) * HEAD_DIM, :], preferred_element_type=F32)
                    for r in range(rep)]

        state = [None] * rep
        nxt = scores(0)
        for c in range(nchunks):
            cur = nxt
            if c + 1 < nchunks:
                nxt = scores(c + 1)
            vtc = vt_ref[:, c * tkk:(c + 1) * tkk]
            for r in range(rep):
                st = cur[r]
                mc = jnp.max(st, axis=0, keepdims=True)
                if state[r] is None:
                    m_new = mc
                    p = jnp.exp2(st * c1 - m_new * c1)
                    l = jnp.sum(p, axis=0, keepdims=True)
                    acc = jnp.dot(vtc, p.astype(BF16), preferred_element_type=F32)
                else:
                    m, l, acc = state[r]
                    m_new = jnp.maximum(m, mc)
                    p = jnp.exp2(st * c1 - m_new * c1)
                    alpha = jnp.exp2((m - m_new) * c1)
                    l = alpha * l + jnp.sum(p, axis=0, keepdims=True)
                    acc = alpha * acc + jnp.dot(vtc, p.astype(BF16), preferred_element_type=F32)
                state[r] = (m_new, l, acc)
        for r in range(rep):
            _, l, acc = state[r]
            o_ref[:, r * HEAD_DIM:(r + 1) * HEAD_DIM] = (acc / l).T.astype(o_ref.dtype)

    @pl.when(qi < n_ctx_tiles)
    def _ctx():
        attend(n_ctx, n_ctx)

    @pl.when(qi >= n_ctx_tiles)
    def _lat():
        attend(n_all, tk)


def gqa_attention(qt, kn, vt, n_ctx, tq=256, tk=1408):
    m = kn.shape[0]
    rep = GQA_HEADS // GQA_KV_HEADS
    if m % tk:
        tk = 256
    return pl.pallas_call(
        functools.partial(_gqa_kernel, n_ctx=n_ctx, n_all=m, tq=tq, tk=tk),
        grid=(GQA_KV_HEADS, m // tq),
        in_specs=[pl.BlockSpec((rep * HEAD_DIM, tq), lambda g, i: (g, i)),
                  pl.BlockSpec((m, HEAD_DIM), lambda g, i: (0, g)),
                  pl.BlockSpec((HEAD_DIM, m), lambda g, i: (g, 0))],
        out_specs=pl.BlockSpec((tq, rep * HEAD_DIM), lambda g, i: (i, g)),
        out_shape=jax.ShapeDtypeStruct((m, GQA_HEADS * HEAD_DIM), BF16),
        compiler_params=_cparams(("arbitrary", "arbitrary")),
        name="gqa_attn",
    )(qt, kn, vt)


def _ssd_prep_kernel(prev_ref, x_ref, next_ref, dtr_ref, cw_ref, cb_ref, dtb_ref, u_ref, dt_ref, *, n_ctx, n_all, tm):
    i = pl.program_id(0)
    lo = i * tm
    hi = lo + tm
    top_ok = jnp.logical_and(lo != 0, lo != n_ctx)
    bot_ok = jnp.logical_and(hi != n_ctx, hi != n_all)
    prev = jnp.where(top_ok, prev_ref[...], 0.0)
    nxt = jnp.where(bot_ok, next_ref[...], 0.0)
    ext = jnp.concatenate([prev, x_ref[...], nxt], axis=0)
    half = SSD_CONV // 2
    acc = None
    for j in range(SSD_CONV):
        sl = ext[8 - half + j:8 - half + j + tm, :]
        term = sl * cw_ref[j:j + 1, :]
        acc = term if acc is None else acc + term
    u_ref[...] = _silu(acc + cb_ref[...])
    dt_ref[...] = jax.nn.softplus(dtr_ref[...] + dtb_ref[...])


def ssd_prep(fzx, fr, conv_w, conv_b, dt_bias, n_ctx, tm=256):
    m = fzx.shape[0]
    nb8 = tm // 8
    last8 = m // 8 - 1
    cw = jnp.concatenate([conv_w, jnp.zeros((8 - SSD_CONV, SSD_XBC_W), F32)], axis=0)
    dtb = jnp.concatenate([dt_bias.reshape(-1), jnp.zeros((LANES - 2 * SSD_HEADS,), F32)]).reshape(1, LANES)
    return pl.pallas_call(
        functools.partial(_ssd_prep_kernel, n_ctx=n_ctx, n_all=m, tm=tm),
        grid=(m // tm,),
        in_specs=[pl.BlockSpec((8, SSD_XBC_W), lambda i: (jnp.maximum(i * nb8 - 1, 0), 0)),
                  pl.BlockSpec((tm, SSD_XBC_W), lambda i: (i, 0)),
                  pl.BlockSpec((8, SSD_XBC_W), lambda i: (jnp.minimum((i + 1) * nb8, last8), 0)),
                  pl.BlockSpec((tm, LANES), lambda i: (i, 0)),
                  pl.BlockSpec((8, SSD_XBC_W), lambda i: (0, 0)),
                  pl.BlockSpec((1, SSD_XBC_W), lambda i: (0, 0)),
                  pl.BlockSpec((1, LANES), lambda i: (0, 0))],
        out_specs=[pl.BlockSpec((tm, SSD_XBC_W), lambda i: (i, 0)),
                   pl.BlockSpec((tm, LANES), lambda i: (i, 0))],
        out_shape=[jax.ShapeDtypeStruct((m, SSD_XBC_W), F32),
                   jax.ShapeDtypeStruct((m, LANES), F32)],
        compiler_params=_cparams(("arbitrary",)),
        name="ssd_prep",
    )(fzx, fzx, fzx, fr, cw, conv_b.reshape(1, SSD_XBC_W), dtb)


def _ssd_chunk(u_ref, dt_ref, alog_ref, y_ref, ht_ref, d):
    ln = SSD_CHUNK
    p = SSD_HEAD_DIM
    ns = SSD_STATE
    epg = SSD_HEADS // SSD_GROUPS
    dt = dt_ref[...]
    if d == 1:
        dt = pltpu.roll(dt, LANES - SSD_HEADS, 1)
    a = dt * (-jnp.exp(alog_ref[d:d + 1, :]))
    li = lax.broadcasted_iota(jnp.int32, (ln, ln), 0)
    si = lax.broadcasted_iota(jnp.int32, (ln, ln), 1)
    mask = (li >= si) if d == 0 else (li <= si)
    a_cum = jnp.dot(mask.astype(F32), a, preferred_element_type=F32, precision=lax.Precision.HIGHEST)
    a_cum_t = a_cum.T
    a_tot = a_cum[ln - 1:ln, :] if d == 0 else a_cum[0:1, :]
    w_all = jnp.exp(a_tot - a_cum)
    ea_all = jnp.exp(a_cum)
    eat = jnp.exp(a_tot)
    for g in range(SSD_GROUPS):
        bg = u_ref[:, SSD_W + g * ns:SSD_W + (g + 1) * ns]
        cg = u_ref[:, SSD_W + SSD_GROUPS * ns + g * ns:SSD_W + SSD_GROUPS * ns + (g + 1) * ns]
        cgb = cg.astype(BF16)
        cb = _qk(cgb, bg.astype(BF16))
        bgt = bg.T.astype(BF16)
        for e in range(epg):
            h = g * epg + e
            ac = a_cum[:, h:h + 1]
            act = a_cum_t[h:h + 1, :]
            decay = jnp.exp(jnp.where(mask, ac - act, -jnp.inf))
            mm = (cb * decay).astype(BF16)
            xdt = u_ref[:, h * p:(h + 1) * p] * dt[:, h:h + 1]
            ht = ht_ref[d, h]
            y = (jnp.dot(mm, xdt.astype(BF16), preferred_element_type=F32)
                 + ea_all[:, h:h + 1] * jnp.dot(cgb, ht.astype(BF16), preferred_element_type=F32))
            st = jnp.dot(bgt, (xdt * w_all[:, h:h + 1]).astype(BF16), preferred_element_type=F32)
            ht_ref[d, h] = eat[:, h:h + 1] * ht + st
            y_ref[:, h * p:(h + 1) * p] = y


def _ssd_scan_kernel(uf_ref, dtf_ref, ub_ref, dtb_ref, alog_ref, yf_ref, yb_ref, ht_ref):
    @pl.when(pl.program_id(0) == 0)
    def _init():
        ht_ref[...] = jnp.zeros_like(ht_ref)

    _ssd_chunk(uf_ref, dtf_ref, alog_ref, yf_ref, ht_ref, 0)
    _ssd_chunk(ub_ref, dtb_ref, alog_ref, yb_ref, ht_ref, 1)


def ssd_scan(u, dt, a_log, n_ctx):
    m = u.shape[0]
    nc = m // SSD_CHUNK
    ncc = n_ctx // SSD_CHUNK

    def bwd_chunk(s):
        return jnp.where(s < ncc, ncc - 1 - s, ncc + nc - 1 - s)

    return pl.pallas_call(
        _ssd_scan_kernel,
        grid=(nc,),
        in_specs=[pl.BlockSpec((SSD_CHUNK, SSD_XBC_W), lambda s: (s, 0)),
                  pl.BlockSpec((SSD_CHUNK, LANES), lambda s: (s, 0)),
                  pl.BlockSpec((SSD_CHUNK, SSD_XBC_W), lambda s: (bwd_chunk(s), 0)),
                  pl.BlockSpec((SSD_CHUNK, LANES), lambda s: (bwd_chunk(s), 0)),
                  pl.BlockSpec((2, LANES), lambda s: (0, 0))],
        out_specs=[pl.BlockSpec((SSD_CHUNK, SSD_W), lambda s: (s, 0)),
                   pl.BlockSpec((SSD_CHUNK, SSD_W), lambda s: (bwd_chunk(s), 0))],
        out_shape=[jax.ShapeDtypeStruct((m, SSD_W), F32), jax.ShapeDtypeStruct((m, SSD_W), F32)],
        scratch_shapes=[pltpu.VMEM((2, SSD_HEADS, SSD_STATE, SSD_HEAD_DIM), F32)],
        compiler_params=_cparams(("arbitrary",)),
        name="ssd_scan",
    )(u, dt, u, dt, jnp.concatenate([a_log, jnp.zeros((2, LANES - SSD_HEADS), F32)], axis=1))


def _ssd_out_kernel(yf_ref, yb_ref, xs_ref, z_ref, dsk_ref, nw_ref, o_ref):
    y = yf_ref[...] + yb_ref[...] + dsk_ref[...] * xs_ref[...]
    gated = y * _silu(z_ref[...])
    gw = SSD_W // SSD_GROUPS
    for g in range(SSD_GROUPS):
        blk = gated[:, g * gw:(g + 1) * gw]
        nrm = blk * lax.rsqrt(jnp.mean(blk * blk, axis=-1, keepdims=True) + RMS_EPS)
        o_ref[:, g * gw:(g + 1) * gw] = (nrm * nw_ref[:, g * gw:(g + 1) * gw]).astype(o_ref.dtype)


def ssd_out(yf, yb, u, fb, d_skip, norm_w, tm=768):
    m = u.shape[0]
    dsk = jnp.repeat(d_skip.astype(F32), SSD_HEAD_DIM).reshape(1, SSD_W)
    return pl.pallas_call(
        _ssd_out_kernel,
        grid=(m // tm,),
        in_specs=[pl.BlockSpec((tm, SSD_W), lambda i: (i, 0)),
                  pl.BlockSpec((tm, SSD_W), lambda i: (i, 0)),
                  pl.BlockSpec((tm, SSD_W), lambda i: (i, 0)),
                  pl.BlockSpec((tm, SSD_W), lambda i: (i, ZX_Z // SSD_W)),
                  pl.BlockSpec((1, SSD_W), lambda i: (0, 0)),
                  pl.BlockSpec((1, SSD_W), lambda i: (0, 0))],
        out_specs=pl.BlockSpec((tm, SSD_W), lambda i: (i, 0)),
        out_shape=jax.ShapeDtypeStruct((m, SSD_W), BF16),
        compiler_params=_cparams(("arbitrary",)),
        name="ssd_out",
    )(yf, yb, u, fb, dsk, norm_w.reshape(1, SSD_W))


def _dft_tables(n):
    ang = 2.0 * np.pi * np.outer(np.arange(n), np.arange(n)) / n
    return jnp.asarray(np.cos(ang), F32), jnp.asarray(np.sin(ang), F32)


def _split_bf16(a):
    hi = a.astype(BF16)
    return hi, (a - hi.astype(F32)).astype(BF16)


def _hdot(a, b):
    ah, al = _split_bf16(a)
    bh, bl = _split_bf16(b)
    return (jnp.dot(ah, bh, preferred_element_type=F32)
            + (jnp.dot(ah, bl, preferred_element_type=F32) + jnp.dot(al, bh, preferred_element_type=F32)))


def _fnet_a_kernel(c_ref, s_ref, x_ref, yr_ref, yi_ref):
    x = x_ref[...]
    yr_ref[...] = _hdot(c_ref[...], x)
    yi_ref[...] = -_hdot(s_ref[...], x)


def _fnet_b_kernel(yr_ref, yi_ref, twc_ref, tws_ref, c1_ref, s1_ref, cc_ref, sc_ref, o_ref, *, scale, kb):
    c1, s1 = c1_ref[...], s1_ref[...]
    cc, sc = cc_ref[...], sc_ref[...]
    for j in range(kb):
        yr, yi = yr_ref[j], yi_ref[j]
        tc, ts = twc_ref[j], tws_ref[j]
        pr = yr * tc + yi * ts
        pi = yi * tc - yr * ts
        zr = _hdot(c1, pr) + _hdot(s1, pi)
        zi = _hdot(c1, pi) - _hdot(s1, pr)
        for g in range(FNET_GROUPS):
            cs = slice(g * FNET_GROUP_DIM, (g + 1) * FNET_GROUP_DIM)
            o_ref[:, j, cs] = (_hdot(zr[:, cs], cc) + _hdot(zi[:, cs], sc)) * scale


def _fnet_ctx_kernel(x_ref, cn_ref, sn_ref, cc_ref, sc_ref, o_ref, *, scale):
    x = x_ref[...]
    wr = _hdot(cn_ref[...], x)
    ws = _hdot(sn_ref[...], x)
    cc, sc = cc_ref[...], sc_ref[...]
    for g in range(FNET_GROUPS):
        cs = slice(g * FNET_GROUP_DIM, (g + 1) * FNET_GROUP_DIM)
        o_ref[:, cs] = (_hdot(wr[:, cs], cc) - _hdot(ws[:, cs], sc)) * scale


def fourier_mix(f_lat, f_ctx):
    n_lat, w = f_lat.shape
    n_ctx = f_ctx.shape[0]
    n2 = 128
    n1 = n_lat // n2
    c2, s2 = _dft_tables(n2)
    c1, s1 = _dft_tables(n1)
    cc, sc = _dft_tables(FNET_GROUP_DIM)
    tw = 2.0 * np.pi * np.outer(np.arange(n2), np.arange(n1)) / n_lat
    twc = jnp.asarray(np.cos(tw), F32).reshape(n2, n1, 1)
    tws = jnp.asarray(np.sin(tw), F32).reshape(n2, n1, 1)
    xr = f_lat.reshape(n2, n1 * w)
    tn = min(4096, n1 * w)
    yr, yi = pl.pallas_call(
        _fnet_a_kernel,
        grid=(n1 * w // tn,),
        in_specs=[pl.BlockSpec((n2, n2), lambda j: (0, 0)),
                  pl.BlockSpec((n2, n2), lambda j: (0, 0)),
                  pl.BlockSpec((n2, tn), lambda j: (0, j))],
        out_specs=[pl.BlockSpec((n2, tn), lambda j: (0, j)),
                   pl.BlockSpec((n2, tn), lambda j: (0, j))],
        out_shape=[jax.ShapeDtypeStruct((n2, n1 * w), F32)] * 2,
        compiler_params=_cparams(("arbitrary",)),
        name="fnet_stage_a",
    )(c2, s2, xr)
    kb = 8
    lat = pl.pallas_call(
        functools.partial(_fnet_b_kernel, scale=float(1.0 / math.sqrt(n_lat * FNET_GROUP_DIM)), kb=kb),
        grid=(n2 // kb,),
        in_specs=[pl.BlockSpec((kb, n1, w), lambda j: (j, 0, 0)),
                  pl.BlockSpec((kb, n1, w), lambda j: (j, 0, 0)),
                  pl.BlockSpec((kb, n1, 1), lambda j: (j, 0, 0)),
                  pl.BlockSpec((kb, n1, 1), lambda j: (j, 0, 0)),
                  pl.BlockSpec((n1, n1), lambda j: (0, 0)),
                  pl.BlockSpec((n1, n1), lambda j: (0, 0)),
                  pl.BlockSpec((FNET_GROUP_DIM, FNET_GROUP_DIM), lambda j: (0, 0)),
                  pl.BlockSpec((FNET_GROUP_DIM, FNET_GROUP_DIM), lambda j: (0, 0))],
        out_specs=pl.BlockSpec((n1, kb, w), lambda j: (0, j, 0)),
        out_shape=jax.ShapeDtypeStruct((n1, n2, w), F32),
        compiler_params=_cparams(("arbitrary",)),
        name="fnet_stage_b",
    )(yr.reshape(n2, n1, w), yi.reshape(n2, n1, w), twc, tws, c1, s1, cc, sc)
    cn, sn = _dft_tables(n_ctx)
    ctx = pl.pallas_call(
        functools.partial(_fnet_ctx_kernel, scale=float(1.0 / math.sqrt(n_ctx * FNET_GROUP_DIM))),
        out_shape=jax.ShapeDtypeStruct((n_ctx, w), F32),
        compiler_params=pltpu.CompilerParams(vmem_limit_bytes=VMEM_LIMIT_BYTES),
        name="fnet_ctx",
    )(f_ctx, cn, sn, cc, sc)
    return jnp.concatenate([ctx, lat.reshape(n_lat, w)], axis=0)


def _merge_kernel(h_ref, *refs):
    wg_refs, o_refs = refs[0:4], refs[4:8]
    wbr_ref, y_ref, ws_ref = refs[8], refs[9], refs[10]

    @pl.when(pl.program_id(1) == 0)
    def _convert():
        for b in range(N_BRANCH):
            ws_ref[b] = wg_refs[b][0].astype(BF16)

    h = h_ref[...]
    y = None
    for b in range(N_BRANCH):
        g = lax.dot_general(h, ws_ref[b], _NT, preferred_element_type=F32)
        pr = jnp.dot(o_refs[b][...].astype(BF16), wbr_ref[0, b].astype(BF16), preferred_element_type=F32)
        t = jax.nn.sigmoid(g) * pr
        y = t if y is None else y + t
    y_ref[...] = y.astype(y_ref.dtype)


def merge(h, wt_all, gate_col0, branches, w_branch_all, layer, tm=768, tn=256):
    m, k = h.shape
    nb = D_MODEL // tn
    wg_specs = [_wt_block(layer, gate_col0 + b * D_MODEL, tn, k) for b in range(N_BRANCH)]
    o_specs = [pl.BlockSpec((tm, BRANCH_W), lambda j, i: (i, 0)) for _ in range(N_BRANCH)]
    return pl.pallas_call(
        _merge_kernel,
        grid=(nb, m // tm),
        in_specs=([pl.BlockSpec((tm, k), lambda j, i: (i, 0))] + wg_specs + o_specs
                  + [pl.BlockSpec((1, N_BRANCH, BRANCH_W, tn), lambda j, i: (layer, 0, 0, j))]),
        out_specs=pl.BlockSpec((tm, tn), lambda j, i: (i, j)),
        out_shape=jax.ShapeDtypeStruct((m, D_MODEL), BF16),
        scratch_shapes=[pltpu.VMEM((N_BRANCH, tn, k), BF16)],
        compiler_params=_cparams(("arbitrary", "arbitrary")),
        name="merge",
    )(h, *([wt_all] * N_BRANCH), *branches, w_branch_all)


def _expert_kernel(idx_ref, h_hbm, wg_ref, wu_ref, wd_ref, gate_ref, o_ref, xs_ref, sem, *, nf):
    e = pl.program_id(0)
    f = pl.program_id(1)
    cap = xs_ref.shape[1]
    slot = e % 2

    def start_gather(ee, sl):
        def issue(c, carry):
            row = idx_ref[ee, c]
            pltpu.make_async_copy(h_hbm.at[pl.ds(row, 1)], xs_ref.at[sl, pl.ds(c, 1)], sem.at[sl]).start()
            return carry

        lax.fori_loop(0, cap, issue, 0, unroll=8)

    @pl.when(jnp.logical_and(e == 0, f == 0))
    def _cold_start():
        start_gather(0, 0)

    @pl.when(f == 0)
    def _wait_rows():
        pltpu.make_async_copy(h_hbm.at[pl.ds(0, cap)], xs_ref.at[slot], sem.at[slot]).wait()

    @pl.when(jnp.logical_and(f == 0, e + 1 < pl.num_programs(0)))
    def _prefetch_next():
        start_gather(e + 1, 1 - slot)

    xs = xs_ref[slot].astype(BF16)
    hg = jnp.dot(xs, wg_ref[0, 0].astype(BF16), preferred_element_type=F32)
    hu = jnp.dot(xs, wu_ref[0, 0].astype(BF16), preferred_element_type=F32)
    hid = (_silu(hg) * hu).astype(BF16)
    part = jnp.dot(hid, wd_ref[0, 0].astype(BF16), preferred_element_type=F32)

    @pl.when(f == 0)
    def _first():
        o_ref[0] = part

    @pl.when(f != 0)
    def _rest():
        o_ref[0] += part

    @pl.when(f == nf - 1)
    def _done():
        o_ref[0] = o_ref[0] * gate_ref[0]


def expert_ffn(h2, idx, gate, w_gate, w_up, w_down, layer, tf=256):
    ne, cap = idx.shape
    d = h2.shape[1]
    ff = w_gate.shape[3]
    grid_spec = pltpu.PrefetchScalarGridSpec(
        num_scalar_prefetch=1,
        grid=(ne, ff // tf),
        in_specs=[pl.BlockSpec(memory_space=pl.ANY),
                  pl.BlockSpec((1, 1, d, tf), lambda e, f, ix: (layer, e, 0, f)),
                  pl.BlockSpec((1, 1, d, tf), lambda e, f, ix: (layer, e, 0, f)),
                  pl.BlockSpec((1, 1, tf, d), lambda e, f, ix: (layer, e, f, 0)),
                  pl.BlockSpec((1, cap, 1), lambda e, f, ix: (e, 0, 0))],
        out_specs=pl.BlockSpec((1, cap, d), lambda e, f, ix: (e, 0, 0)),
        scratch_shapes=[pltpu.VMEM((2, cap, d), F32), pltpu.SemaphoreType.DMA((2,))],
    )
    return pl.pallas_call(
        functools.partial(_expert_kernel, nf=ff // tf),
        grid_spec=grid_spec,
        out_shape=jax.ShapeDtypeStruct((ne, cap, d), F32),
        compiler_params=_cparams(("arbitrary", "arbitrary")),
        name="expert_ffn",
    )(idx, h2, w_gate, w_up, w_down, gate.reshape(ne, cap, 1))


COMBINE_TOKENS = 64


def _combine_kernel(src_ref, dst_ref, tstart_ref, rmax_ref, ye_hbm, x_ref, g_ref, o_ref, s_ref, sem, *, n_ctx,
                    ntiles):
    t = pl.program_id(0)
    tt = COMBINE_TOKENS
    d = s_ref.shape[2]
    buf = t % 2

    def fill(tile, b):
        def zero(r, carry):
            s_ref[b, pl.ds(pl.multiple_of(r * tt, tt), tt), :] = jnp.zeros((tt, d), F32)
            return carry

        lax.fori_loop(0, rmax_ref[tile], zero, 0)

        def issue(p, carry):
            pltpu.make_async_copy(ye_hbm.at[pl.ds(src_ref[p], 1)], s_ref.at[b, pl.ds(dst_ref[p], 1)],
                                  sem.at[b]).start()
            return carry

        lax.fori_loop(tstart_ref[tile], tstart_ref[tile + 1], issue, 0)

    @pl.when(t == 0)
    def _first():
        fill(0, 0)

    @pl.when(t + 1 < ntiles)
    def _next():
        fill(t + 1, 1 - buf)

    def wait(p, carry):
        pltpu.make_async_copy(ye_hbm.at[pl.ds(0, 1)], s_ref.at[buf, pl.ds(0, 1)], sem.at[buf]).wait()
        return carry

    lax.fori_loop(tstart_ref[t], tstart_ref[t + 1], wait, 0)

    def add(r, acc):
        return acc + s_ref[buf, pl.ds(pl.multiple_of(r * tt, tt), tt), :]

    acc = lax.fori_loop(0, rmax_ref[t], add, jnp.zeros((tt, d), F32))
    row = t * tt + lax.broadcasted_iota(jnp.int32, (tt, 1), 0)
    g = jnp.where(row < n_ctx, g_ref[1:2, :], g_ref[0:1, :])
    o_ref[...] = x_ref[...] + g * acc


def moe_combine(x, ye, idx, mod, gate_chunk, n_ctx):
    m, d = x.shape
    ne, cap = idx.shape
    npairs = ne * cap
    tt = COMBINE_TOKENS
    ntiles = m // tt
    tok = idx.reshape(-1)
    tok_s, src_s = lax.sort((tok, jnp.arange(npairs, dtype=jnp.int32)), num_keys=1)
    pos = jnp.arange(npairs, dtype=jnp.int32)
    is_first = jnp.concatenate([jnp.ones((1,), bool), tok_s[1:] != tok_s[:-1]])
    first = lax.cummax(jnp.where(is_first, pos, 0), axis=0)
    rank = pos - first
    dst = rank * tt + tok_s % tt
    tile_s = tok_s // tt
    tiles = jnp.arange(ntiles + 1, dtype=jnp.int32)
    tstart = jnp.sum((tile_s[None, :] < tiles[:, None]).astype(jnp.int32), axis=1)
    rmax = jnp.max(jnp.where(tile_s[None, :] == tiles[:ntiles, None], rank[None, :] + 1, 0), axis=1)
    grid_spec = pltpu.PrefetchScalarGridSpec(
        num_scalar_prefetch=4,
        grid=(ntiles,),
        in_specs=[pl.BlockSpec(memory_space=pl.ANY),
                  pl.BlockSpec((tt, d), lambda t, *_: (t, 0)),
                  pl.BlockSpec((8, d), lambda t, *_: (0, gate_chunk))],
        out_specs=pl.BlockSpec((tt, d), lambda t, *_: (t, 0)),
        scratch_shapes=[pltpu.VMEM((2, ne * tt, d), F32), pltpu.SemaphoreType.DMA((2,))],
    )
    return pl.pallas_call(
        functools.partial(_combine_kernel, n_ctx=n_ctx, ntiles=ntiles),
        grid_spec=grid_spec,
        out_shape=jax.ShapeDtypeStruct((m, d), F32),
        compiler_params=_cparams(("arbitrary",)),
        name="moe_combine",
    )(src_s, dst, tstart, rmax, ye.reshape(npairs, d), x, mod)


def moe(x, h2, aff, mod, w_gate, w_up, w_down, layer, n_ctx):
    m = h2.shape[0]
    n_lat = m - n_ctx
    a = aff[:, :N_EXPERTS]
    g_c, i_c = lax.top_k(a[:n_ctx].T, CAPACITY_FACTOR * n_ctx // N_EXPERTS)
    g_l, i_l = lax.top_k(a[n_ctx:].T, CAPACITY_FACTOR * n_lat // N_EXPERTS)
    idx = jnp.concatenate([i_c, i_l + n_ctx], axis=1).astype(jnp.int32)
    gate = jnp.concatenate([g_c, g_l], axis=1)
    ye = expert_ffn(h2, idx, gate, w_gate, w_up, w_down, layer)
    return moe_combine(x, ye, idx, mod, 5, n_ctx)


_P_NA = 0
_P_Z = _P_NA + 3 * NA_HEADS * HEAD_DIM
_P_XBC = _P_Z + SSD_W
_P_DT = _P_XBC + SSD_XBC_W
_P_GQ = _P_DT + 2 * SSD_HEADS
_P_FN = _P_GQ + (GQA_HEADS + 2 * GQA_KV_HEADS) * HEAD_DIM
_P_GATES = _P_FN + FNET_GROUPS * FNET_GROUP_DIM


def _layer(x, mod, n_ctx, layer, norm_mix, norm_ffn, wt_in_all, na_rpb, conv_w, conv_b, a_log, dt_bias, d_skip,
           ssd_gn, q_gain, k_gain, w_branch_all, w_out_all, w_router, w_gate_all, w_up_all, w_down_all, rope):
    m = x.shape[0]
    tm = 768 if m % 768 == 0 else 256
    h = norm_mod(x, norm_mix, mod, n_ctx, 0, tm=tm)
    ab = matmul_wt(h, wt_in_all, layer, _P_NA, _P_Z - _P_NA, BF16, tm, 768, "proj_na")
    fzx = matmul_wt(h, wt_in_all, layer, _P_Z, _P_DT - _P_Z, F32, tm, 512, "proj_zx", out_perm=(2, 0, 1))
    fr = matmul_wt(h, wt_in_all, layer, _P_GQ, _P_GATES - _P_GQ, F32, tm, 512, "proj_rest")
    fdt = matmul_wt(h, wt_in_all, layer, _P_DT, LANES, F32, tm, LANES, "proj_dt")

    rows = (m - n_ctx) // GRID_W
    bias, tid = na_bias_table(na_rpb, rows)
    o_na = na_attention(ab, bias, tid, n_ctx)

    u, dt = ssd_prep(fzx, fdt, conv_w, conv_b, dt_bias, n_ctx)
    yf, yb = ssd_scan(u, dt, a_log, n_ctx)
    o_ssd = ssd_out(yf, yb, u, fzx, d_skip, ssd_gn, tm=tm)

    qt, kn, vt = qk_prep(fr, rope[0], rope[1], q_gain, k_gain, tm=tm)
    o_gqa = gqa_attention(qt, kn, vt, n_ctx)

    f_in = fr[:, FR_FN:FR_FN + FNET_GROUPS * FNET_GROUP_DIM]
    o_fn = fourier_mix(f_in[n_ctx:], f_in[:n_ctx])

    y = merge(h, wt_in_all, _P_GATES, (o_na, o_ssd, o_gqa, o_fn), w_branch_all, layer, tm=tm)
    x = matmul_residual(y, w_out_all, layer, x, mod, 2, n_ctx, tm, 1024, "out_proj")

    wr = jnp.concatenate([w_router, jnp.zeros((D_MODEL, LANES - N_EXPERTS), w_router.dtype)], axis=1)
    h2, aff = norm_router(x, norm_ffn, mod, wr, n_ctx, 3, tm=tm)
    return moe(x, h2, aff, mod, w_gate_all, w_up_all, w_down_all, layer, n_ctx)


def kernel(x, c, ctx, c_ctx, w_ada, b_ada, norm_mix, norm_ffn, w_in, na_rpb, ssd_conv_w, ssd_conv_b, ssd_a_log,
           ssd_dt_bias, ssd_d, ssd_norm, gqa_q_norm, gqa_k_norm, w_branch, w_out, w_router, moe_w_gate, moe_w_up,
           moe_w_down, final_norm_w):
    n_lat = x.shape[1]
    n_ctx = ctx.shape[1]
    depth = w_ada.shape[0]
    cc = jnp.concatenate([c[0:1], c_ctx[None], jnp.zeros((6, D_MODEL), F32)], axis=0)
    mods = ada_all(cc, w_ada, b_ada)
    rope = rope_tables(n_ctx, n_lat)
    xs = jnp.concatenate([ctx[0], x[0]], axis=0)
    wt_in = jnp.swapaxes(w_in, 1, 2)
    for l in range(depth):
        xs = _layer(xs, mods[l], n_ctx, l, norm_mix[l], norm_ffn[l], wt_in, na_rpb[l], ssd_conv_w[l],
                    ssd_conv_b[l], ssd_a_log[l], ssd_dt_bias[l], ssd_d[l], ssd_norm[l], gqa_q_norm[l],
                    gqa_k_norm[l], w_branch, w_out, w_router[l], moe_w_gate, moe_w_up, moe_w_down, rope)
    out = final_norm(xs, final_norm_w, n_ctx, n_lat)
    return out[None]
```

```python
import functools
import math

import numpy as np
import jax
import jax.numpy as jnp
from jax import lax
from jax.experimental import pallas as pl
from jax.experimental.pallas import tpu as pltpu

F32 = jnp.float32
BF16 = jnp.bfloat16

D_MODEL = 2048
GRID_W = 64
HEAD_DIM = 128
RMS_EPS = 1e-6
N_BRANCH = 4
BRANCH_W = 512
NA_HEADS = 4
NA_WIN_R = 8
NA_WIN_C = 16
SSD_HEADS = 8
SSD_HEAD_DIM = 64
SSD_GROUPS = 2
SSD_STATE = 128
SSD_CONV = 5
SSD_CHUNK = 128
SSD_W = SSD_HEADS * SSD_HEAD_DIM
SSD_XBC_W = SSD_W + 2 * SSD_GROUPS * SSD_STATE
GQA_HEADS = 4
GQA_KV_HEADS = 2
ROPE_THETA = 10000.0
FNET_GROUPS = 4
FNET_GROUP_DIM = 128
N_EXPERTS = 16
EXPERT_FF = D_MODEL // 2
CAPACITY_FACTOR = 2
LOG2E = 1.4426950408889634

VMEM_LIMIT_BYTES = 56 * 1024 * 1024
LANES = 128

ZX_XBC, ZX_Z, ZX_W = 0, 1024, 1536
FR_Q, FR_K, FR_V, FR_FN, FR_W = 0, 512, 768, 1024, 1536


def _cparams(sem):
    return pltpu.CompilerParams(dimension_semantics=sem, vmem_limit_bytes=VMEM_LIMIT_BYTES)


def _silu(x):
    return x * jax.nn.sigmoid(x)


def _ada_kernel(ct_ref, w_ref, b_ref, o_ref):
    a = _silu(ct_ref[...])
    w = w_ref[0]
    rows = [jnp.sum(w * a[:, r:r + 1], axis=0, keepdims=True) + b_ref[0] for r in range(2)]
    o_ref[0] = jnp.concatenate(rows + [jnp.zeros((6, w.shape[1]), F32)], axis=0)


def ada_all(cc, w_ada, b_ada):
    depth, d, n = w_ada.shape
    tn = 1024
    return pl.pallas_call(
        _ada_kernel,
        grid=(depth, n // tn),
        in_specs=[pl.BlockSpec((d, 8), lambda l, j: (0, 0)),
                  pl.BlockSpec((1, d, tn), lambda l, j: (l, 0, j)),
                  pl.BlockSpec((1, 1, tn), lambda l, j: (l, 0, j))],
        out_specs=pl.BlockSpec((1, 8, tn), lambda l, j: (l, 0, j)),
        out_shape=jax.ShapeDtypeStruct((depth, 8, n), F32),
        compiler_params=_cparams(("arbitrary", "arbitrary")),
        name="ada_mod",
    )(cc.T, w_ada, b_ada.reshape(depth, 1, n))


def _row_select(mod_ref, chunk, row_is_ctx):
    lo = chunk * D_MODEL
    return jnp.where(row_is_ctx, mod_ref[1:2, lo:lo + D_MODEL], mod_ref[0:1, lo:lo + D_MODEL])


def _norm_body(x_ref, nw_ref):
    x = x_ref[...]
    ms = jnp.mean(x * x, axis=-1, keepdims=True)
    return x * lax.rsqrt(ms + RMS_EPS) * nw_ref[...]


def _norm_mod_kernel(x_ref, nw_ref, mod_ref, o_ref, *, n_ctx, tm, sh_chunk):
    y = _norm_body(x_ref, nw_ref)
    row = pl.program_id(0) * tm + lax.broadcasted_iota(jnp.int32, (tm, 1), 0)
    is_ctx = row < n_ctx
    sh = _row_select(mod_ref, sh_chunk, is_ctx)
    sc = _row_select(mod_ref, sh_chunk + 1, is_ctx)
    o_ref[...] = (y * (1.0 + sc) + sh).astype(o_ref.dtype)


def norm_mod(x, nw, mod, n_ctx, sh_chunk, tm=768):
    m, d = x.shape
    return pl.pallas_call(
        functools.partial(_norm_mod_kernel, n_ctx=n_ctx, tm=tm, sh_chunk=sh_chunk),
        grid=(m // tm,),
        in_specs=[pl.BlockSpec((tm, d), lambda i: (i, 0)),
                  pl.BlockSpec((1, d), lambda i: (0, 0)),
                  pl.BlockSpec(mod.shape, lambda i: (0, 0))],
        out_specs=pl.BlockSpec((tm, d), lambda i: (i, 0)),
        out_shape=jax.ShapeDtypeStruct((m, d), BF16),
        compiler_params=_cparams(("arbitrary",)),
        name="norm_mod",
    )(x, nw.reshape(1, d), mod)


def _norm_router_kernel(x_ref, nw_ref, mod_ref, wr_ref, h_ref, aff_ref, *, n_ctx, tm, sh_chunk):
    y = _norm_body(x_ref, nw_ref)
    row = pl.program_id(0) * tm + lax.broadcasted_iota(jnp.int32, (tm, 1), 0)
    is_ctx = row < n_ctx
    sh = _row_select(mod_ref, sh_chunk, is_ctx)
    sc = _row_select(mod_ref, sh_chunk + 1, is_ctx)
    h = y * (1.0 + sc) + sh
    h_ref[...] = h
    logits = jnp.dot(h.astype(BF16), wr_ref[...].astype(BF16), preferred_element_type=F32)
    lane = lax.broadcasted_iota(jnp.int32, logits.shape, 1)
    logits = jnp.where(lane < N_EXPERTS, logits, -jnp.inf)
    mx = jnp.max(logits, axis=-1, keepdims=True)
    e = jnp.exp(logits - mx)
    aff_ref[...] = e / jnp.sum(e, axis=-1, keepdims=True)


def norm_router(x, nw, mod, w_router_pad, n_ctx, sh_chunk, tm=768):
    m, d = x.shape
    return pl.pallas_call(
        functools.partial(_norm_router_kernel, n_ctx=n_ctx, tm=tm, sh_chunk=sh_chunk),
        grid=(m // tm,),
        in_specs=[pl.BlockSpec((tm, d), lambda i: (i, 0)),
                  pl.BlockSpec((1, d), lambda i: (0, 0)),
                  pl.BlockSpec(mod.shape, lambda i: (0, 0)),
                  pl.BlockSpec((d, LANES), lambda i: (0, 0))],
        out_specs=[pl.BlockSpec((tm, d), lambda i: (i, 0)),
                   pl.BlockSpec((tm, LANES), lambda i: (i, 0))],
        out_shape=[jax.ShapeDtypeStruct((m, d), F32),
                   jax.ShapeDtypeStruct((m, LANES), F32)],
        compiler_params=_cparams(("arbitrary",)),
        name="norm_router",
    )(x, nw.reshape(1, d), mod, w_router_pad)


def _final_norm_kernel(x_ref, nw_ref, o_ref):
    o_ref[...] = _norm_body(x_ref, nw_ref)


def final_norm(x, nw, row0, n_rows, tm=256):
    d = x.shape[1]
    off = row0 // tm
    return pl.pallas_call(
        _final_norm_kernel,
        grid=(n_rows // tm,),
        in_specs=[pl.BlockSpec((tm, d), lambda i: (i + off, 0)),
                  pl.BlockSpec((1, d), lambda i: (0, 0))],
        out_specs=pl.BlockSpec((tm, d), lambda i: (i, 0)),
        out_shape=jax.ShapeDtypeStruct((n_rows, d), F32),
        compiler_params=_cparams(("arbitrary",)),
        name="final_norm",
    )(x, nw.reshape(1, d))


_NT = (((1,), (1,)), ((), ()))


def _wt_block(layer, row0, tn, k):
    return pl.BlockSpec((pl.Element(1), pl.Element(tn), pl.Element(k)),
                        lambda j, i: (layer, pl.multiple_of(row0 + j * tn, 16), 0))


def _mm_wt_kernel(a_ref, wt_ref, o_ref):
    w = wt_ref[0].astype(BF16)
    o_ref[...] = lax.dot_general(a_ref[...], w, _NT, preferred_element_type=F32).astype(o_ref.dtype)


def matmul_wt(a, wt_all, layer, col0, ncols, out_dtype, tm, tn, name, out_perm=None):
    m, k = a.shape
    nb = ncols // tn
    perm = tuple(range(nb)) if out_perm is None else tuple(out_perm)

    def out_map(j, i):
        pj = j
        for src, dst in enumerate(perm):
            pj = jnp.where(j == src, dst, pj)
        return (i, pj)

    return pl.pallas_call(
        _mm_wt_kernel,
        grid=(nb, m // tm),
        in_specs=[pl.BlockSpec((tm, k), lambda j, i: (i, 0)), _wt_block(layer, col0, tn, k)],
        out_specs=pl.BlockSpec((tm, tn), out_map),
        out_shape=jax.ShapeDtypeStruct((m, ncols), out_dtype),
        compiler_params=_cparams(("arbitrary", "arbitrary")),
        name=name,
    )(a, wt_all)


def _mm_residual_kernel(a_ref, w_ref, x_ref, g_ref, o_ref, *, n_ctx, tm):
    acc = jnp.dot(a_ref[...].astype(BF16), w_ref[0].astype(BF16), preferred_element_type=F32)
    row = pl.program_id(1) * tm + lax.broadcasted_iota(jnp.int32, (tm, 1), 0)
    g = jnp.where(row < n_ctx, g_ref[1:2, :], g_ref[0:1, :])
    o_ref[...] = x_ref[...] + g * acc


def matmul_residual(a, w_all, layer, x, mod, gate_chunk, n_ctx, tm, tn, name):
    m, k = a.shape
    n = w_all.shape[2]
    goff = gate_chunk * D_MODEL // tn
    return pl.pallas_call(
        functools.partial(_mm_residual_kernel, n_ctx=n_ctx, tm=tm),
        grid=(n // tn, m // tm),
        in_specs=[pl.BlockSpec((tm, k), lambda j, i: (i, 0)),
                  pl.BlockSpec((1, k, tn), lambda j, i: (layer, 0, j)),
                  pl.BlockSpec((tm, tn), lambda j, i: (i, j)),
                  pl.BlockSpec((8, tn), lambda j, i: (0, goff + j))],
        out_specs=pl.BlockSpec((tm, tn), lambda j, i: (i, j)),
        out_shape=jax.ShapeDtypeStruct((m, n), F32),
        compiler_params=_cparams(("arbitrary", "arbitrary")),
        name=name,
    )(a, w_all, x, mod)


def _softmax_pv(parts):
    m = None
    for s, _ in parts:
        mi = jnp.max(s, axis=-1, keepdims=True)
        m = mi if m is None else jnp.maximum(m, mi)
    ps, l = [], None
    for s, _ in parts:
        p = jnp.exp(s - m)
        ps.append(p)
        li = jnp.sum(p, axis=-1, keepdims=True)
        l = li if l is None else l + li
    inv = 1.0 / l
    o = None
    for p, (_, v) in zip(ps, parts):
        oi = jnp.dot((p * inv).astype(BF16), v, preferred_element_type=F32)
        o = oi if o is None else o + oi
    return o


def _qk(q, k):
    return lax.dot_general(q, k, (((1,), (1,)), ((), ())), preferred_element_type=F32)


NA_QROWS = 4
NA_KROWS = NA_WIN_R + NA_QROWS - 1


def _na_kernel(tid_ref, q_ref, k_ref, v_ref, b_ref, o_ref, *, n_ctx, rows):
    del tid_ref
    step = pl.program_id(0)
    tq = NA_QROWS * GRID_W
    n_ctx_steps = n_ctx // tq
    scale = HEAD_DIM ** -0.5
    win = NA_KROWS * GRID_W

    @pl.when(step < n_ctx_steps)
    def _ctx():
        for h in range(NA_HEADS):
            cs = slice(h * HEAD_DIM, (h + 1) * HEAD_DIM)
            q = q_ref[:, cs]
            s_c = _qk(q, k_ref[0:n_ctx, cs]) * scale
            o_ref[:, cs] = _softmax_pv([(s_c, v_ref[0:n_ctx, cs])]).astype(o_ref.dtype)

    @pl.when(step >= n_ctx_steps)
    def _lat():
        p = step - n_ctx_steps
        row0 = jnp.clip(NA_QROWS * p - NA_WIN_R // 2, 0, rows - NA_KROWS)
        base = pl.multiple_of(n_ctx + row0 * GRID_W, GRID_W)
        for h in range(NA_HEADS):
            cs = slice(h * HEAD_DIM, (h + 1) * HEAD_DIM)
            q = q_ref[:, cs]
            s_w = _qk(q, k_ref[pl.ds(base, win), cs]) * scale + b_ref[0, h]
            s_c = _qk(q, k_ref[0:n_ctx, cs]) * scale
            o = _softmax_pv([(s_w, v_ref[pl.ds(base, win), cs]), (s_c, v_ref[0:n_ctx, cs])])
            o_ref[:, cs] = o.astype(o_ref.dtype)


def _na_patterns(rows):
    a = np.arange(NA_QROWS)[:, None]
    i = np.arange(NA_KROWS)[None, :]
    pats, keys, tid = [], [], []
    for p in range(rows // NA_QROWS):
        row0 = int(np.clip(NA_QROWS * p - NA_WIN_R // 2, 0, rows - NA_KROWS))
        r = NA_QROWS * p + a
        rs = np.clip(r - NA_WIN_R // 2, 0, rows - NA_WIN_R)
        krow = row0 + i
        valid = (krow >= rs) & (krow < rs + NA_WIN_R)
        ridx = np.clip(krow - r + NA_WIN_R - 1, 0, 2 * NA_WIN_R - 2)
        key = (valid.tobytes(), ridx.tobytes())
        if key not in keys:
            keys.append(key)
            pats.append((valid, ridx))
        tid.append(keys.index(key))
    return np.stack([v for v, _ in pats]), np.stack([x for _, x in pats]), np.asarray(tid, np.int32)


def na_bias_table(rpb, rows):
    valid_r, ridx, tid = _na_patterns(rows)
    kc = np.arange(GRID_W)[None, :]
    qc = np.arange(GRID_W)[:, None]
    col_start = np.clip(qc - NA_WIN_C // 2, 0, GRID_W - NA_WIN_C)
    valid_c = (kc >= col_start) & (kc < col_start + NA_WIN_C)
    cidx = np.clip(kc - qc + NA_WIN_C - 1, 0, 2 * NA_WIN_C - 2)
    nh, nbr, nbc = rpb.shape
    onehot = (np.arange(nbc)[:, None, None] == cidx[None]) & valid_c[None]
    toe = jnp.einsum('hbj,jqk->hbqk', rpb.astype(F32), jnp.asarray(onehot, F32), precision=lax.Precision.HIGHEST)
    toe = toe + jnp.asarray(np.where(valid_c, 0.0, -np.inf), F32)
    neg = jnp.full((nh, GRID_W, GRID_W), -jnp.inf, F32)
    npat = valid_r.shape[0]
    blocks = [toe[:, int(ridx[p, a, i])] if valid_r[p, a, i] else neg
              for p in range(npat) for a in range(NA_QROWS) for i in range(NA_KROWS)]
    t = jnp.stack(blocks, axis=0).reshape(npat, NA_QROWS, NA_KROWS, nh, GRID_W, GRID_W)
    t = jnp.transpose(t, (0, 3, 1, 4, 2, 5))
    return t.reshape(npat, nh, NA_QROWS * GRID_W, NA_KROWS * GRID_W), jnp.asarray(tid)


def na_attention(ab, bias, tid, n_ctx):
    m = ab.shape[0]
    rows = (m - n_ctx) // GRID_W
    tq = NA_QROWS * GRID_W
    n_ctx_steps = n_ctx // tq
    w = NA_HEADS * HEAD_DIM
    grid_spec = pltpu.PrefetchScalarGridSpec(
        num_scalar_prefetch=1,
        grid=(m // tq,),
        in_specs=[pl.BlockSpec((tq, w), lambda s, t: (s, 0)),
                  pl.BlockSpec((m, w), lambda s, t: (0, 1)),
                  pl.BlockSpec((m, w), lambda s, t: (0, 2)),
                  pl.BlockSpec((1, NA_HEADS, tq, NA_KROWS * GRID_W),
                               lambda s, t: (t[jnp.maximum(s - n_ctx_steps, 0)], 0, 0, 0))],
        out_specs=pl.BlockSpec((tq, w), lambda s, t: (s, 0)),
    )
    return pl.pallas_call(
        functools.partial(_na_kernel, n_ctx=n_ctx, rows=rows),
        grid_spec=grid_spec,
        out_shape=jax.ShapeDtypeStruct((m, w), BF16),
        compiler_params=_cparams(("arbitrary",)),
        name="na_attn",
    )(tid, ab, ab, ab, bias)


def rope_tables(n_ctx, n_lat):
    t = jnp.arange(n_lat)
    pos = jnp.stack([t // GRID_W, t % GRID_W], axis=-1).astype(F32)
    n_freq = HEAD_DIM // 4
    inv = ROPE_THETA ** (-jnp.arange(n_freq, dtype=F32) / n_freq)
    ang = pos[:, :, None] * inv
    c, s = jnp.cos(ang), jnp.sin(ang)
    cos = jnp.concatenate([c[:, 0], c[:, 0], c[:, 1], c[:, 1]], axis=-1)
    sin = jnp.concatenate([-s[:, 0], s[:, 0], -s[:, 1], s[:, 1]], axis=-1)
    cos = jnp.concatenate([jnp.ones((n_ctx, HEAD_DIM), F32), cos], axis=0)
    sin = jnp.concatenate([jnp.zeros((n_ctx, HEAD_DIM), F32), sin], axis=0)
    return cos, sin


def _qk_prep_kernel(x_ref, cos_ref, sin_ref, qg_ref, kg_ref, qt_ref, ko_ref, vt_ref):
    cos = cos_ref[...]
    sin = sin_ref[...]
    lane = lax.broadcasted_iota(jnp.int32, (1, HEAD_DIM), 1)
    first_half = (lane % (HEAD_DIM // 2)) < (HEAD_DIM // 4)
    for h in range(GQA_HEADS + GQA_KV_HEADS):
        x = x_ref[:, h * HEAD_DIM:(h + 1) * HEAD_DIM]
        gain = qg_ref[...] if h < GQA_HEADS else kg_ref[...]
        y = x * lax.rsqrt(jnp.mean(x * x, axis=-1, keepdims=True) + RMS_EPS) * gain
        sw = jnp.where(first_half, pltpu.roll(y, HEAD_DIM - HEAD_DIM // 4, 1), pltpu.roll(y, HEAD_DIM // 4, 1))
        out = y * cos + sw * sin
        if h < GQA_HEADS:
            qt_ref[h * HEAD_DIM:(h + 1) * HEAD_DIM, :] = out.T.astype(BF16)
        else:
            hk = h - GQA_HEADS
            ko_ref[:, hk * HEAD_DIM:(hk + 1) * HEAD_DIM] = out.astype(BF16)
    v0 = (GQA_HEADS + GQA_KV_HEADS) * HEAD_DIM
    for hv in range(GQA_KV_HEADS):
        v = x_ref[:, v0 + hv * HEAD_DIM:v0 + (hv + 1) * HEAD_DIM]
        vt_ref[hv * HEAD_DIM:(hv + 1) * HEAD_DIM, :] = v.T.astype(BF16)


def qk_prep(fr, cos, sin, q_gain, k_gain, tm=768):
    m = fr.shape[0]
    wq, wk = GQA_HEADS * HEAD_DIM, GQA_KV_HEADS * HEAD_DIM
    return pl.pallas_call(
        _qk_prep_kernel,
        grid=(m // tm,),
        in_specs=[pl.BlockSpec((tm, wq + 2 * wk), lambda i: (i, 0)),
                  pl.BlockSpec((tm, HEAD_DIM), lambda i: (i, 0)),
                  pl.BlockSpec((tm, HEAD_DIM), lambda i: (i, 0)),
                  pl.BlockSpec((1, HEAD_DIM), lambda i: (0, 0)),
                  pl.BlockSpec((1, HEAD_DIM), lambda i: (0, 0))],
        out_specs=[pl.BlockSpec((wq, tm), lambda i: (0, i)),
                   pl.BlockSpec((tm, wk), lambda i: (i, 0)),
                   pl.BlockSpec((wk, tm), lambda i: (0, i))],
        out_shape=[jax.ShapeDtypeStruct((wq, m), BF16),
                   jax.ShapeDtypeStruct((m, wk), BF16),
                   jax.ShapeDtypeStruct((wk, m), BF16)],
        compiler_params=_cparams(("arbitrary",)),
        name="qk_prep",
    )(fr, cos, sin, q_gain.reshape(1, HEAD_DIM), k_gain.reshape(1, HEAD_DIM))


def _gqa_kernel(qt_ref, k_ref, vt_ref, o_ref, *, n_ctx, n_all, tq, tk):
    qi = pl.program_id(1)
    c1 = (HEAD_DIM ** -0.5) * LOG2E
    n_ctx_tiles = n_ctx // tq
    rep = GQA_HEADS // GQA_KV_HEADS

    def attend(n_keys, tkk):
        nchunks = n_keys // tkk

        def scores(c):
            kc = k_ref[c * tkk:(c + 1) * tkk, :]
            return [jnp.dot(kc, qt_ref[r * HEAD_DIM:(r + 1) * HEAD_DIM, :], preferred_element_type=F32)
                    for r in range(rep)]

        state = [None] * rep
        nxt = scores(0)
        for c in range(nchunks):
            cur = nxt
            if c + 1 < nchunks:
                nxt = scores(c + 1)
            vtc = vt_ref[:, c * tkk:(c + 1) * tkk]
            for r in range(rep):
                st = cur[r]
                mc = jnp.max(st, axis=0, keepdims=True)
                if state[r] is None:
                    m_new = mc
                    p = jnp.exp2(st * c1 - m_new * c1)
                    l = jnp.sum(p, axis=0, keepdims=True)
                    acc = jnp.dot(vtc, p.astype(BF16), preferred_element_type=F32)
                else:
                    m, l, acc = state[r]
                    m_new = jnp.maximum(m, mc)
                    p = jnp.exp2(st * c1 - m_new * c1)
                    alpha = jnp.exp2((m - m_new) * c1)
                    l = alpha * l + jnp.sum(p, axis=0, keepdims=True)
                    acc = alpha * acc + jnp.dot(vtc, p.astype(BF16), preferred_element_type=F32)
                state[r] = (m_new, l, acc)
        for r in range(rep):
            _, l, acc = state[r]
            o_ref[:, r * HEAD_DIM:(r + 1) * HEAD_DIM] = (acc / l).T.astype(o_ref.dtype)

    @pl.when(qi < n_ctx_tiles)
    def _ctx():
        attend(n_ctx, n_ctx)

    @pl.when(qi >= n_ctx_tiles)
    def _lat():
        attend(n_all, tk)


def gqa_attention(qt, kn, vt, n_ctx, tq=256, tk=1408):
    m = kn.shape[0]
    rep = GQA_HEADS // GQA_KV_HEADS
    if m % tk:
        tk = 256
    return pl.pallas_call(
        functools.partial(_gqa_kernel, n_ctx=n_ctx, n_all=m, tq=tq, tk=tk),
        grid=(GQA_KV_HEADS, m // tq),
        in_specs=[pl.BlockSpec((rep * HEAD_DIM, tq), lambda g, i: (g, i)),
                  pl.BlockSpec((m, HEAD_DIM), lambda g, i: (0, g)),
                  pl.BlockSpec((HEAD_DIM, m), lambda g, i: (g, 0))],
        out_specs=pl.BlockSpec((tq, rep * HEAD_DIM), lambda g, i: (i, g)),
        out_shape=jax.ShapeDtypeStruct((m, GQA_HEADS * HEAD_DIM), BF16),
        compiler_params=_cparams(("arbitrary", "arbitrary")),
        name="gqa_attn",
    )(qt, kn, vt)


def _ssd_prep_kernel(prev_ref, x_ref, next_ref, dtr_ref, cw_ref, cb_ref, dtb_ref, u_ref, dt_ref, *, n_ctx, n_all, tm):
    i = pl.program_id(0)
    lo = i * tm
    hi = lo + tm
    top_ok = jnp.logical_and(lo != 0, lo != n_ctx)
    bot_ok = jnp.logical_and(hi != n_ctx, hi != n_all)
    prev = jnp.where(top_ok, prev_ref[...], 0.0)
    nxt = jnp.where(bot_ok, next_ref[...], 0.0)
    ext = jnp.concatenate([prev, x_ref[...], nxt], axis=0)
    half = SSD_CONV // 2
    acc = None
    for j in range(SSD_CONV):
        sl = ext[8 - half + j:8 - half + j + tm, :]
        term = sl * cw_ref[j:j + 1, :]
        acc = term if acc is None else acc + term
    u_ref[...] = _silu(acc + cb_ref[...])
    dt_ref[...] = jax.nn.softplus(dtr_ref[...] + dtb_ref[...])


def ssd_prep(fzx, fr, conv_w, conv_b, dt_bias, n_ctx, tm=256):
    m = fzx.shape[0]
    nb8 = tm // 8
    last8 = m // 8 - 1
    cw = jnp.concatenate([conv_w, jnp.zeros((8 - SSD_CONV, SSD_XBC_W), F32)], axis=0)
    dtb = jnp.concatenate([dt_bias.reshape(-1), jnp.zeros((LANES - 2 * SSD_HEADS,), F32)]).reshape(1, LANES)
    return pl.pallas_call(
        functools.partial(_ssd_prep_kernel, n_ctx=n_ctx, n_all=m, tm=tm),
        grid=(m // tm,),
        in_specs=[pl.BlockSpec((8, SSD_XBC_W), lambda i: (jnp.maximum(i * nb8 - 1, 0), 0)),
                  pl.BlockSpec((tm, SSD_XBC_W), lambda i: (i, 0)),
                  pl.BlockSpec((8, SSD_XBC_W), lambda i: (jnp.minimum((i + 1) * nb8, last8), 0)),
                  pl.BlockSpec((tm, LANES), lambda i: (i, 0)),
                  pl.BlockSpec((8, SSD_XBC_W), lambda i: (0, 0)),
                  pl.BlockSpec((1, SSD_XBC_W), lambda i: (0, 0)),
                  pl.BlockSpec((1, LANES), lambda i: (0, 0))],
        out_specs=[pl.BlockSpec((tm, SSD_XBC_W), lambda i: (i, 0)),
                   pl.BlockSpec((tm, LANES), lambda i: (i, 0))],
        out_shape=[jax.ShapeDtypeStruct((m, SSD_XBC_W), F32),
                   jax.ShapeDtypeStruct((m, LANES), F32)],
        compiler_params=_cparams(("arbitrary",)),
        name="ssd_prep",
    )(fzx, fzx, fzx, fr, cw, conv_b.reshape(1, SSD_XBC_W), dtb)


def _ssd_chunk(u_ref, dt_ref, alog_ref, y_ref, ht_ref, d):
    ln = SSD_CHUNK
    p = SSD_HEAD_DIM
    ns = SSD_STATE
    epg = SSD_HEADS // SSD_GROUPS
    dt = dt_ref[...]
    if d == 1:
        dt = pltpu.roll(dt, LANES - SSD_HEADS, 1)
    a = dt * (-jnp.exp(alog_ref[d:d + 1, :]))
    li = lax.broadcasted_iota(jnp.int32, (ln, ln), 0)
    si = lax.broadcasted_iota(jnp.int32, (ln, ln), 1)
    mask = (li >= si) if d == 0 else (li <= si)
    a_cum = jnp.dot(mask.astype(F32), a, preferred_element_type=F32, precision=lax.Precision.HIGHEST)
    a_cum_t = a_cum.T
    a_tot = a_cum[ln - 1:ln, :] if d == 0 else a_cum[0:1, :]
    w_all = jnp.exp(a_tot - a_cum)
    ea_all = jnp.exp(a_cum)
    eat = jnp.exp(a_tot)
    for g in range(SSD_GROUPS):
        bg = u_ref[:, SSD_W + g * ns:SSD_W + (g + 1) * ns]
        cg = u_ref[:, SSD_W + SSD_GROUPS * ns + g * ns:SSD_W + SSD_GROUPS * ns + (g + 1) * ns]
        cgb = cg.astype(BF16)
        cb = _qk(cgb, bg.astype(BF16))
        bgt = bg.T.astype(BF16)
        for e in range(epg):
            h = g * epg + e
            ac = a_cum[:, h:h + 1]
            act = a_cum_t[h:h + 1, :]
            decay = jnp.exp(jnp.where(mask, ac - act, -jnp.inf))
            mm = (cb * decay).astype(BF16)
            xdt = u_ref[:, h * p:(h + 1) * p] * dt[:, h:h + 1]
            ht = ht_ref[d, h]
            y = (jnp.dot(mm, xdt.astype(BF16), preferred_element_type=F32)
                 + ea_all[:, h:h + 1] * jnp.dot(cgb, ht.astype(BF16), preferred_element_type=F32))
            st = jnp.dot(bgt, (xdt * w_all[:, h:h + 1]).astype(BF16), preferred_element_type=F32)
            ht_ref[d, h] = eat[:, h:h + 1] * ht + st
            y_ref[:, h * p:(h + 1) * p] = y


def _ssd_scan_kernel(uf_ref, dtf_ref, ub_ref, dtb_ref, alog_ref, yf_ref, yb_ref, ht_ref):
    @pl.when(pl.program_id(0) == 0)
    def _init():
        ht_ref[...] = jnp.zeros_like(ht_ref)

    _ssd_chunk(uf_ref, dtf_ref, alog_ref, yf_ref, ht_ref, 0)
    _ssd_chunk(ub_ref, dtb_ref, alog_ref, yb_ref, ht_ref, 1)


def ssd_scan(u, dt, a_log, n_ctx):
    m = u.shape[0]
    nc = m // SSD_CHUNK
    ncc = n_ctx // SSD_CHUNK

    def bwd_chunk(s):
        return jnp.where(s < ncc, ncc - 1 - s, ncc + nc - 1 - s)

    return pl.pallas_call(
        _ssd_scan_kernel,
        grid=(nc,),
        in_specs=[pl.BlockSpec((SSD_CHUNK, SSD_XBC_W), lambda s: (s, 0)),
                  pl.BlockSpec((SSD_CHUNK, LANES), lambda s: (s, 0)),
                  pl.BlockSpec((SSD_CHUNK, SSD_XBC_W), lambda s: (bwd_chunk(s), 0)),
                  pl.BlockSpec((SSD_CHUNK, LANES), lambda s: (bwd_chunk(s), 0)),
                  pl.BlockSpec((2, LANES), lambda s: (0, 0))],
        out_specs=[pl.BlockSpec((SSD_CHUNK, SSD_W), lambda s: (s, 0)),
                   pl.BlockSpec((SSD_CHUNK, SSD_W), lambda s: (bwd_chunk(s), 0))],
        out_shape=[jax.ShapeDtypeStruct((m, SSD_W), F32), jax.ShapeDtypeStruct((m, SSD_W), F32)],
        scratch_shapes=[pltpu.VMEM((2, SSD_HEADS, SSD_STATE, SSD_HEAD_DIM), F32)],
        compiler_params=_cparams(("arbitrary",)),
        name="ssd_scan",
    )(u, dt, u, dt, jnp.concatenate([a_log, jnp.zeros((2, LANES - SSD_HEADS), F32)], axis=1))


def _ssd_out_kernel(yf_ref, yb_ref, xs_ref, z_ref, dsk_ref, nw_ref, o_ref):
    y = yf_ref[...] + yb_ref[...] + dsk_ref[...] * xs_ref[...]
    gated = y * _silu(z_ref[...])
    gw = SSD_W // SSD_GROUPS
    for g in range(SSD_GROUPS):
        blk = gated[:, g * gw:(g + 1) * gw]
        nrm = blk * lax.rsqrt(jnp.mean(blk * blk, axis=-1, keepdims=True) + RMS_EPS)
        o_ref[:, g * gw:(g + 1) * gw] = (nrm * nw_ref[:, g * gw:(g + 1) * gw]).astype(o_ref.dtype)


def ssd_out(yf, yb, u, fb, d_skip, norm_w, tm=768):
    m = u.shape[0]
    dsk = jnp.repeat(d_skip.astype(F32), SSD_HEAD_DIM).reshape(1, SSD_W)
    return pl.pallas_call(
        _ssd_out_kernel,
        grid=(m // tm,),
        in_specs=[pl.BlockSpec((tm, SSD_W), lambda i: (i, 0)),
                  pl.BlockSpec((tm, SSD_W), lambda i: (i, 0)),
                  pl.BlockSpec((tm, SSD_W), lambda i: (i, 0)),
                  pl.BlockSpec((tm, SSD_W), lambda i: (i, ZX_Z // SSD_W)),
                  pl.BlockSpec((1, SSD_W), lambda i: (0, 0)),
                  pl.BlockSpec((1, SSD_W), lambda i: (0, 0))],
        out_specs=pl.BlockSpec((tm, SSD_W), lambda i: (i, 0)),
        out_shape=jax.ShapeDtypeStruct((m, SSD_W), BF16),
        compiler_params=_cparams(("arbitrary",)),
        name="ssd_out",
    )(yf, yb, u, fb, dsk, norm_w.reshape(1, SSD_W))


def _dft_tables(n):
    ang = 2.0 * np.pi * np.outer(np.arange(n), np.arange(n)) / n
    return jnp.asarray(np.cos(ang), F32), jnp.asarray(np.sin(ang), F32)


def _split_bf16(a):
    hi = a.astype(BF16)
    return hi, (a - hi.astype(F32)).astype(BF16)


def _hdot(a, b):
    ah, al = _split_bf16(a)
    bh, bl = _split_bf16(b)
    return (jnp.dot(ah, bh, preferred_element_type=F32)
            + (jnp.dot(ah, bl, preferred_element_type=F32) + jnp.dot(al, bh, preferred_element_type=F32)))


def _fnet_a_kernel(c_ref, s_ref, x_ref, yr_ref, yi_ref):
    x = x_ref[...]
    yr_ref[...] = _hdot(c_ref[...], x)
    yi_ref[...] = -_hdot(s_ref[...], x)


def _fnet_b_kernel(yr_ref, yi_ref, twc_ref, tws_ref, c1_ref, s1_ref, cc_ref, sc_ref, o_ref, *, scale, kb):
    c1, s1 = c1_ref[...], s1_ref[...]
    cc, sc = cc_ref[...], sc_ref[...]
    for j in range(kb):
        yr, yi = yr_ref[j], yi_ref[j]
        tc, ts = twc_ref[j], tws_ref[j]
        pr = yr * tc + yi * ts
        pi = yi * tc - yr * ts
        zr = _hdot(c1, pr) + _hdot(s1, pi)
        zi = _hdot(c1, pi) - _hdot(s1, pr)
        for g in range(FNET_GROUPS):
            cs = slice(g * FNET_GROUP_DIM, (g + 1) * FNET_GROUP_DIM)
            o_ref[:, j, cs] = (_hdot(zr[:, cs], cc) + _hdot(zi[:, cs], sc)) * scale


def _fnet_ctx_kernel(x_ref, cn_ref, sn_ref, cc_ref, sc_ref, o_ref, *, scale):
    x = x_ref[...]
    wr = _hdot(cn_ref[...], x)
    ws = _hdot(sn_ref[...], x)
    cc, sc = cc_ref[...], sc_ref[...]
    for g in range(FNET_GROUPS):
        cs = slice(g * FNET_GROUP_DIM, (g + 1) * FNET_GROUP_DIM)
        o_ref[:, cs] = (_hdot(wr[:, cs], cc) - _hdot(ws[:, cs], sc)) * scale


def fourier_mix(f_lat, f_ctx):
    n_lat, w = f_lat.shape
    n_ctx = f_ctx.shape[0]
    n2 = 128
    n1 = n_lat // n2
    c2, s2 = _dft_tables(n2)
    c1, s1 = _dft_tables(n1)
    cc, sc = _dft_tables(FNET_GROUP_DIM)
    tw = 2.0 * np.pi * np.outer(np.arange(n2), np.arange(n1)) / n_lat
    twc = jnp.asarray(np.cos(tw), F32).reshape(n2, n1, 1)
    tws = jnp.asarray(np.sin(tw), F32).reshape(n2, n1, 1)
    xr = f_lat.reshape(n2, n1 * w)
    tn = min(4096, n1 * w)
    yr, yi = pl.pallas_call(
        _fnet_a_kernel,
        grid=(n1 * w // tn,),
        in_specs=[pl.BlockSpec((n2, n2), lambda j: (0, 0)),
                  pl.BlockSpec((n2, n2), lambda j: (0, 0)),
                  pl.BlockSpec((n2, tn), lambda j: (0, j))],
        out_specs=[pl.BlockSpec((n2, tn), lambda j: (0, j)),
                   pl.BlockSpec((n2, tn), lambda j: (0, j))],
        out_shape=[jax.ShapeDtypeStruct((n2, n1 * w), F32)] * 2,
        compiler_params=_cparams(("arbitrary",)),
        name="fnet_stage_a",
    )(c2, s2, xr)
    kb = 8
    lat = pl.pallas_call(
        functools.partial(_fnet_b_kernel, scale=float(1.0 / math.sqrt(n_lat * FNET_GROUP_DIM)), kb=kb),
        grid=(n2 // kb,),
        in_specs=[pl.BlockSpec((kb, n1, w), lambda j: (j, 0, 0)),
                  pl.BlockSpec((kb, n1, w), lambda j: (j, 0, 0)),
                  pl.BlockSpec((kb, n1, 1), lambda j: (j, 0, 0)),
                  pl.BlockSpec((kb, n1, 1), lambda j: (j, 0, 0)),
                  pl.BlockSpec((n1, n1), lambda j: (0, 0)),
                  pl.BlockSpec((n1, n1), lambda j: (0, 0)),
                  pl.BlockSpec((FNET_GROUP_DIM, FNET_GROUP_DIM), lambda j: (0, 0)),
                  pl.BlockSpec((FNET_GROUP_DIM, FNET_GROUP_DIM), lambda j: (0, 0))],
        out_specs=pl.BlockSpec((n1, kb, w), lambda j: (0, j, 0)),
        out_shape=jax.ShapeDtypeStruct((n1, n2, w), F32),
        compiler_params=_cparams(("arbitrary",)),
        name="fnet_stage_b",
    )(yr.reshape(n2, n1, w), yi.reshape(n2, n1, w), twc, tws, c1, s1, cc, sc)
    cn, sn = _dft_tables(n_ctx)
    ctx = pl.pallas_call(
        functools.partial(_fnet_ctx_kernel, scale=float(1.0 / math.sqrt(n_ctx * FNET_GROUP_DIM))),
        out_shape=jax.ShapeDtypeStruct((n_ctx, w), F32),
        compiler_params=pltpu.CompilerParams(vmem_limit_bytes=VMEM_LIMIT_BYTES),
        name="fnet_ctx",
    )(f_ctx, cn, sn, cc, sc)
    return jnp.concatenate([ctx, lat.reshape(n_lat, w)], axis=0)


def _merge_kernel(h_ref, *refs):
    wg_refs, o_refs = refs[0:4], refs[4:8]
    wbr_ref, y_ref, ws_ref = refs[8], refs[9], refs[10]

    @pl.when(pl.program_id(1) == 0)
    def _convert():
        for b in range(N_BRANCH):
            ws_ref[b] = wg_refs[b][0].astype(BF16)

    h = h_ref[...]
    y = None
    for b in range(N_BRANCH):
        g = lax.dot_general(h, ws_ref[b], _NT, preferred_element_type=F32)
        pr = jnp.dot(o_refs[b][...].astype(BF16), wbr_ref[0, b].astype(BF16), preferred_element_type=F32)
        t = jax.nn.sigmoid(g) * pr
        y = t if y is None else y + t
    y_ref[...] = y.astype(y_ref.dtype)


def merge(h, wt_all, gate_col0, branches, w_branch_all, layer, tm=768, tn=256):
    m, k = h.shape
    nb = D_MODEL // tn
    wg_specs = [_wt_block(layer, gate_col0 + b * D_MODEL, tn, k) for b in range(N_BRANCH)]
    o_specs = [pl.BlockSpec((tm, BRANCH_W), lambda j, i: (i, 0)) for _ in range(N_BRANCH)]
    return pl.pallas_call(
        _merge_kernel,
        grid=(nb, m // tm),
        in_specs=([pl.BlockSpec((tm, k), lambda j, i: (i, 0))] + wg_specs + o_specs
                  + [pl.BlockSpec((1, N_BRANCH, BRANCH_W, tn), lambda j, i: (layer, 0, 0, j))]),
        out_specs=pl.BlockSpec((tm, tn), lambda j, i: (i, j)),
        out_shape=jax.ShapeDtypeStruct((m, D_MODEL), BF16),
        scratch_shapes=[pltpu.VMEM((N_BRANCH, tn, k), BF16)],
        compiler_params=_cparams(("arbitrary", "arbitrary")),
        name="merge",
    )(h, *([wt_all] * N_BRANCH), *branches, w_branch_all)


def _expert_kernel(idx_ref, h_hbm, wg_ref, wu_ref, wd_ref, gate_ref, o_ref, xs_ref, sem, *, nf):
    e = pl.program_id(0)
    f = pl.program_id(1)
    cap = xs_ref.shape[1]
    slot = e % 2

    def start_gather(ee, sl):
        def issue(c, carry):
            row = idx_ref[ee, c]
            pltpu.make_async_copy(h_hbm.at[pl.ds(row, 1)], xs_ref.at[sl, pl.ds(c, 1)], sem.at[sl]).start()
            return carry

        lax.fori_loop(0, cap, issue, 0, unroll=8)

    @pl.when(jnp.logical_and(e == 0, f == 0))
    def _cold_start():
        start_gather(0, 0)

    @pl.when(f == 0)
    def _wait_rows():
        pltpu.make_async_copy(h_hbm.at[pl.ds(0, cap)], xs_ref.at[slot], sem.at[slot]).wait()

    @pl.when(jnp.logical_and(f == 0, e + 1 < pl.num_programs(0)))
    def _prefetch_next():
        start_gather(e + 1, 1 - slot)

    xs = xs_ref[slot].astype(BF16)
    hg = jnp.dot(xs, wg_ref[0, 0].astype(BF16), preferred_element_type=F32)
    hu = jnp.dot(xs, wu_ref[0, 0].astype(BF16), preferred_element_type=F32)
    hid = (_silu(hg) * hu).astype(BF16)
    part = jnp.dot(hid, wd_ref[0, 0].astype(BF16), preferred_element_type=F32)

    @pl.when(f == 0)
    def _first():
        o_ref[0] = part

    @pl.when(f != 0)
    def _rest():
        o_ref[0] += part

    @pl.when(f == nf - 1)
    def _done():
        o_ref[0] = o_ref[0] * gate_ref[0]


def expert_ffn(h2, idx, gate, w_gate, w_up, w_down, layer, tf=256):
    ne, cap = idx.shape
    d = h2.shape[1]
    ff = w_gate.shape[3]
    grid_spec = pltpu.PrefetchScalarGridSpec(
        num_scalar_prefetch=1,
        grid=(ne, ff // tf),
        in_specs=[pl.BlockSpec(memory_space=pl.ANY),
                  pl.BlockSpec((1, 1, d, tf), lambda e, f, ix: (layer, e, 0, f)),
                  pl.BlockSpec((1, 1, d, tf), lambda e, f, ix: (layer, e, 0, f)),
                  pl.BlockSpec((1, 1, tf, d), lambda e, f, ix: (layer, e, f, 0)),
                  pl.BlockSpec((1, cap, 1), lambda e, f, ix: (e, 0, 0))],
        out_specs=pl.BlockSpec((1, cap, d), lambda e, f, ix: (e, 0, 0)),
        scratch_shapes=[pltpu.VMEM((2, cap, d), F32), pltpu.SemaphoreType.DMA((2,))],
    )
    return pl.pallas_call(
        functools.partial(_expert_kernel, nf=ff // tf),
        grid_spec=grid_spec,
        out_shape=jax.ShapeDtypeStruct((ne, cap, d), F32),
        compiler_params=_cparams(("arbitrary", "arbitrary")),
        name="expert_ffn",
    )(idx, h2, w_gate, w_up, w_down, gate.reshape(ne, cap, 1))


COMBINE_TOKENS = 64
COMBINE_WAIT_ROWS = 8


def _combine_kernel(src_ref, dst_ref, tstart_ref, rmax_ref, ye_hbm, x_ref, g_ref, o_ref, s_ref, sem, *, n_ctx,
                    ntiles):
    t = pl.program_id(0)
    tt = COMBINE_TOKENS
    d = s_ref.shape[2]
    buf = t % 2

    def fill(tile, b):
        def zero(r, carry):
            s_ref[b, pl.ds(pl.multiple_of(r * tt, tt), tt), :] = jnp.zeros((tt, d), F32)
            return carry

        lax.fori_loop(0, rmax_ref[tile], zero, 0)

        def issue(p, carry):
            pltpu.make_async_copy(ye_hbm.at[pl.ds(src_ref[p], 1)], s_ref.at[b, pl.ds(dst_ref[p], 1)],
                                  sem.at[b]).start()
            return carry

        lax.fori_loop(tstart_ref[tile], tstart_ref[tile + 1], issue, 0)

    @pl.when(t == 0)
    def _first():
        fill(0, 0)

    @pl.when(t + 1 < ntiles)
    def _next():
        fill(t + 1, 1 - buf)

    def wait_rows(nrows):
        def wait(p, carry):
            pltpu.make_async_copy(ye_hbm.at[pl.ds(0, nrows)], s_ref.at[buf, pl.ds(0, nrows)], sem.at[buf]).wait()
            return carry
        return wait

    npairs = tstart_ref[t + 1] - tstart_ref[t]
    lax.fori_loop(0, npairs // COMBINE_WAIT_ROWS, wait_rows(COMBINE_WAIT_ROWS), 0)
    lax.fori_loop(0, npairs % COMBINE_WAIT_ROWS, wait_rows(1), 0)

    def add(r, acc):
        return acc + s_ref[buf, pl.ds(pl.multiple_of(r * tt, tt), tt), :]

    acc = lax.fori_loop(0, rmax_ref[t], add, jnp.zeros((tt, d), F32))
    row = t * tt + lax.broadcasted_iota(jnp.int32, (tt, 1), 0)
    g = jnp.where(row < n_ctx, g_ref[1:2, :], g_ref[0:1, :])
    o_ref[...] = x_ref[...] + g * acc


def moe_combine(x, ye, idx, mod, gate_chunk, n_ctx):
    m, d = x.shape
    ne, cap = idx.shape
    npairs = ne * cap
    tt = COMBINE_TOKENS
    ntiles = m // tt
    tok = idx.reshape(-1)
    tok_s, src_s = lax.sort((tok, jnp.arange(npairs, dtype=jnp.int32)), num_keys=1)
    pos = jnp.arange(npairs, dtype=jnp.int32)
    is_first = jnp.concatenate([jnp.ones((1,), bool), tok_s[1:] != tok_s[:-1]])
    first = lax.cummax(jnp.where(is_first, pos, 0), axis=0)
    rank = pos - first
    dst = rank * tt + tok_s % tt
    tile_s = tok_s // tt
    tiles = jnp.arange(ntiles + 1, dtype=jnp.int32)
    tstart = jnp.sum((tile_s[None, :] < tiles[:, None]).astype(jnp.int32), axis=1)
    rmax = jnp.max(jnp.where(tile_s[None, :] == tiles[:ntiles, None], rank[None, :] + 1, 0), axis=1)
    grid_spec = pltpu.PrefetchScalarGridSpec(
        num_scalar_prefetch=4,
        grid=(ntiles,),
        in_specs=[pl.BlockSpec(memory_space=pl.ANY),
                  pl.BlockSpec((tt, d), lambda t, *_: (t, 0)),
                  pl.BlockSpec((8, d), lambda t, *_: (0, gate_chunk))],
        out_specs=pl.BlockSpec((tt, d), lambda t, *_: (t, 0)),
        scratch_shapes=[pltpu.VMEM((2, ne * tt, d), F32), pltpu.SemaphoreType.DMA((2,))],
    )
    return pl.pallas_call(
        functools.partial(_combine_kernel, n_ctx=n_ctx, ntiles=ntiles),
        grid_spec=grid_spec,
        out_shape=jax.ShapeDtypeStruct((m, d), F32),
        compiler_params=_cparams(("arbitrary",)),
        name="moe_combine",
    )(src_s, dst, tstart, rmax, ye.reshape(npairs, d), x, mod)


def moe(x, h2, aff, mod, w_gate, w_up, w_down, layer, n_ctx):
    m = h2.shape[0]
    n_lat = m - n_ctx
    a = aff[:, :N_EXPERTS]
    g_c, i_c = lax.top_k(a[:n_ctx].T, CAPACITY_FACTOR * n_ctx // N_EXPERTS)
    g_l, i_l = lax.top_k(a[n_ctx:].T, CAPACITY_FACTOR * n_lat // N_EXPERTS)
    idx = jnp.concatenate([i_c, i_l + n_ctx], axis=1).astype(jnp.int32)
    gate = jnp.concatenate([g_c, g_l], axis=1)
    ye = expert_ffn(h2, idx, gate, w_gate, w_up, w_down, layer)
    return moe_combine(x, ye, idx, mod, 5, n_ctx)


_P_NA = 0
_P_Z = _P_NA + 3 * NA_HEADS * HEAD_DIM
_P_XBC = _P_Z + SSD_W
_P_DT = _P_XBC + SSD_XBC_W
_P_GQ = _P_DT + 2 * SSD_HEADS
_P_FN = _P_GQ + (GQA_HEADS + 2 * GQA_KV_HEADS) * HEAD_DIM
_P_GATES = _P_FN + FNET_GROUPS * FNET_GROUP_DIM


def _layer(x, mod, n_ctx, layer, norm_mix, norm_ffn, wt_in_all, na_rpb, conv_w, conv_b, a_log, dt_bias, d_skip,
           ssd_gn, q_gain, k_gain, w_branch_all, w_out_all, w_router, w_gate_all, w_up_all, w_down_all, rope):
    m = x.shape[0]
    tm = 768 if m % 768 == 0 else 256
    h = norm_mod(x, norm_mix, mod, n_ctx, 0, tm=tm)
    ab = matmul_wt(h, wt_in_all, layer, _P_NA, _P_Z - _P_NA, BF16, tm, 768, "proj_na")
    fzx = matmul_wt(h, wt_in_all, layer, _P_Z, _P_DT - _P_Z, F32, tm, 512, "proj_zx", out_perm=(2, 0, 1))
    fr = matmul_wt(h, wt_in_all, layer, _P_GQ, _P_GATES - _P_GQ, F32, tm, 512, "proj_rest")
    fdt = matmul_wt(h, wt_in_all, layer, _P_DT, LANES, F32, tm, LANES, "proj_dt")

    rows = (m - n_ctx) // GRID_W
    bias, tid = na_bias_table(na_rpb, rows)
    o_na = na_attention(ab, bias, tid, n_ctx)

    u, dt = ssd_prep(fzx, fdt, conv_w, conv_b, dt_bias, n_ctx)
    yf, yb = ssd_scan(u, dt, a_log, n_ctx)
    o_ssd = ssd_out(yf, yb, u, fzx, d_skip, ssd_gn, tm=tm)

    qt, kn, vt = qk_prep(fr, rope[0], rope[1], q_gain, k_gain, tm=tm)
    o_gqa = gqa_attention(qt, kn, vt, n_ctx)

    f_in = fr[:, FR_FN:FR_FN + FNET_GROUPS * FNET_GROUP_DIM]
    o_fn = fourier_mix(f_in[n_ctx:], f_in[:n_ctx])

    y = merge(h, wt_in_all, _P_GATES, (o_na, o_ssd, o_gqa, o_fn), w_branch_all, layer, tm=tm)
    x = matmul_residual(y, w_out_all, layer, x, mod, 2, n_ctx, tm, 1024, "out_proj")

    wr = jnp.concatenate([w_router, jnp.zeros((D_MODEL, LANES - N_EXPERTS), w_router.dtype)], axis=1)
    h2, aff = norm_router(x, norm_ffn, mod, wr, n_ctx, 3, tm=tm)
    return moe(x, h2, aff, mod, w_gate_all, w_up_all, w_down_all, layer, n_ctx)


def kernel(x, c, ctx, c_ctx, w_ada, b_ada, norm_mix, norm_ffn, w_in, na_rpb, ssd_conv_w, ssd_conv_b, ssd_a_log,
           ssd_dt_bias, ssd_d, ssd_norm, gqa_q_norm, gqa_k_norm, w_branch, w_out, w_router, moe_w_gate, moe_w_up,
           moe_w_down, final_norm_w):
    n_lat = x.shape[1]
    n_ctx = ctx.shape[1]
    depth = w_ada.shape[0]
    cc = jnp.concatenate([c[0:1], c_ctx[None], jnp.zeros((6, D_MODEL), F32)], axis=0)
    mods = ada_all(cc, w_ada, b_ada)
    rope = rope_tables(n_ctx, n_lat)
    xs = jnp.concatenate([ctx[0], x[0]], axis=0)
    wt_in = jnp.swapaxes(w_in, 1, 2)
    for l in range(depth):
        xs = _layer(xs, mods[l], n_ctx, l, norm_mix[l], norm_ffn[l], wt_in, na_rpb[l], ssd_conv_w[l],
                    ssd_conv_b[l], ssd_a_log[l], ssd_dt_bias[l], ssd_d[l], ssd_norm[l], gqa_q_norm[l],
                    gqa_k_norm[l], w_branch, w_out, w_router[l], moe_w_gate, moe_w_up, moe_w_down, rope)
    out = final_norm(xs, final_norm_w, n_ctx, n_lat)
    return out[None]
```

```python
import functools
import math

import numpy as np
import jax
import jax.numpy as jnp
from jax import lax
from jax.experimental import pallas as pl
from jax.experimental.pallas import tpu as pltpu

F32 = jnp.float32
BF16 = jnp.bfloat16

D_MODEL = 2048
GRID_W = 64
HEAD_DIM = 128
RMS_EPS = 1e-6
N_BRANCH = 4
BRANCH_W = 512
NA_HEADS = 4
NA_WIN_R = 8
NA_WIN_C = 16
SSD_HEADS = 8
SSD_HEAD_DIM = 64
SSD_GROUPS = 2
SSD_STATE = 128
SSD_CONV = 5
SSD_CHUNK = 128
SSD_W = SSD_HEADS * SSD_HEAD_DIM
SSD_XBC_W = SSD_W + 2 * SSD_GROUPS * SSD_STATE
GQA_HEADS = 4
GQA_KV_HEADS = 2
ROPE_THETA = 10000.0
FNET_GROUPS = 4
FNET_GROUP_DIM = 128
N_EXPERTS = 16
EXPERT_FF = D_MODEL // 2
CAPACITY_FACTOR = 2
LOG2E = 1.4426950408889634

VMEM_LIMIT_BYTES = 56 * 1024 * 1024
LANES = 128

ZX_XBC, ZX_Z, ZX_W = 0, 1024, 1536
FR_Q, FR_K, FR_V, FR_FN, FR_W = 0, 512, 768, 1024, 1536


def _cparams(sem):
    return pltpu.CompilerParams(dimension_semantics=sem, vmem_limit_bytes=VMEM_LIMIT_BYTES)


def _silu(x):
    return x * jax.nn.sigmoid(x)


def _ada_kernel(ct_ref, w_ref, b_ref, o_ref):
    a = _silu(ct_ref[...])
    w = w_ref[0]
    rows = [jnp.sum(w * a[:, r:r + 1], axis=0, keepdims=True) + b_ref[0] for r in range(2)]
    o_ref[0] = jnp.concatenate(rows + [jnp.zeros((6, w.shape[1]), F32)], axis=0)


def ada_all(cc, w_ada, b_ada):
    depth, d, n = w_ada.shape
    tn = 1024
    return pl.pallas_call(
        _ada_kernel,
        grid=(depth, n // tn),
        in_specs=[pl.BlockSpec((d, 8), lambda l, j: (0, 0)),
                  pl.BlockSpec((1, d, tn), lambda l, j: (l, 0, j)),
                  pl.BlockSpec((1, 1, tn), lambda l, j: (l, 0, j))],
        out_specs=pl.BlockSpec((1, 8, tn), lambda l, j: (l, 0, j)),
        out_shape=jax.ShapeDtypeStruct((depth, 8, n), F32),
        compiler_params=_cparams(("arbitrary", "arbitrary")),
        name="ada_mod",
    )(cc.T, w_ada, b_ada.reshape(depth, 1, n))


def _row_select(mod_ref, chunk, row_is_ctx):
    lo = chunk * D_MODEL
    return jnp.where(row_is_ctx, mod_ref[1:2, lo:lo + D_MODEL], mod_ref[0:1, lo:lo + D_MODEL])


def _norm_body(x_ref, nw_ref):
    x = x_ref[...]
    ms = jnp.mean(x * x, axis=-1, keepdims=True)
    return x * lax.rsqrt(ms + RMS_EPS) * nw_ref[...]


def _norm_mod_kernel(x_ref, nw_ref, mod_ref, o_ref, *, n_ctx, tm, sh_chunk):
    y = _norm_body(x_ref, nw_ref)
    row = pl.program_id(0) * tm + lax.broadcasted_iota(jnp.int32, (tm, 1), 0)
    is_ctx = row < n_ctx
    sh = _row_select(mod_ref, sh_chunk, is_ctx)
    sc = _row_select(mod_ref, sh_chunk + 1, is_ctx)
    o_ref[...] = (y * (1.0 + sc) + sh).astype(o_ref.dtype)


def norm_mod(x, nw, mod, n_ctx, sh_chunk, tm=768):
    m, d = x.shape
    return pl.pallas_call(
        functools.partial(_norm_mod_kernel, n_ctx=n_ctx, tm=tm, sh_chunk=sh_chunk),
        grid=(m // tm,),
        in_specs=[pl.BlockSpec((tm, d), lambda i: (i, 0)),
                  pl.BlockSpec((1, d), lambda i: (0, 0)),
                  pl.BlockSpec(mod.shape, lambda i: (0, 0))],
        out_specs=pl.BlockSpec((tm, d), lambda i: (i, 0)),
        out_shape=jax.ShapeDtypeStruct((m, d), BF16),
        compiler_params=_cparams(("arbitrary",)),
        name="norm_mod",
    )(x, nw.reshape(1, d), mod)


def _norm_router_kernel(x_ref, nw_ref, mod_ref, wr_ref, h_ref, aff_ref, *, n_ctx, tm, sh_chunk):
    y = _norm_body(x_ref, nw_ref)
    row = pl.program_id(0) * tm + lax.broadcasted_iota(jnp.int32, (tm, 1), 0)
    is_ctx = row < n_ctx
    sh = _row_select(mod_ref, sh_chunk, is_ctx)
    sc = _row_select(mod_ref, sh_chunk + 1, is_ctx)
    h = y * (1.0 + sc) + sh
    h_ref[...] = h
    logits = jnp.dot(h.astype(BF16), wr_ref[...].astype(BF16), preferred_element_type=F32)
    lane = lax.broadcasted_iota(jnp.int32, logits.shape, 1)
    logits = jnp.where(lane < N_EXPERTS, logits, -jnp.inf)
    mx = jnp.max(logits, axis=-1, keepdims=True)
    e = jnp.exp(logits - mx)
    aff_ref[...] = e / jnp.sum(e, axis=-1, keepdims=True)


def norm_router(x, nw, mod, w_router_pad, n_ctx, sh_chunk, tm=768):
    m, d = x.shape
    return pl.pallas_call(
        functools.partial(_norm_router_kernel, n_ctx=n_ctx, tm=tm, sh_chunk=sh_chunk),
        grid=(m // tm,),
        in_specs=[pl.BlockSpec((tm, d), lambda i: (i, 0)),
                  pl.BlockSpec((1, d), lambda i: (0, 0)),
                  pl.BlockSpec(mod.shape, lambda i: (0, 0)),
                  pl.BlockSpec((d, LANES), lambda i: (0, 0))],
        out_specs=[pl.BlockSpec((tm, d), lambda i: (i, 0)),
                   pl.BlockSpec((tm, LANES), lambda i: (i, 0))],
        out_shape=[jax.ShapeDtypeStruct((m, d), F32),
                   jax.ShapeDtypeStruct((m, LANES), F32)],
        compiler_params=_cparams(("arbitrary",)),
        name="norm_router",
    )(x, nw.reshape(1, d), mod, w_router_pad)


def _final_norm_kernel(x_ref, nw_ref, o_ref):
    o_ref[...] = _norm_body(x_ref, nw_ref)


def final_norm(x, nw, row0, n_rows, tm=256):
    d = x.shape[1]
    off = row0 // tm
    return pl.pallas_call(
        _final_norm_kernel,
        grid=(n_rows // tm,),
        in_specs=[pl.BlockSpec((tm, d), lambda i: (i + off, 0)),
                  pl.BlockSpec((1, d), lambda i: (0, 0))],
        out_specs=pl.BlockSpec((tm, d), lambda i: (i, 0)),
        out_shape=jax.ShapeDtypeStruct((n_rows, d), F32),
        compiler_params=_cparams(("arbitrary",)),
        name="final_norm",
    )(x, nw.reshape(1, d))


_NT = (((1,), (1,)), ((), ()))


def _wt_block(layer, row0, tn, k):
    return pl.BlockSpec((pl.Element(1), pl.Element(tn), pl.Element(k)),
                        lambda j, i: (layer, pl.multiple_of(row0 + j * tn, 16), 0))


def _mm_wt_kernel(a_ref, wt_ref, o_ref):
    w = wt_ref[0].astype(BF16)
    o_ref[...] = lax.dot_general(a_ref[...], w, _NT, preferred_element_type=F32).astype(o_ref.dtype)


def matmul_wt(a, wt_all, layer, col0, ncols, out_dtype, tm, tn, name, out_perm=None):
    m, k = a.shape
    nb = ncols // tn
    perm = tuple(range(nb)) if out_perm is None else tuple(out_perm)

    def out_map(j, i):
        pj = j
        for src, dst in enumerate(perm):
            pj = jnp.where(j == src, dst, pj)
        return (i, pj)

    return pl.pallas_call(
        _mm_wt_kernel,
        grid=(nb, m // tm),
        in_specs=[pl.BlockSpec((tm, k), lambda j, i: (i, 0)), _wt_block(layer, col0, tn, k)],
        out_specs=pl.BlockSpec((tm, tn), out_map),
        out_shape=jax.ShapeDtypeStruct((m, ncols), out_dtype),
        compiler_params=_cparams(("arbitrary", "arbitrary")),
        name=name,
    )(a, wt_all)


def _mm_residual_kernel(a_ref, w_ref, x_ref, g_ref, o_ref, *, n_ctx, tm):
    acc = jnp.dot(a_ref[...].astype(BF16), w_ref[0].astype(BF16), preferred_element_type=F32)
    row = pl.program_id(1) * tm + lax.broadcasted_iota(jnp.int32, (tm, 1), 0)
    g = jnp.where(row < n_ctx, g_ref[1:2, :], g_ref[0:1, :])
    o_ref[...] = x_ref[...] + g * acc


def matmul_residual(a, w_all, layer, x, mod, gate_chunk, n_ctx, tm, tn, name):
    m, k = a.shape
    n = w_all.shape[2]
    goff = gate_chunk * D_MODEL // tn
    return pl.pallas_call(
        functools.partial(_mm_residual_kernel, n_ctx=n_ctx, tm=tm),
        grid=(n // tn, m // tm),
        in_specs=[pl.BlockSpec((tm, k), lambda j, i: (i, 0)),
                  pl.BlockSpec((1, k, tn), lambda j, i: (layer, 0, j)),
                  pl.BlockSpec((tm, tn), lambda j, i: (i, j)),
                  pl.BlockSpec((8, tn), lambda j, i: (0, goff + j))],
        out_specs=pl.BlockSpec((tm, tn), lambda j, i: (i, j)),
        out_shape=jax.ShapeDtypeStruct((m, n), F32),
        compiler_params=_cparams(("arbitrary", "arbitrary")),
        name=name,
    )(a, w_all, x, mod)


def _softmax_pv(parts):
    m = None
    for s, _ in parts:
        mi = jnp.max(s, axis=-1, keepdims=True)
        m = mi if m is None else jnp.maximum(m, mi)
    ps, l = [], None
    for s, _ in parts:
        p = jnp.exp(s - m)
        ps.append(p)
        li = jnp.sum(p, axis=-1, keepdims=True)
        l = li if l is None else l + li
    inv = 1.0 / l
    o = None
    for p, (_, v) in zip(ps, parts):
        oi = jnp.dot((p * inv).astype(BF16), v, preferred_element_type=F32)
        o = oi if o is None else o + oi
    return o


def _qk(q, k):
    return lax.dot_general(q, k, (((1,), (1,)), ((), ())), preferred_element_type=F32)


NA_QROWS = 4
NA_KROWS = NA_WIN_R + NA_QROWS - 1


def _na_kernel(tid_ref, q_ref, k_ref, v_ref, b_ref, o_ref, *, n_ctx, rows):
    del tid_ref
    step = pl.program_id(0)
    tq = NA_QROWS * GRID_W
    n_ctx_steps = n_ctx // tq
    scale = HEAD_DIM ** -0.5
    win = NA_KROWS * GRID_W

    @pl.when(step < n_ctx_steps)
    def _ctx():
        for h in range(NA_HEADS):
            cs = slice(h * HEAD_DIM, (h + 1) * HEAD_DIM)
            q = q_ref[:, cs]
            s_c = _qk(q, k_ref[0:n_ctx, cs]) * scale
            o_ref[:, cs] = _softmax_pv([(s_c, v_ref[0:n_ctx, cs])]).astype(o_ref.dtype)

    @pl.when(step >= n_ctx_steps)
    def _lat():
        p = step - n_ctx_steps
        row0 = jnp.clip(NA_QROWS * p - NA_WIN_R // 2, 0, rows - NA_KROWS)
        base = pl.multiple_of(n_ctx + row0 * GRID_W, GRID_W)
        for h in range(NA_HEADS):
            cs = slice(h * HEAD_DIM, (h + 1) * HEAD_DIM)
            q = q_ref[:, cs]
            s_w = _qk(q, k_ref[pl.ds(base, win), cs]) * scale + b_ref[0, h]
            s_c = _qk(q, k_ref[0:n_ctx, cs]) * scale
            o = _softmax_pv([(s_w, v_ref[pl.ds(base, win), cs]), (s_c, v_ref[0:n_ctx, cs])])
            o_ref[:, cs] = o.astype(o_ref.dtype)


def _na_patterns(rows):
    a = np.arange(NA_QROWS)[:, None]
    i = np.arange(NA_KROWS)[None, :]
    pats, keys, tid = [], [], []
    for p in range(rows // NA_QROWS):
        row0 = int(np.clip(NA_QROWS * p - NA_WIN_R // 2, 0, rows - NA_KROWS))
        r = NA_QROWS * p + a
        rs = np.clip(r - NA_WIN_R // 2, 0, rows - NA_WIN_R)
        krow = row0 + i
        valid = (krow >= rs) & (krow < rs + NA_WIN_R)
        ridx = np.clip(krow - r + NA_WIN_R - 1, 0, 2 * NA_WIN_R - 2)
        key = (valid.tobytes(), ridx.tobytes())
        if key not in keys:
            keys.append(key)
            pats.append((valid, ridx))
        tid.append(keys.index(key))
    return np.stack([v for v, _ in pats]), np.stack([x for _, x in pats]), np.asarray(tid, np.int32)


def na_bias_table(rpb, rows):
    valid_r, ridx, tid = _na_patterns(rows)
    kc = np.arange(GRID_W)[None, :]
    qc = np.arange(GRID_W)[:, None]
    col_start = np.clip(qc - NA_WIN_C // 2, 0, GRID_W - NA_WIN_C)
    valid_c = (kc >= col_start) & (kc < col_start + NA_WIN_C)
    cidx = np.clip(kc - qc + NA_WIN_C - 1, 0, 2 * NA_WIN_C - 2)
    nh, nbr, nbc = rpb.shape
    onehot = (np.arange(nbc)[:, None, None] == cidx[None]) & valid_c[None]
    toe = jnp.einsum('hbj,jqk->hbqk', rpb.astype(F32), jnp.asarray(onehot, F32), precision=lax.Precision.HIGHEST)
    toe = toe + jnp.asarray(np.where(valid_c, 0.0, -np.inf), F32)
    neg = jnp.full((nh, GRID_W, GRID_W), -jnp.inf, F32)
    npat = valid_r.shape[0]
    blocks = [toe[:, int(ridx[p, a, i])] if valid_r[p, a, i] else neg
              for p in range(npat) for a in range(NA_QROWS) for i in range(NA_KROWS)]
    t = jnp.stack(blocks, axis=0).reshape(npat, NA_QROWS, NA_KROWS, nh, GRID_W, GRID_W)
    t = jnp.transpose(t, (0, 3, 1, 4, 2, 5))
    return t.reshape(npat, nh, NA_QROWS * GRID_W, NA_KROWS * GRID_W), jnp.asarray(tid)


def na_attention(ab, bias, tid, n_ctx):
    m = ab.shape[0]
    rows = (m - n_ctx) // GRID_W
    tq = NA_QROWS * GRID_W
    n_ctx_steps = n_ctx // tq
    w = NA_HEADS * HEAD_DIM
    grid_spec = pltpu.PrefetchScalarGridSpec(
        num_scalar_prefetch=1,
        grid=(m // tq,),
        in_specs=[pl.BlockSpec((tq, w), lambda s, t: (s, 0)),
                  pl.BlockSpec((m, w), lambda s, t: (0, 1)),
                  pl.BlockSpec((m, w), lambda s, t: (0, 2)),
                  pl.BlockSpec((1, NA_HEADS, tq, NA_KROWS * GRID_W),
                               lambda s, t: (t[jnp.maximum(s - n_ctx_steps, 0)], 0, 0, 0))],
        out_specs=pl.BlockSpec((tq, w), lambda s, t: (s, 0)),
    )
    return pl.pallas_call(
        functools.partial(_na_kernel, n_ctx=n_ctx, rows=rows),
        grid_spec=grid_spec,
        out_shape=jax.ShapeDtypeStruct((m, w), BF16),
        compiler_params=_cparams(("arbitrary",)),
        name="na_attn",
    )(tid, ab, ab, ab, bias)


def rope_tables(n_ctx, n_lat):
    t = jnp.arange(n_lat)
    pos = jnp.stack([t // GRID_W, t % GRID_W], axis=-1).astype(F32)
    n_freq = HEAD_DIM // 4
    inv = ROPE_THETA ** (-jnp.arange(n_freq, dtype=F32) / n_freq)
    ang = pos[:, :, None] * inv
    c, s = jnp.cos(ang), jnp.sin(ang)
    cos = jnp.concatenate([c[:, 0], c[:, 0], c[:, 1], c[:, 1]], axis=-1)
    sin = jnp.concatenate([-s[:, 0], s[:, 0], -s[:, 1], s[:, 1]], axis=-1)
    cos = jnp.concatenate([jnp.ones((n_ctx, HEAD_DIM), F32), cos], axis=0)
    sin = jnp.concatenate([jnp.zeros((n_ctx, HEAD_DIM), F32), sin], axis=0)
    return cos, sin


def _qk_prep_kernel(x_ref, cos_ref, sin_ref, qg_ref, kg_ref, qt_ref, ko_ref, vt_ref):
    cos = cos_ref[...]
    sin = sin_ref[...]
    lane = lax.broadcasted_iota(jnp.int32, (1, HEAD_DIM), 1)
    first_half = (lane % (HEAD_DIM // 2)) < (HEAD_DIM // 4)
    for h in range(GQA_HEADS + GQA_KV_HEADS):
        x = x_ref[:, h * HEAD_DIM:(h + 1) * HEAD_DIM]
        gain = qg_ref[...] if h < GQA_HEADS else kg_ref[...]
        y = x * lax.rsqrt(jnp.mean(x * x, axis=-1, keepdims=True) + RMS_EPS) * gain
        sw = jnp.where(first_half, pltpu.roll(y, HEAD_DIM - HEAD_DIM // 4, 1), pltpu.roll(y, HEAD_DIM // 4, 1))
        out = y * cos + sw * sin
        if h < GQA_HEADS:
            qt_ref[h * HEAD_DIM:(h + 1) * HEAD_DIM, :] = out.T.astype(BF16)
        else:
            hk = h - GQA_HEADS
            ko_ref[:, hk * HEAD_DIM:(hk + 1) * HEAD_DIM] = out.astype(BF16)
    v0 = (GQA_HEADS + GQA_KV_HEADS) * HEAD_DIM
    for hv in range(GQA_KV_HEADS):
        v = x_ref[:, v0 + hv * HEAD_DIM:v0 + (hv + 1) * HEAD_DIM]
        vt_ref[hv * HEAD_DIM:(hv + 1) * HEAD_DIM, :] = v.T.astype(BF16)


def qk_prep(fr, cos, sin, q_gain, k_gain, tm=768):
    m = fr.shape[0]
    wq, wk = GQA_HEADS * HEAD_DIM, GQA_KV_HEADS * HEAD_DIM
    return pl.pallas_call(
        _qk_prep_kernel,
        grid=(m // tm,),
        in_specs=[pl.BlockSpec((tm, wq + 2 * wk), lambda i: (i, 0)),
                  pl.BlockSpec((tm, HEAD_DIM), lambda i: (i, 0)),
                  pl.BlockSpec((tm, HEAD_DIM), lambda i: (i, 0)),
                  pl.BlockSpec((1, HEAD_DIM), lambda i: (0, 0)),
                  pl.BlockSpec((1, HEAD_DIM), lambda i: (0, 0))],
        out_specs=[pl.BlockSpec((wq, tm), lambda i: (0, i)),
                   pl.BlockSpec((tm, wk), lambda i: (i, 0)),
                   pl.BlockSpec((wk, tm), lambda i: (0, i))],
        out_shape=[jax.ShapeDtypeStruct((wq, m), BF16),
                   jax.ShapeDtypeStruct((m, wk), BF16),
                   jax.ShapeDtypeStruct((wk, m), BF16)],
        compiler_params=_cparams(("arbitrary",)),
        name="qk_prep",
    )(fr, cos, sin, q_gain.reshape(1, HEAD_DIM), k_gain.reshape(1, HEAD_DIM))


def _gqa_kernel(qt_ref, k_ref, vt_ref, o_ref, *, n_ctx, n_all, tq, tk):
    qi = pl.program_id(1)
    c1 = (HEAD_DIM ** -0.5) * LOG2E
    n_ctx_tiles = n_ctx // tq
    rep = GQA_HEADS // GQA_KV_HEADS

    def attend(n_keys, tkk):
        nchunks = n_keys // tkk

        def scores(c):
            kc = k_ref[c * tkk:(c + 1) * tkk, :]
            return [jnp.dot(kc, qt_ref[r * HEAD_DIM:(r + 1) * HEAD_DIM, :], preferred_element_type=F32)
                    for r in range(rep)]

        state = [None] * rep
        nxt = scores(0)
        for c in range(nchunks):
            cur = nxt
            if c + 1 < nchunks:
                nxt = scores(c + 1)
            vtc = vt_ref[:, c * tkk:(c + 1) * tkk]
            for r in range(rep):
                st = cur[r]
                mc = jnp.max(st, axis=0, keepdims=True)
                if state[r] is None:
                    m_new = mc
                    p = jnp.exp2(st * c1 - m_new * c1)
                    l = jnp.sum(p, axis=0, keepdims=True)
                    acc = jnp.dot(vtc, p.astype(BF16), preferred_element_type=F32)
                else:
                    m, l, acc = state[r]
                    m_new = jnp.maximum(m, mc)
                    p = jnp.exp2(st * c1 - m_new * c1)
                    alpha = jnp.exp2((m - m_new) * c1)
                    l = alpha * l + jnp.sum(p, axis=0, keepdims=True)
                    acc = alpha * acc + jnp.dot(vtc, p.astype(BF16), preferred_element_type=F32)
                state[r] = (m_new, l, acc)
        for r in range(rep):
            _, l, acc = state[r]
            o_ref[:, r * HEAD_DIM:(r + 1) * HEAD_DIM] = (acc / l).T.astype(o_ref.dtype)

    @pl.when(qi < n_ctx_tiles)
    def _ctx():
        attend(n_ctx, n_ctx)

    @pl.when(qi >= n_ctx_tiles)
    def _lat():
        attend(n_all, tk)


def gqa_attention(qt, kn, vt, n_ctx, tq=256, tk=1408):
    m = kn.shape[0]
    rep = GQA_HEADS // GQA_KV_HEADS
    if m % tk:
        tk = 256
    return pl.pallas_call(
        functools.partial(_gqa_kernel, n_ctx=n_ctx, n_all=m, tq=tq, tk=tk),
        grid=(GQA_KV_HEADS, m // tq),
        in_specs=[pl.BlockSpec((rep * HEAD_DIM, tq), lambda g, i: (g, i)),
                  pl.BlockSpec((m, HEAD_DIM), lambda g, i: (0, g)),
                  pl.BlockSpec((HEAD_DIM, m), lambda g, i: (g, 0))],
        out_specs=pl.BlockSpec((tq, rep * HEAD_DIM), lambda g, i: (i, g)),
        out_shape=jax.ShapeDtypeStruct((m, GQA_HEADS * HEAD_DIM), BF16),
        compiler_params=_cparams(("arbitrary", "arbitrary")),
        name="gqa_attn",
    )(qt, kn, vt)


def _ssd_prep_kernel(prev_ref, x_ref, next_ref, dtr_ref, cw_ref, cb_ref, dtb_ref, u_ref, dt_ref, *, n_ctx, n_all, tm):
    i = pl.program_id(0)
    lo = i * tm
    hi = lo + tm
    top_ok = jnp.logical_and(lo != 0, lo != n_ctx)
    bot_ok = jnp.logical_and(hi != n_ctx, hi != n_all)
    prev = jnp.where(top_ok, prev_ref[...], 0.0)
    nxt = jnp.where(bot_ok, next_ref[...], 0.0)
    ext = jnp.concatenate([prev, x_ref[...], nxt], axis=0)
    half = SSD_CONV // 2
    acc = None
    for j in range(SSD_CONV):
        sl = ext[8 - half + j:8 - half + j + tm, :]
        term = sl * cw_ref[j:j + 1, :]
        acc = term if acc is None else acc + term
    u_ref[...] = _silu(acc + cb_ref[...])
    dt_ref[...] = jax.nn.softplus(dtr_ref[...] + dtb_ref[...])


def ssd_prep(fzx, fr, conv_w, conv_b, dt_bias, n_ctx, tm=256):
    m = fzx.shape[0]
    nb8 = tm // 8
    last8 = m // 8 - 1
    cw = jnp.concatenate([conv_w, jnp.zeros((8 - SSD_CONV, SSD_XBC_W), F32)], axis=0)
    dtb = jnp.concatenate([dt_bias.reshape(-1), jnp.zeros((LANES - 2 * SSD_HEADS,), F32)]).reshape(1, LANES)
    return pl.pallas_call(
        functools.partial(_ssd_prep_kernel, n_ctx=n_ctx, n_all=m, tm=tm),
        grid=(m // tm,),
        in_specs=[pl.BlockSpec((8, SSD_XBC_W), lambda i: (jnp.maximum(i * nb8 - 1, 0), 0)),
                  pl.BlockSpec((tm, SSD_XBC_W), lambda i: (i, 0)),
                  pl.BlockSpec((8, SSD_XBC_W), lambda i: (jnp.minimum((i + 1) * nb8, last8), 0)),
                  pl.BlockSpec((tm, LANES), lambda i: (i, 0)),
                  pl.BlockSpec((8, SSD_XBC_W), lambda i: (0, 0)),
                  pl.BlockSpec((1, SSD_XBC_W), lambda i: (0, 0)),
                  pl.BlockSpec((1, LANES), lambda i: (0, 0))],
        out_specs=[pl.BlockSpec((tm, SSD_XBC_W), lambda i: (i, 0)),
                   pl.BlockSpec((tm, LANES), lambda i: (i, 0))],
        out_shape=[jax.ShapeDtypeStruct((m, SSD_XBC_W), F32),
                   jax.ShapeDtypeStruct((m, LANES), F32)],
        compiler_params=_cparams(("arbitrary",)),
        name="ssd_prep",
    )(fzx, fzx, fzx, fr, cw, conv_b.reshape(1, SSD_XBC_W), dtb)


def _ssd_chunk(u_ref, dt_ref, alog_ref, y_ref, ht_ref, d):
    ln = SSD_CHUNK
    p = SSD_HEAD_DIM
    ns = SSD_STATE
    epg = SSD_HEADS // SSD_GROUPS
    dt = dt_ref[...]
    if d == 1:
        dt = pltpu.roll(dt, LANES - SSD_HEADS, 1)
    a = dt * (-jnp.exp(alog_ref[d:d + 1, :]))
    li = lax.broadcasted_iota(jnp.int32, (ln, ln), 0)
    si = lax.broadcasted_iota(jnp.int32, (ln, ln), 1)
    mask = (li >= si) if d == 0 else (li <= si)
    a_cum = jnp.dot(mask.astype(F32), a, preferred_element_type=F32, precision=lax.Precision.HIGHEST)
    a_cum_t = a_cum.T
    a_tot = a_cum[ln - 1:ln, :] if d == 0 else a_cum[0:1, :]
    w_all = jnp.exp(a_tot - a_cum)
    ea_all = jnp.exp(a_cum)
    eat = jnp.exp(a_tot)
    for g in range(SSD_GROUPS):
        bg = u_ref[:, SSD_W + g * ns:SSD_W + (g + 1) * ns]
        cg = u_ref[:, SSD_W + SSD_GROUPS * ns + g * ns:SSD_W + SSD_GROUPS * ns + (g + 1) * ns]
        cgb = cg.astype(BF16)
        cb = _qk(cgb, bg.astype(BF16))
        bgt = bg.T.astype(BF16)
        for e in range(epg):
            h = g * epg + e
            ac = a_cum[:, h:h + 1]
            act = a_cum_t[h:h + 1, :]
            decay = jnp.exp(jnp.where(mask, ac - act, -jnp.inf))
            mm = (cb * decay).astype(BF16)
            xdt = u_ref[:, h * p:(h + 1) * p] * dt[:, h:h + 1]
            ht = ht_ref[d, h]
            y = (jnp.dot(mm, xdt.astype(BF16), preferred_element_type=F32)
                 + ea_all[:, h:h + 1] * jnp.dot(cgb, ht.astype(BF16), preferred_element_type=F32))
            st = jnp.dot(bgt, (xdt * w_all[:, h:h + 1]).astype(BF16), preferred_element_type=F32)
            ht_ref[d, h] = eat[:, h:h + 1] * ht + st
            y_ref[:, h * p:(h + 1) * p] = y


def _ssd_scan_kernel(uf_ref, dtf_ref, ub_ref, dtb_ref, alog_ref, yf_ref, yb_ref, ht_ref):
    @pl.when(pl.program_id(0) == 0)
    def _init():
        ht_ref[...] = jnp.zeros_like(ht_ref)

    _ssd_chunk(uf_ref, dtf_ref, alog_ref, yf_ref, ht_ref, 0)
    _ssd_chunk(ub_ref, dtb_ref, alog_ref, yb_ref, ht_ref, 1)


def ssd_scan(u, dt, a_log, n_ctx):
    m = u.shape[0]
    nc = m // SSD_CHUNK
    ncc = n_ctx // SSD_CHUNK

    def bwd_chunk(s):
        return jnp.where(s < ncc, ncc - 1 - s, ncc + nc - 1 - s)

    return pl.pallas_call(
        _ssd_scan_kernel,
        grid=(nc,),
        in_specs=[pl.BlockSpec((SSD_CHUNK, SSD_XBC_W), lambda s: (s, 0)),
                  pl.BlockSpec((SSD_CHUNK, LANES), lambda s: (s, 0)),
                  pl.BlockSpec((SSD_CHUNK, SSD_XBC_W), lambda s: (bwd_chunk(s), 0)),
                  pl.BlockSpec((SSD_CHUNK, LANES), lambda s: (bwd_chunk(s), 0)),
                  pl.BlockSpec((2, LANES), lambda s: (0, 0))],
        out_specs=[pl.BlockSpec((SSD_CHUNK, SSD_W), lambda s: (s, 0)),
                   pl.BlockSpec((SSD_CHUNK, SSD_W), lambda s: (bwd_chunk(s), 0))],
        out_shape=[jax.ShapeDtypeStruct((m, SSD_W), F32), jax.ShapeDtypeStruct((m, SSD_W), F32)],
        scratch_shapes=[pltpu.VMEM((2, SSD_HEADS, SSD_STATE, SSD_HEAD_DIM), F32)],
        compiler_params=_cparams(("arbitrary",)),
        name="ssd_scan",
    )(u, dt, u, dt, jnp.concatenate([a_log, jnp.zeros((2, LANES - SSD_HEADS), F32)], axis=1))


def _ssd_out_kernel(yf_ref, yb_ref, xs_ref, z_ref, dsk_ref, nw_ref, o_ref):
    y = yf_ref[...] + yb_ref[...] + dsk_ref[...] * xs_ref[...]
    gated = y * _silu(z_ref[...])
    gw = SSD_W // SSD_GROUPS
    for g in range(SSD_GROUPS):
        blk = gated[:, g * gw:(g + 1) * gw]
        nrm = blk * lax.rsqrt(jnp.mean(blk * blk, axis=-1, keepdims=True) + RMS_EPS)
        o_ref[:, g * gw:(g + 1) * gw] = (nrm * nw_ref[:, g * gw:(g + 1) * gw]).astype(o_ref.dtype)


def ssd_out(yf, yb, u, fb, d_skip, norm_w, tm=768):
    m = u.shape[0]
    dsk = jnp.repeat(d_skip.astype(F32), SSD_HEAD_DIM).reshape(1, SSD_W)
    return pl.pallas_call(
        _ssd_out_kernel,
        grid=(m // tm,),
        in_specs=[pl.BlockSpec((tm, SSD_W), lambda i: (i, 0)),
                  pl.BlockSpec((tm, SSD_W), lambda i: (i, 0)),
                  pl.BlockSpec((tm, SSD_W), lambda i: (i, 0)),
                  pl.BlockSpec((tm, SSD_W), lambda i: (i, ZX_Z // SSD_W)),
                  pl.BlockSpec((1, SSD_W), lambda i: (0, 0)),
                  pl.BlockSpec((1, SSD_W), lambda i: (0, 0))],
        out_specs=pl.BlockSpec((tm, SSD_W), lambda i: (i, 0)),
        out_shape=jax.ShapeDtypeStruct((m, SSD_W), BF16),
        compiler_params=_cparams(("arbitrary",)),
        name="ssd_out",
    )(yf, yb, u, fb, dsk, norm_w.reshape(1, SSD_W))


def _dft_tables(n):
    ang = 2.0 * np.pi * np.outer(np.arange(n), np.arange(n)) / n
    return jnp.asarray(np.cos(ang), F32), jnp.asarray(np.sin(ang), F32)


def _split_bf16(a):
    hi = a.astype(BF16)
    return hi, (a - hi.astype(F32)).astype(BF16)


def _hdot(a, b):
    ah, al = _split_bf16(a)
    bh, bl = _split_bf16(b)
    return (jnp.dot(ah, bh, preferred_element_type=F32)
            + (jnp.dot(ah, bl, preferred_element_type=F32) + jnp.dot(al, bh, preferred_element_type=F32)))


def _fnet_a_kernel(c_ref, s_ref, x_ref, yr_ref, yi_ref):
    x = x_ref[...]
    yr_ref[...] = _hdot(c_ref[...], x)
    yi_ref[...] = -_hdot(s_ref[...], x)


def _fnet_b_kernel(yr_ref, yi_ref, twc_ref, tws_ref, c1_ref, s1_ref, cc_ref, sc_ref, o_ref, *, scale, kb):
    c1, s1 = c1_ref[...], s1_ref[...]
    cc, sc = cc_ref[...], sc_ref[...]
    for j in range(kb):
        yr, yi = yr_ref[j], yi_ref[j]
        tc, ts = twc_ref[j], tws_ref[j]
        pr = yr * tc + yi * ts
        pi = yi * tc - yr * ts
        zr = _hdot(c1, pr) + _hdot(s1, pi)
        zi = _hdot(c1, pi) - _hdot(s1, pr)
        for g in range(FNET_GROUPS):
            cs = slice(g * FNET_GROUP_DIM, (g + 1) * FNET_GROUP_DIM)
            o_ref[:, j, cs] = (_hdot(zr[:, cs], cc) + _hdot(zi[:, cs], sc)) * scale


def _fnet_ctx_kernel(x_ref, cn_ref, sn_ref, cc_ref, sc_ref, o_ref, *, scale):
    x = x_ref[...]
    wr = _hdot(cn_ref[...], x)
    ws = _hdot(sn_ref[...], x)
    cc, sc = cc_ref[...], sc_ref[...]
    for g in range(FNET_GROUPS):
        cs = slice(g * FNET_GROUP_DIM, (g + 1) * FNET_GROUP_DIM)
        o_ref[:, cs] = (_hdot(wr[:, cs], cc) - _hdot(ws[:, cs], sc)) * scale


def fourier_mix(f_lat, f_ctx):
    n_lat, w = f_lat.shape
    n_ctx = f_ctx.shape[0]
    n2 = 128
    n1 = n_lat // n2
    c2, s2 = _dft_tables(n2)
    c1, s1 = _dft_tables(n1)
    cc, sc = _dft_tables(FNET_GROUP_DIM)
    tw = 2.0 * np.pi * np.outer(np.arange(n2), np.arange(n1)) / n_lat
    twc = jnp.asarray(np.cos(tw), F32).reshape(n2, n1, 1)
    tws = jnp.asarray(np.sin(tw), F32).reshape(n2, n1, 1)
    xr = f_lat.reshape(n2, n1 * w)
    tn = min(4096, n1 * w)
    yr, yi = pl.pallas_call(
        _fnet_a_kernel,
        grid=(n1 * w // tn,),
        in_specs=[pl.BlockSpec((n2, n2), lambda j: (0, 0)),
                  pl.BlockSpec((n2, n2), lambda j: (0, 0)),
                  pl.BlockSpec((n2, tn), lambda j: (0, j))],
        out_specs=[pl.BlockSpec((n2, tn), lambda j: (0, j)),
                   pl.BlockSpec((n2, tn), lambda j: (0, j))],
        out_shape=[jax.ShapeDtypeStruct((n2, n1 * w), F32)] * 2,
        compiler_params=_cparams(("arbitrary",)),
        name="fnet_stage_a",
    )(c2, s2, xr)
    kb = 8
    lat = pl.pallas_call(
        functools.partial(_fnet_b_kernel, scale=float(1.0 / math.sqrt(n_lat * FNET_GROUP_DIM)), kb=kb),
        grid=(n2 // kb,),
        in_specs=[pl.BlockSpec((kb, n1, w), lambda j: (j, 0, 0)),
                  pl.BlockSpec((kb, n1, w), lambda j: (j, 0, 0)),
                  pl.BlockSpec((kb, n1, 1), lambda j: (j, 0, 0)),
                  pl.BlockSpec((kb, n1, 1), lambda j: (j, 0, 0)),
                  pl.BlockSpec((n1, n1), lambda j: (0, 0)),
                  pl.BlockSpec((n1, n1), lambda j: (0, 0)),
                  pl.BlockSpec((FNET_GROUP_DIM, FNET_GROUP_DIM), lambda j: (0, 0)),
                  pl.BlockSpec((FNET_GROUP_DIM, FNET_GROUP_DIM), lambda j: (0, 0))],
        out_specs=pl.BlockSpec((n1, kb, w), lambda j: (0, j, 0)),
        out_shape=jax.ShapeDtypeStruct((n1, n2, w), F32),
        compiler_params=_cparams(("arbitrary",)),
        name="fnet_stage_b",
    )(yr.reshape(n2, n1, w), yi.reshape(n2, n1, w), twc, tws, c1, s1, cc, sc)
    cn, sn = _dft_tables(n_ctx)
    ctx = pl.pallas_call(
        functools.partial(_fnet_ctx_kernel, scale=float(1.0 / math.sqrt(n_ctx * FNET_GROUP_DIM))),
        out_shape=jax.ShapeDtypeStruct((n_ctx, w), F32),
        compiler_params=pltpu.CompilerParams(vmem_limit_bytes=VMEM_LIMIT_BYTES),
        name="fnet_ctx",
    )(f_ctx, cn, sn, cc, sc)
    return jnp.concatenate([ctx, lat.reshape(n_lat, w)], axis=0)


def _merge_kernel(h_ref, *refs):
    wg_refs, o_refs = refs[0:4], refs[4:8]
    wbr_ref, y_ref, ws_ref = refs[8], refs[9], refs[10]

    @pl.when(pl.program_id(1) == 0)
    def _convert():
        for b in range(N_BRANCH):
            ws_ref[b] = wg_refs[b][0].astype(BF16)

    h = h_ref[...]
    y = None
    for b in range(N_BRANCH):
        g = lax.dot_general(h, ws_ref[b], _NT, preferred_element_type=F32)
        pr = jnp.dot(o_refs[b][...].astype(BF16), wbr_ref[0, b].astype(BF16), preferred_element_type=F32)
        t = jax.nn.sigmoid(g) * pr
        y = t if y is None else y + t
    y_ref[...] = y.astype(y_ref.dtype)


def merge(h, wt_all, gate_col0, branches, w_branch_all, layer, tm=768, tn=256):
    m, k = h.shape
    nb = D_MODEL // tn
    wg_specs = [_wt_block(layer, gate_col0 + b * D_MODEL, tn, k) for b in range(N_BRANCH)]
    o_specs = [pl.BlockSpec((tm, BRANCH_W), lambda j, i: (i, 0)) for _ in range(N_BRANCH)]
    return pl.pallas_call(
        _merge_kernel,
        grid=(nb, m // tm),
        in_specs=([pl.BlockSpec((tm, k), lambda j, i: (i, 0))] + wg_specs + o_specs
                  + [pl.BlockSpec((1, N_BRANCH, BRANCH_W, tn), lambda j, i: (layer, 0, 0, j))]),
        out_specs=pl.BlockSpec((tm, tn), lambda j, i: (i, j)),
        out_shape=jax.ShapeDtypeStruct((m, D_MODEL), BF16),
        scratch_shapes=[pltpu.VMEM((N_BRANCH, tn, k), BF16)],
        compiler_params=_cparams(("arbitrary", "arbitrary")),
        name="merge",
    )(h, *([wt_all] * N_BRANCH), *branches, w_branch_all)


def _expert_kernel(idx_ref, h_hbm, wg_ref, wu_ref, wd_ref, gate_ref, o_ref, xs_ref, sem, *, nf):
    e = pl.program_id(0)
    f = pl.program_id(1)
    cap = xs_ref.shape[1]
    slot = e % 2

    def start_gather(ee, sl):
        def issue(c, carry):
            row = idx_ref[ee, c]
            pltpu.make_async_copy(h_hbm.at[pl.ds(row, 1)], xs_ref.at[sl, pl.ds(c, 1)], sem.at[sl]).start()
            return carry

        lax.fori_loop(0, cap, issue, 0, unroll=8)

    @pl.when(jnp.logical_and(e == 0, f == 0))
    def _cold_start():
        start_gather(0, 0)

    @pl.when(f == 0)
    def _wait_rows():
        pltpu.make_async_copy(h_hbm.at[pl.ds(0, cap)], xs_ref.at[slot], sem.at[slot]).wait()

    @pl.when(jnp.logical_and(f == 0, e + 1 < pl.num_programs(0)))
    def _prefetch_next():
        start_gather(e + 1, 1 - slot)

    xs = xs_ref[slot].astype(BF16)
    hg = jnp.dot(xs, wg_ref[0, 0].astype(BF16), preferred_element_type=F32)
    hu = jnp.dot(xs, wu_ref[0, 0].astype(BF16), preferred_element_type=F32)
    hid = (_silu(hg) * hu).astype(BF16)
    part = jnp.dot(hid, wd_ref[0, 0].astype(BF16), preferred_element_type=F32)

    @pl.when(f == 0)
    def _first():
        o_ref[0] = part

    @pl.when(f != 0)
    def _rest():
        o_ref[0] += part

    @pl.when(f == nf - 1)
    def _done():
        o_ref[0] = o_ref[0] * gate_ref[0]


def expert_ffn(h2, idx, gate, w_gate, w_up, w_down, layer, tf=256):
    ne, cap = idx.shape
    d = h2.shape[1]
    ff = w_gate.shape[3]
    grid_spec = pltpu.PrefetchScalarGridSpec(
        num_scalar_prefetch=1,
        grid=(ne, ff // tf),
        in_specs=[pl.BlockSpec(memory_space=pl.ANY),
                  pl.BlockSpec((1, 1, d, tf), lambda e, f, ix: (layer, e, 0, f)),
                  pl.BlockSpec((1, 1, d, tf), lambda e, f, ix: (layer, e, 0, f)),
                  pl.BlockSpec((1, 1, tf, d), lambda e, f, ix: (layer, e, f, 0)),
                  pl.BlockSpec((1, cap, 1), lambda e, f, ix: (e, 0, 0))],
        out_specs=pl.BlockSpec((1, cap, d), lambda e, f, ix: (e, 0, 0)),
        scratch_shapes=[pltpu.VMEM((2, cap, d), F32), pltpu.SemaphoreType.DMA((2,))],
    )
    return pl.pallas_call(
        functools.partial(_expert_kernel, nf=ff // tf),
        grid_spec=grid_spec,
        out_shape=jax.ShapeDtypeStruct((ne, cap, d), F32),
        compiler_params=_cparams(("arbitrary", "arbitrary")),
        name="expert_ffn",
    )(idx, h2, w_gate, w_up, w_down, gate.reshape(ne, cap, 1))


COMBINE_TOKENS = 64
COMBINE_WAIT_ROWS = 16
COMBINE_ISSUE_GROUP = 4


def _combine_kernel(src_ref, dst_ref, tstart_ref, rmax_ref, ye_hbm, x_ref, g_ref, o_ref, s_ref, sem, *, n_ctx,
                    ntiles):
    t = pl.program_id(0)
    tt = COMBINE_TOKENS
    d = s_ref.shape[2]
    buf = t % 2

    def fill(tile, b):
        def zero(r, carry):
            s_ref[b, pl.ds(pl.multiple_of(r * tt, tt), tt), :] = jnp.zeros((tt, d), F32)
            return carry

        lax.fori_loop(0, rmax_ref[tile], zero, 0)

        def start_row(p):
            pltpu.make_async_copy(ye_hbm.at[pl.ds(src_ref[p], 1)], s_ref.at[b, pl.ds(dst_ref[p], 1)],
                                  sem.at[b]).start()

        p_lo = tstart_ref[tile]
        n = tstart_ref[tile + 1] - p_lo
        grp = COMBINE_ISSUE_GROUP

        def issue_group(i, carry):
            for u in range(grp):
                start_row(p_lo + i * grp + u)
            return carry

        def issue_one(i, carry):
            start_row(p_lo + (n // grp) * grp + i)
            return carry

        lax.fori_loop(0, n // grp, issue_group, 0)
        lax.fori_loop(0, n % grp, issue_one, 0)

    @pl.when(t == 0)
    def _first():
        fill(0, 0)

    @pl.when(t + 1 < ntiles)
    def _next():
        fill(t + 1, 1 - buf)

    def wait_rows(nrows):
        def wait(p, carry):
            pltpu.make_async_copy(ye_hbm.at[pl.ds(0, nrows)], s_ref.at[buf, pl.ds(0, nrows)], sem.at[buf]).wait()
            return carry
        return wait

    npairs = tstart_ref[t + 1] - tstart_ref[t]
    lax.fori_loop(0, npairs // COMBINE_WAIT_ROWS, wait_rows(COMBINE_WAIT_ROWS), 0)
    lax.fori_loop(0, npairs % COMBINE_WAIT_ROWS, wait_rows(1), 0)

    def add(r, acc):
        return acc + s_ref[buf, pl.ds(pl.multiple_of(r * tt, tt), tt), :]

    acc = lax.fori_loop(0, rmax_ref[t], add, jnp.zeros((tt, d), F32))
    row = t * tt + lax.broadcasted_iota(jnp.int32, (tt, 1), 0)
    g = jnp.where(row < n_ctx, g_ref[1:2, :], g_ref[0:1, :])
    o_ref[...] = x_ref[...] + g * acc


def moe_combine(x, ye, idx, mod, gate_chunk, n_ctx):
    m, d = x.shape
    ne, cap = idx.shape
    npairs = ne * cap
    tt = COMBINE_TOKENS
    ntiles = m // tt
    tok = idx.reshape(-1)
    tok_s, src_s = lax.sort((tok, jnp.arange(npairs, dtype=jnp.int32)), num_keys=1)
    pos = jnp.arange(npairs, dtype=jnp.int32)
    is_first = jnp.concatenate([jnp.ones((1,), bool), tok_s[1:] != tok_s[:-1]])
    first = lax.cummax(jnp.where(is_first, pos, 0), axis=0)
    rank = pos - first
    dst = rank * tt + tok_s % tt
    tile_s = tok_s // tt
    tiles = jnp.arange(ntiles + 1, dtype=jnp.int32)
    tstart = jnp.sum((tile_s[None, :] < tiles[:, None]).astype(jnp.int32), axis=1)
    rmax = jnp.max(jnp.where(tile_s[None, :] == tiles[:ntiles, None], rank[None, :] + 1, 0), axis=1)
    grid_spec = pltpu.PrefetchScalarGridSpec(
        num_scalar_prefetch=4,
        grid=(ntiles,),
        in_specs=[pl.BlockSpec(memory_space=pl.ANY),
                  pl.BlockSpec((tt, d), lambda t, *_: (t, 0)),
                  pl.BlockSpec((8, d), lambda t, *_: (0, gate_chunk))],
        out_specs=pl.BlockSpec((tt, d), lambda t, *_: (t, 0)),
        scratch_shapes=[pltpu.VMEM((2, ne * tt, d), F32), pltpu.SemaphoreType.DMA((2,))],
    )
    return pl.pallas_call(
        functools.partial(_combine_kernel, n_ctx=n_ctx, ntiles=ntiles),
        grid_spec=grid_spec,
        out_shape=jax.ShapeDtypeStruct((m, d), F32),
        compiler_params=_cparams(("arbitrary",)),
        name="moe_combine",
    )(src_s, dst, tstart, rmax, ye.reshape(npairs, d), x, mod)


def moe(x, h2, aff, mod, w_gate, w_up, w_down, layer, n_ctx):
    m = h2.shape[0]
    n_lat = m - n_ctx
    a = aff[:, :N_EXPERTS]
    g_c, i_c = lax.top_k(a[:n_ctx].T, CAPACITY_FACTOR * n_ctx // N_EXPERTS)
    g_l, i_l = lax.top_k(a[n_ctx:].T, CAPACITY_FACTOR * n_lat // N_EXPERTS)
    idx = jnp.concatenate([i_c, i_l + n_ctx], axis=1).astype(jnp.int32)
    gate = jnp.concatenate([g_c, g_l], axis=1)
    ye = expert_ffn(h2, idx, gate, w_gate, w_up, w_down, layer)
    return moe_combine(x, ye, idx, mod, 5, n_ctx)


_P_NA = 0
_P_Z = _P_NA + 3 * NA_HEADS * HEAD_DIM
_P_XBC = _P_Z + SSD_W
_P_DT = _P_XBC + SSD_XBC_W
_P_GQ = _P_DT + 2 * SSD_HEADS
_P_FN = _P_GQ + (GQA_HEADS + 2 * GQA_KV_HEADS) * HEAD_DIM
_P_GATES = _P_FN + FNET_GROUPS * FNET_GROUP_DIM


def _layer(x, mod, n_ctx, layer, norm_mix, norm_ffn, wt_in_all, na_rpb, conv_w, conv_b, a_log, dt_bias, d_skip,
           ssd_gn, q_gain, k_gain, w_branch_all, w_out_all, w_router, w_gate_all, w_up_all, w_down_all, rope):
    m = x.shape[0]
    tm = 768 if m % 768 == 0 else 256
    h = norm_mod(x, norm_mix, mod, n_ctx, 0, tm=tm)
    ab = matmul_wt(h, wt_in_all, layer, _P_NA, _P_Z - _P_NA, BF16, tm, 768, "proj_na")
    fzx = matmul_wt(h, wt_in_all, layer, _P_Z, _P_DT - _P_Z, F32, tm, 512, "proj_zx", out_perm=(2, 0, 1))
    fr = matmul_wt(h, wt_in_all, layer, _P_GQ, _P_GATES - _P_GQ, F32, tm, 512, "proj_rest")
    fdt = matmul_wt(h, wt_in_all, layer, _P_DT, LANES, F32, tm, LANES, "proj_dt")

    rows = (m - n_ctx) // GRID_W
    bias, tid = na_bias_table(na_rpb, rows)
    o_na = na_attention(ab, bias, tid, n_ctx)

    u, dt = ssd_prep(fzx, fdt, conv_w, conv_b, dt_bias, n_ctx)
    yf, yb = ssd_scan(u, dt, a_log, n_ctx)
    o_ssd = ssd_out(yf, yb, u, fzx, d_skip, ssd_gn, tm=tm)

    qt, kn, vt = qk_prep(fr, rope[0], rope[1], q_gain, k_gain, tm=tm)
    o_gqa = gqa_attention(qt, kn, vt, n_ctx)

    f_in = fr[:, FR_FN:FR_FN + FNET_GROUPS * FNET_GROUP_DIM]
    o_fn = fourier_mix(f_in[n_ctx:], f_in[:n_ctx])

    y = merge(h, wt_in_all, _P_GATES, (o_na, o_ssd, o_gqa, o_fn), w_branch_all, layer, tm=tm)
    x = matmul_residual(y, w_out_all, layer, x, mod, 2, n_ctx, tm, 1024, "out_proj")

    wr = jnp.concatenate([w_router, jnp.zeros((D_MODEL, LANES - N_EXPERTS), w_router.dtype)], axis=1)
    h2, aff = norm_router(x, norm_ffn, mod, wr, n_ctx, 3, tm=tm)
    return moe(x, h2, aff, mod, w_gate_all, w_up_all, w_down_all, layer, n_ctx)


def kernel(x, c, ctx, c_ctx, w_ada, b_ada, norm_mix, norm_ffn, w_in, na_rpb, ssd_conv_w, ssd_conv_b, ssd_a_log,
           ssd_dt_bias, ssd_d, ssd_norm, gqa_q_norm, gqa_k_norm, w_branch, w_out, w_router, moe_w_gate, moe_w_up,
           moe_w_down, final_norm_w):
    n_lat = x.shape[1]
    n_ctx = ctx.shape[1]
    depth = w_ada.shape[0]
    cc = jnp.concatenate([c[0:1], c_ctx[None], jnp.zeros((6, D_MODEL), F32)], axis=0)
    mods = ada_all(cc, w_ada, b_ada)
    rope = rope_tables(n_ctx, n_lat)
    xs = jnp.concatenate([ctx[0], x[0]], axis=0)
    wt_in = jnp.swapaxes(w_in, 1, 2)
    for l in range(depth):
        xs = _layer(xs, mods[l], n_ctx, l, norm_mix[l], norm_ffn[l], wt_in, na_rpb[l], ssd_conv_w[l],
                    ssd_conv_b[l], ssd_a_log[l], ssd_dt_bias[l], ssd_d[l], ssd_norm[l], gqa_q_norm[l],
                    gqa_k_norm[l], w_branch, w_out, w_router[l], moe_w_gate, moe_w_up, moe_w_down, rope)
    out = final_norm(xs, final_norm_w, n_ctx, n_lat)
    return out[None]
```
